```python
import math
import jax, jax.numpy as jnp
from jax import lax
import numpy as np

D_MODEL = 1024
BATCH = 4
SEQ = 8192
DEPTH = 2

MEM_LEN = 256
D_FF = 2816
CONV_CH = 256
CONV_W = 31
SC_CH = 256
SC_W = 3
N_HEADS = 8
HEAD_DIM = 64
ATT_W = N_HEADS * HEAD_DIM
IDX_HEADS = 4
IDX_DIM = 64
TOPK_MAX = 256
Q_BLOCK = 128
ROPE_THETA = 10000.0
XA_HEADS = 4
XA_DIM = D_MODEL // XA_HEADS
MIX_W = CONV_CH + SC_CH + ATT_W
SPLIT_SIZES = (CONV_CH, CONV_CH,
               SC_CH, SC_CH, SC_CH,
               ATT_W, ATT_W, ATT_W,
               IDX_HEADS * IDX_DIM, IDX_DIM, IDX_HEADS)
N_IN = sum(SPLIT_SIZES)
SPLIT_POINTS = tuple(int(v) for v in np.cumsum(SPLIT_SIZES)[:-1])
IDX_SCALE = (IDX_HEADS ** -0.5) * (IDX_DIM ** -0.5)

kernel_name = "hymba_style_conformer_shortconv_dsa_hybrid"


def rms_norm(x, g, eps=1e-6):
    xf = x.astype(jnp.float32)
    y = xf * lax.rsqrt(jnp.mean(xf * xf, axis=-1, keepdims=True) + eps)
    return (y * g.astype(jnp.float32)).astype(x.dtype)


def layer_norm(x, g, b, eps=1e-5):
    xf = x.astype(jnp.float32)
    mu = jnp.mean(xf, axis=-1, keepdims=True)
    var = jnp.mean(jnp.square(xf - mu), axis=-1, keepdims=True)
    y = (xf - mu) * lax.rsqrt(var + eps)
    return (y * g.astype(jnp.float32) + b.astype(jnp.float32)).astype(x.dtype)


def swiglu(h, w_gate, w_up, w_down):
    return (jax.nn.silu(h @ w_gate) * (h @ w_up)) @ w_down


def causal_dwconv(x, w):
    width, ch = w.shape
    return lax.conv_general_dilated(
        x, w[:, None, :].astype(x.dtype), window_strides=(1,), padding=[(width - 1, 0)],
        dimension_numbers=('NWC', 'WIO', 'NWC'), feature_group_count=ch)


def rope_tables(positions, dim):
    inv_freq = ROPE_THETA ** (-jnp.arange(0, dim, 2, dtype=jnp.float32) / dim)
    ang = positions.astype(jnp.float32)[..., None] * inv_freq
    return jnp.cos(ang), jnp.sin(ang)


def apply_rope(t, cos, sin):
    tf = t.astype(jnp.float32)
    half = tf.shape[-1] // 2
    t1, t2 = tf[..., :half], tf[..., half:]
    c, s = cos[:, :, None, :], sin[:, :, None, :]
    return jnp.concatenate([t1 * c - t2 * s, t2 * c + t1 * s], axis=-1).astype(t.dtype)


def conformer_conv(a_val, a_gate, dw, dw_b, ln_g, ln_b):
    h = a_val * jax.nn.sigmoid(a_gate)
    h = causal_dwconv(h, dw) + dw_b
    return jax.nn.silu(layer_norm(h, ln_g, ln_b))


def short_gated_conv(b_gate, c_gate, h, w):
    return b_gate * causal_dwconv(c_gate * h, w)


def dsa_attention(q, k, v, q_idx, k_idx, w_idx):
    bsz, seq, nh, dh = q.shape
    topk = min(TOPK_MAX, seq // 4)
    nb = seq // Q_BLOCK
    key_pos = jnp.arange(seq)
    b_ar = jnp.arange(bsz)[:, None, None]

    def to_blocks(t):
        return jnp.moveaxis(t.reshape(bsz, nb, Q_BLOCK, *t.shape[2:]), 1, 0)

    q_pos = jnp.arange(seq).reshape(nb, Q_BLOCK)

    def one_block(args):
        qb, qib, wb, tb = args
        logits = jnp.einsum('bqhd,bsd->bqhs', qib, k_idx).astype(jnp.float32)
        score = jnp.einsum('bqh,bqhs->bqs', wb.astype(jnp.float32), jax.nn.relu(logits)) * IDX_SCALE
        causal = key_pos[None, :] <= tb[:, None]
        score = jnp.where(causal[None], score, -jnp.inf)
        top_val, top_idx = lax.top_k(score, topk)
        k_sel = k[b_ar, top_idx]
        v_sel = v[b_ar, top_idx]
        s = jnp.einsum('bqhd,bqkhd->bqhk', qb, k_sel).astype(jnp.float32) * (dh ** -0.5)
        s = jnp.where(jnp.isfinite(top_val)[:, :, None, :], s, -jnp.inf)
        p = jax.nn.softmax(s, axis=-1).astype(v.dtype)
        return jnp.einsum('bqhk,bqkhd->bqhd', p, v_sel)

    out = lax.map(one_block, (to_blocks(q), to_blocks(q_idx), to_blocks(w_idx), q_pos))
    return jnp.moveaxis(out, 0, 1).reshape(bsz, seq, nh * dh)


def memory_cross_attention(h, mem_n, wq, wkv, wo):
    bsz, seq, _ = h.shape
    m = mem_n.shape[1]
    q = (h @ wq).reshape(bsz, seq, XA_HEADS, XA_DIM)
    k, v = jnp.split(mem_n @ wkv, 2, axis=-1)
    k = k.reshape(bsz, m, XA_HEADS, XA_DIM)
    v = v.reshape(bsz, m, XA_HEADS, XA_DIM)
    s = jnp.einsum('bshd,bmhd->bhsm', q, k).astype(jnp.float32) * (XA_DIM ** -0.5)
    p = jax.nn.softmax(s, axis=-1).astype(v.dtype)
    o = jnp.einsum('bhsm,bmhd->bshd', p, v).reshape(bsz, seq, D_MODEL)
    return o @ wo


def setup_inputs(seed: int = 0) -> dict:
    key = jax.random.key(seed)
    ks = iter(jax.random.split(key, 32))
    L, D, F = DEPTH, D_MODEL, D_FF

    def w(shape, fan_in):
        return jax.random.normal(next(ks), shape, jnp.float32) * (fan_in ** -0.5)

    def gain(shape):
        return 1.0 + 0.01 * jax.random.normal(next(ks), shape, jnp.float32)

    def bias(shape):
        return 0.01 * jax.random.normal(next(ks), shape, jnp.float32)

    x = jax.random.normal(next(ks), (BATCH, SEQ, D), jnp.float32)
    mem = jax.random.normal(next(ks), (BATCH, MEM_LEN, D), jnp.float32)
    start = jax.random.randint(next(ks), (BATCH, 1), 0, 1024, dtype=jnp.int32)
    positions = (start + jnp.arange(SEQ, dtype=jnp.int32)[None, :]).astype(jnp.int32)
    return {
        "x": x, "mem": mem, "positions": positions,
        "ffn1_norm": gain((L, D)), "ffn1_w_gate": w((L, D, F), D), "ffn1_w_up": w((L, D, F), D),
        "ffn1_w_down": w((L, F, D), F),
        "mix_norm": gain((L, D)), "w_in": w((L, D, N_IN), D),
        "conf_dw": w((L, CONV_W, CONV_CH), CONV_W), "conf_dw_b": bias((L, CONV_CH)),
        "conf_ln_g": gain((L, CONV_CH)), "conf_ln_b": bias((L, CONV_CH)),
        "sc_dw": w((L, SC_W, SC_CH), SC_W),
        "w_out": w((L, MIX_W, D), MIX_W),
        "xa_norm": gain((L, D)), "mem_norm": gain((L, D)),
        "xa_wq": w((L, D, D), D), "xa_wkv": w((L, D, 2 * D), D), "xa_wo": w((L, D, D), D),
        "ffn2_norm": gain((L, D)), "ffn2_w_gate": w((L, D, F), D), "ffn2_w_up": w((L, D, F), D),
        "ffn2_w_down": w((L, F, D), F),
        "final_norm": gain((D,)),
    }


def reference(x, mem, positions, ffn1_norm, ffn1_w_gate, ffn1_w_up, ffn1_w_down, mix_norm, w_in,
              conf_dw, conf_dw_b, conf_ln_g, conf_ln_b, sc_dw, w_out, xa_norm, mem_norm,
              xa_wq, xa_wkv, xa_wo, ffn2_norm, ffn2_w_gate, ffn2_w_up, ffn2_w_down, final_norm):
    bsz, seq, _ = x.shape
    cos, sin = rope_tables(positions, HEAD_DIM)
    for l in range(DEPTH):
        x = x + 0.5 * swiglu(rms_norm(x, ffn1_norm[l]), ffn1_w_gate[l], ffn1_w_up[l], ffn1_w_down[l])
        h = rms_norm(x, mix_norm[l])
        (a_val, a_gate, b_gate, c_gate, b_h, q, k, v,
         q_idx, k_idx, w_idx) = jnp.split(h @ w_in[l], SPLIT_POINTS, axis=-1)
        y_a = conformer_conv(a_val, a_gate, conf_dw[l], conf_dw_b[l], conf_ln_g[l], conf_ln_b[l])
        y_b = short_gated_conv(b_gate, c_gate, b_h, sc_dw[l])
        q = apply_rope(q.reshape(bsz, seq, N_HEADS, HEAD_DIM), cos, sin)
        k = apply_rope(k.reshape(bsz, seq, N_HEADS, HEAD_DIM), cos, sin)
        v = v.reshape(bsz, seq, N_HEADS, HEAD_DIM)
        q_idx = apply_rope(q_idx.reshape(bsz, seq, IDX_HEADS, IDX_DIM), cos, sin)
        k_idx = apply_rope(k_idx[:, :, None, :], cos, sin)[:, :, 0, :]
        y_c = dsa_attention(q, k, v, q_idx, k_idx, w_idx)
        x = x + jnp.concatenate([y_a, y_b, y_c], axis=-1) @ w_out[l]
        x = x + memory_cross_attention(rms_norm(x, xa_norm[l]), rms_norm(mem, mem_norm[l]),
                                       xa_wq[l], xa_wkv[l], xa_wo[l])
        x = x + 0.5 * swiglu(rms_norm(x, ffn2_norm[l]), ffn2_w_gate[l], ffn2_w_up[l], ffn2_w_down[l])
    return rms_norm(x, final_norm)
```

```python
import functools

import jax
import jax.numpy as jnp
from jax import lax
from jax.experimental import pallas as pl
from jax.experimental.pallas import tpu as pltpu

F32 = jnp.float32
BF16 = jnp.bfloat16
I32 = jnp.int32

CONV_CH = 256
CONV_W = 31
SC_CH = 256
SC_W = 3
N_HEADS = 8
HEAD_DIM = 64
ATT_W = N_HEADS * HEAD_DIM
IDX_HEADS = 4
IDX_DIM = 64
TOPK_MAX = 256
ROPE_THETA = 10000.0
XA_HEADS = 4
IDX_SCALE = (IDX_HEADS ** -0.5) * (IDX_DIM ** -0.5)
ATT_SCALE = HEAD_DIM ** -0.5

LANES = 128
VMEM_LIMIT = 58 * 1024 * 1024

CONV_COLS = 2 * CONV_CH + 3 * SC_CH
Q_OFF = CONV_COLS
K_OFF = Q_OFF + ATT_W
V_OFF = K_OFF + ATT_W
QI_OFF = V_OFF + ATT_W
KW_OFF = QI_OFF + IDX_HEADS * IDX_DIM
N_IN_PAD = KW_OFF + LANES

INT_MIN = -2 ** 31
INT_MAX = 2 ** 31 - 1
NEG_INF_KEY = -2139095041
NEG_BIG = -1e30


def _cparams(sem, vmem=VMEM_LIMIT):
    return pltpu.CompilerParams(dimension_semantics=sem, vmem_limit_bytes=vmem)


def _rms(x, g, eps=1e-6):
    return x * lax.rsqrt(jnp.mean(x * x, axis=-1, keepdims=True) + eps) * g


def _sigmoid(x):
    return 1.0 / (1.0 + jnp.exp(-x))


def _rope_kernel(pos_ref, invf_ref, sgn_ref, cos_ref, sin_ref):
    ang = pos_ref[...].astype(F32) * invf_ref[...]
    cos_ref[...] = jnp.cos(ang)
    sin_ref[...] = jnp.sin(ang) * sgn_ref[...]


def _rope_tables(positions, tm):
    t = positions.size
    half = HEAD_DIM // 2
    inv_freq = ROPE_THETA ** (-jnp.arange(0, HEAD_DIM, 2, dtype=F32) / HEAD_DIM)
    invf = jnp.tile(inv_freq, LANES // half)[None, :]
    sgn = jnp.tile(jnp.concatenate([-jnp.ones((half,), F32), jnp.ones((half,), F32)]),
                   LANES // HEAD_DIM)[None, :]
    pos = jnp.broadcast_to(positions.reshape(t, 1), (t, LANES))
    row = pl.BlockSpec((tm, LANES), lambda i: (i, 0))
    one = pl.BlockSpec((1, LANES), lambda i: (0, 0))
    return pl.pallas_call(
        _rope_kernel,
        grid=(t // tm,),
        in_specs=[row, one, one],
        out_specs=[row, row],
        out_shape=[jax.ShapeDtypeStruct((t, LANES), F32)] * 2,
        compiler_params=_cparams(("parallel",)),
        name="rope_tables",
    )(pos, invf, sgn)


def _ffn_kernel(x_ref, g_ref, wg_ref, wu_ref, wd_ref, fg_ref, o_ref, h_scr, acc_scr, *, final):
    j = pl.program_id(1)

    @pl.when(j == 0)
    def _():
        h_scr[...] = _rms(x_ref[...], g_ref[...]).astype(BF16)
        acc_scr[...] = jnp.zeros_like(acc_scr)

    h = h_scr[...]
    a = jnp.dot(h, wg_ref[...], preferred_element_type=F32)
    b = jnp.dot(h, wu_ref[...], preferred_element_type=F32)
    t = (a * _sigmoid(a)) * b
    acc_scr[...] += jnp.dot(t.astype(BF16), wd_ref[...], preferred_element_type=F32)

    @pl.when(j == pl.num_programs(1) - 1)
    def _():
        y = x_ref[...] + 0.5 * acc_scr[...]
        if final:
            y = _rms(y, fg_ref[...])
        o_ref[...] = y


def _ffn(x, g, wg, wu, wd, fg, *, final, tm, tf):
    t, d = x.shape
    f = wg.shape[1]
    return pl.pallas_call(
        functools.partial(_ffn_kernel, final=final),
        grid=(t // tm, f // tf),
        in_specs=[
            pl.BlockSpec((tm, d), lambda i, j: (i, 0)),
            pl.BlockSpec((1, d), lambda i, j: (0, 0)),
            pl.BlockSpec((d, tf), lambda i, j: (0, j)),
            pl.BlockSpec((d, tf), lambda i, j: (0, j)),
            pl.BlockSpec((tf, d), lambda i, j: (j, 0)),
            pl.BlockSpec((1, d), lambda i, j: (0, 0)),
        ],
        out_specs=pl.BlockSpec((tm, d), lambda i, j: (i, 0)),
        out_shape=jax.ShapeDtypeStruct((t, d), F32),
        scratch_shapes=[pltpu.VMEM((tm, d), BF16), pltpu.VMEM((tm, d), F32)],
        compiler_params=_cparams(("parallel", "arbitrary")),
        name="ffn_final" if final else "ffn",
    )(x, g, wg, wu, wd, fg)


def _proj_kernel(x_ref, g_ref, w_ref, cos_ref, sin_ref,
                 conv_ref, qT_ref, k_ref, vT_ref, qiT_ref, kidx_ref, kwT_ref, *, kc):
    tm = x_ref.shape[0]
    h = _rms(x_ref[...], g_ref[...]).astype(BF16)
    cos = cos_ref[...]
    sin = sin_ref[...]
    lane = lax.broadcasted_iota(I32, (tm, LANES), 1)
    first_half = (lane % HEAD_DIM) < (HEAD_DIM // 2)

    def rope(t, c, s):
        rot = jnp.where(first_half, pltpu.roll(t, LANES - HEAD_DIM // 2, 1), pltpu.roll(t, HEAD_DIM // 2, 1))
        return t * c + rot * s

    conv_ref[...] = jnp.dot(h, w_ref[:, 0:CONV_COLS], preferred_element_type=F32)

    q = jnp.dot(h, w_ref[:, Q_OFF:Q_OFF + ATT_W], preferred_element_type=F32)
    for g in range(ATT_W // LANES):
        qr = rope(q[:, g * LANES:(g + 1) * LANES], cos, sin) * ATT_SCALE
        qT_ref[g * LANES:(g + 1) * LANES, :] = qr.T.astype(BF16)

    k = jnp.dot(h, w_ref[:, K_OFF:K_OFF + ATT_W], preferred_element_type=F32)
    for g in range(ATT_W // LANES):
        k_ref[:, g * LANES:(g + 1) * LANES] = rope(k[:, g * LANES:(g + 1) * LANES], cos, sin).astype(BF16)

    v = jnp.dot(h, w_ref[:, V_OFF:V_OFF + ATT_W], preferred_element_type=F32)
    for c in range(tm // kc):
        vT_ref[c] = v[c * kc:(c + 1) * kc, :].T.astype(BF16)

    qi = jnp.dot(h, w_ref[:, QI_OFF:QI_OFF + IDX_HEADS * IDX_DIM], preferred_element_type=F32)
    for g in range(IDX_HEADS * IDX_DIM // LANES):
        qiT_ref[g * LANES:(g + 1) * LANES, :] = rope(qi[:, g * LANES:(g + 1) * LANES], cos, sin).T.astype(BF16)

    kw = jnp.dot(h, w_ref[:, KW_OFF:KW_OFF + LANES], preferred_element_type=F32)
    is_kidx = lane < IDX_DIM
    kw = rope(kw, jnp.where(is_kidx, cos, 1.0), jnp.where(is_kidx, sin, 0.0))
    kidx_ref[...] = kw.astype(BF16)
    kwT_ref[...] = kw.T


def _proj(x, g, w, cos, sin, *, bsz, seq, tm, kc):
    d = x.shape[1]
    ns = seq // tm
    tok = lambda b, s: (b * ns + s, 0)
    return pl.pallas_call(
        functools.partial(_proj_kernel, kc=kc),
        grid=(bsz, ns),
        in_specs=[
            pl.BlockSpec((tm, d), tok),
            pl.BlockSpec((1, d), lambda b, s: (0, 0)),
            pl.BlockSpec((d, N_IN_PAD), lambda b, s: (0, 0)),
            pl.BlockSpec((tm, LANES), tok),
            pl.BlockSpec((tm, LANES), tok),
        ],
        out_specs=[
            pl.BlockSpec((None, tm, CONV_COLS), lambda b, s: (b, s, 0)),
            pl.BlockSpec((None, ATT_W, tm), lambda b, s: (b, 0, s)),
            pl.BlockSpec((None, tm, ATT_W), lambda b, s: (b, s, 0)),
            pl.BlockSpec((None, tm // kc, ATT_W, kc), lambda b, s: (b, s, 0, 0)),
            pl.BlockSpec((None, IDX_HEADS * IDX_DIM, tm), lambda b, s: (b, 0, s)),
            pl.BlockSpec((None, tm, LANES), lambda b, s: (b, s, 0)),
            pl.BlockSpec((None, LANES, tm), lambda b, s: (b, 0, s)),
        ],
        out_shape=[
            jax.ShapeDtypeStruct((bsz, seq, CONV_COLS), F32),
            jax.ShapeDtypeStruct((bsz, ATT_W, seq), BF16),
            jax.ShapeDtypeStruct((bsz, seq, ATT_W), BF16),
            jax.ShapeDtypeStruct((bsz, seq // kc, ATT_W, kc), BF16),
            jax.ShapeDtypeStruct((bsz, IDX_HEADS * IDX_DIM, seq), BF16),
            jax.ShapeDtypeStruct((bsz, seq, LANES), BF16),
            jax.ShapeDtypeStruct((bsz, LANES, seq), F32),
        ],
        compiler_params=_cparams(("parallel", "parallel")),
        name="in_proj",
    )(x, g, w, cos, sin)


CONV_HALO = 32
SC_HALO = 8


def _conv_kernel(c_ref, dw_ref, dwb_ref, lng_ref, lnb_ref, sw_ref, o_ref, ha_scr, ub_scr, *, rc):
    ts = c_ref.shape[0]

    @pl.when(pl.program_id(1) == 0)
    def _():
        ha_scr[0:CONV_HALO, :] = jnp.zeros((CONV_HALO, CONV_CH), F32)
        ub_scr[0:SC_HALO, :] = jnp.zeros((SC_HALO, SC_CH), F32)

    ha_scr[CONV_HALO:CONV_HALO + ts, :] = c_ref[:, 0:CONV_CH] * _sigmoid(c_ref[:, CONV_CH:2 * CONV_CH])
    ub_scr[SC_HALO:SC_HALO + ts, :] = (c_ref[:, 2 * CONV_CH + SC_CH:2 * CONV_CH + 2 * SC_CH]
                                       * c_ref[:, 2 * CONV_CH + 2 * SC_CH:2 * CONV_CH + 3 * SC_CH])

    for r in range(ts // rc):
        base = r * rc
        acc = jnp.zeros((rc, CONV_CH), F32)
        for j in range(CONV_W):
            off = base + CONV_HALO - (CONV_W - 1) + j
            acc = acc + ha_scr[off:off + rc, :] * dw_ref[j:j + 1, :]
        acc = acc + dwb_ref[...]
        mu = jnp.mean(acc, axis=-1, keepdims=True)
        cen = acc - mu
        var = jnp.mean(cen * cen, axis=-1, keepdims=True)
        y = cen * lax.rsqrt(var + 1e-5) * lng_ref[...] + lnb_ref[...]
        o_ref[base:base + rc, 0:CONV_CH] = (y * _sigmoid(y)).astype(BF16)

        accb = jnp.zeros((rc, SC_CH), F32)
        for j in range(SC_W):
            off = base + SC_HALO - (SC_W - 1) + j
            accb = accb + ub_scr[off:off + rc, :] * sw_ref[j:j + 1, :]
        bg = c_ref[base:base + rc, 2 * CONV_CH:2 * CONV_CH + SC_CH]
        o_ref[base:base + rc, CONV_CH:CONV_CH + SC_CH] = (bg * accb).astype(BF16)

    ha_scr[0:CONV_HALO, :] = ha_scr[ts:ts + CONV_HALO, :]
    ub_scr[0:SC_HALO, :] = ub_scr[ts:ts + SC_HALO, :]


def _conv_mixers(conv_in, dw, dwb, lng, lnb, sw, *, ts, rc):
    bsz, seq, _ = conv_in.shape
    small = lambda r, c: pl.BlockSpec((r, c), lambda b, s: (0, 0))
    return pl.pallas_call(
        functools.partial(_conv_kernel, rc=rc),
        grid=(bsz, seq // ts),
        in_specs=[
            pl.BlockSpec((None, ts, CONV_COLS), lambda b, s: (b, s, 0)),
            small(CONV_W, CONV_CH), small(1, CONV_CH), small(1, CONV_CH), small(1, CONV_CH),
            small(SC_W, SC_CH),
        ],
        out_specs=pl.BlockSpec((None, ts, CONV_CH + SC_CH), lambda b, s: (b, s, 0)),
        out_shape=jax.ShapeDtypeStruct((bsz, seq, CONV_CH + SC_CH), BF16),
        scratch_shapes=[pltpu.VMEM((ts + CONV_HALO, CONV_CH), F32),
                        pltpu.VMEM((ts + SC_HALO, SC_CH), F32)],
        compiler_params=_cparams(("arbitrary", "arbitrary")),
        name="conv_mixers",
    )(conv_in, dw, dwb, lng, lnb, sw)


def _sortable(x):
    b = lax.bitcast_convert_type(x, I32)
    return b ^ ((b >> 31) & INT_MAX)


def _dsa_kernel(qT_ref, qiT_ref, wT_ref, k_ref, vT_ref, kidx_ref, o_ref, key_scr, jcut_scr, oT_scr,
                *, kc, topk, tie_iters):
    qb = qT_ref.shape[1]
    i = pl.program_id(1)
    nk = (i * qb) // kc + 1
    q_pos = i * qb + lax.broadcasted_iota(I32, (1, qb), 1)
    row_iota = lax.broadcasted_iota(I32, (kc, qb), 0)

    w = wT_ref[...] * IDX_SCALE
    zeros_half = jnp.zeros((LANES - IDX_DIM, qb), BF16)
    qi = [jnp.concatenate([qiT_ref[h * IDX_DIM:(h + 1) * IDX_DIM, :], zeros_half], axis=0)
          for h in range(IDX_HEADS)]

    def score_body(c, carry):
        kcs = kidx_ref[pl.ds(pl.multiple_of(c * kc, kc), kc), :]
        sc = jnp.zeros((kc, qb), F32)
        for h in range(IDX_HEADS):
            lg = jnp.dot(kcs, qi[h], preferred_element_type=F32)
            sc = sc + w[h:h + 1, :] * jnp.maximum(lg, 0.0)
        causal = (c * kc + row_iota) <= q_pos
        key_scr[c] = _sortable(jnp.where(causal, sc, -jnp.inf))
        return carry

    lax.fori_loop(0, nk, score_body, 0)

    fold = 64

    def count(pred):
        def body(c, acc):
            ones = jnp.where(pred(key_scr[c], c), 1, 0)
            for r in range(kc // fold):
                acc = acc + ones[r * fold:(r + 1) * fold, :]
            return acc
        acc = lax.fori_loop(0, nk, body, jnp.zeros((fold, qb), I32))
        return jnp.sum(acc, axis=0, keepdims=True)

    def bisect(_, carry):
        lo, hi, cnt_lo = carry
        mid = (lo & hi) + ((lo ^ hi) >> 1)
        cnt = count(lambda kk, c: kk >= mid)
        ok = cnt >= topk
        return jnp.where(ok, mid, lo), jnp.where(ok, hi, mid), jnp.where(ok, cnt, cnt_lo)

    lo0 = jnp.full((1, qb), INT_MIN, I32)
    hi0 = jnp.full((1, qb), INT_MAX, I32)
    tau, _, cnt_ge = lax.fori_loop(0, 32, bisect, (lo0, hi0, nk * kc + jnp.zeros((1, qb), I32)))

    full = tau > NEG_INF_KEY
    jcut_scr[...] = jnp.where(full, INT_MAX, -1) + jnp.zeros((8, qb), I32)
    excess = jnp.logical_and(full, cnt_ge > topk)

    @pl.when(jnp.max(jnp.where(excess, 1, 0)) > 0)
    def _():
        need = topk - count(lambda kk, c: kk > tau)

        def tie_bisect(_, carry):
            lo, hi = carry
            mid = (lo + hi) >> 1
            cnt = count(lambda kk, c: jnp.logical_and(kk == tau, (c * kc + row_iota) <= mid))
            ok = cnt >= need
            return jnp.where(ok, lo, mid), jnp.where(ok, mid, hi)

        _, hi = lax.fori_loop(0, tie_iters, tie_bisect,
                              (jnp.full((1, qb), -1, I32), nk * kc - 1 + jnp.zeros((1, qb), I32)))
        jcut_scr[...] = jnp.where(full, hi, -1) + jnp.zeros((8, qb), I32)

    tau_eff = jnp.maximum(tau, NEG_INF_KEY)
    jcut = jcut_scr[0:1, :]

    def bias_body(c, carry):
        kk = key_scr[c]
        tie = jnp.where((c * kc + row_iota) <= jcut, 0.0, -jnp.inf)
        bias = jnp.where(kk > tau_eff, 0.0, jnp.where(kk == tau_eff, tie, -jnp.inf))
        key_scr[c] = lax.bitcast_convert_type(bias.astype(F32), I32)
        return carry

    lax.fori_loop(0, nk, bias_body, 0)

    zeros_head = jnp.zeros((HEAD_DIM, qb), BF16)
    for h in range(N_HEADS):
        pair = h // 2
        qh = qT_ref[h * HEAD_DIM:(h + 1) * HEAD_DIM, :]
        qpad = jnp.concatenate([qh, zeros_head] if h % 2 == 0 else [zeros_head, qh], axis=0)

        def att_body(c, carry, pair=pair, h=h, qpad=qpad):
            m, l, acc = carry
            kcs = k_ref[pl.ds(pl.multiple_of(c * kc, kc), kc), pair * LANES:(pair + 1) * LANES]
            s = jnp.dot(kcs, qpad, preferred_element_type=F32)
            s = s + lax.bitcast_convert_type(key_scr[c], F32)
            m_new = jnp.maximum(m, jnp.max(s, axis=0, keepdims=True))
            p = jnp.exp(s - m_new)
            alpha = jnp.exp(m - m_new)
            l_new = alpha * l + jnp.sum(p, axis=0, keepdims=True)
            vt = vT_ref[c, h * HEAD_DIM:(h + 1) * HEAD_DIM, :]
            acc_new = alpha * acc + jnp.dot(vt, p.astype(BF16), preferred_element_type=F32)
            return m_new, l_new, acc_new

        m0 = jnp.full((1, qb), NEG_BIG, F32)
        l0 = jnp.zeros((1, qb), F32)
        a0 = jnp.zeros((HEAD_DIM, qb), F32)
        _, l, acc = lax.fori_loop(0, nk, att_body, (m0, l0, a0))
        oT_scr[h * HEAD_DIM:(h + 1) * HEAD_DIM, :] = acc / l

    o_ref[...] = oT_scr[...].T.astype(BF16)


def _dsa(qT, qiT, kwT, k, vT, kidx, *, qb, kc):
    bsz, seq, _ = k.shape
    topk = min(TOPK_MAX, seq // 4)
    tie_iters = max(1, (seq - 1).bit_length()) + 1
    w_row_block = IDX_DIM // 8
    return pl.pallas_call(
        functools.partial(_dsa_kernel, kc=kc, topk=topk, tie_iters=tie_iters),
        grid=(bsz, seq // qb),
        in_specs=[
            pl.BlockSpec((None, ATT_W, qb), lambda b, i: (b, 0, i)),
            pl.BlockSpec((None, IDX_HEADS * IDX_DIM, qb), lambda b, i: (b, 0, i)),
            pl.BlockSpec((None, 8, qb), lambda b, i: (b, w_row_block, i)),
            pl.BlockSpec((None, seq, ATT_W), lambda b, i: (b, 0, 0)),
            pl.BlockSpec((None, seq // kc, ATT_W, kc), lambda b, i: (b, 0, 0, 0)),
            pl.BlockSpec((None, seq, LANES), lambda b, i: (b, 0, 0)),
        ],
        out_specs=pl.BlockSpec((None, qb, ATT_W), lambda b, i: (b, i, 0)),
        out_shape=jax.ShapeDtypeStruct((bsz, seq, ATT_W), BF16),
        scratch_shapes=[pltpu.VMEM((seq // kc, kc, qb), I32),
                        pltpu.VMEM((8, qb), I32),
                        pltpu.VMEM((ATT_W, qb), F32)],
        compiler_params=_cparams(("parallel", "parallel")),
        name="dsa_attention",
    )(qT, qiT, kwT, k, vT, kidx)


def _outproj_kernel(x_ref, yab_ref, yc_ref, wab_ref, wc_ref, o_ref):
    o_ref[...] = (x_ref[...]
                  + jnp.dot(yab_ref[...], wab_ref[...], preferred_element_type=F32)
                  + jnp.dot(yc_ref[...], wc_ref[...], preferred_element_type=F32))


def _outproj(x, yab, yc, wab, wc, *, tm):
    t, d = x.shape
    row = lambda c: pl.BlockSpec((tm, c), lambda i: (i, 0))
    full = lambda r, c: pl.BlockSpec((r, c), lambda i: (0, 0))
    return pl.pallas_call(
        _outproj_kernel,
        grid=(t // tm,),
        in_specs=[row(d), row(yab.shape[1]), row(yc.shape[1]), full(*wab.shape), full(*wc.shape)],
        out_specs=row(d),
        out_shape=jax.ShapeDtypeStruct((t, d), F32),
        compiler_params=_cparams(("parallel",)),
        name="out_proj",
    )(x, yab, yc, wab, wc)


def _memkv_kernel(mem_ref, g_ref, wkv_ref, kT_ref, v_ref):
    d = mem_ref.shape[1]
    mn = _rms(mem_ref[...], g_ref[...]).astype(BF16)
    kv = jnp.dot(mn, wkv_ref[...], preferred_element_type=F32)
    kT_ref[...] = kv[:, 0:d].T.astype(BF16)
    v_ref[...] = kv[:, d:2 * d].astype(BF16)


def _memkv(mem, g, wkv):
    bsz, m, d = mem.shape
    return pl.pallas_call(
        _memkv_kernel,
        grid=(bsz,),
        in_specs=[pl.BlockSpec((None, m, d), lambda b: (b, 0, 0)),
                  pl.BlockSpec((1, d), lambda b: (0, 0)),
                  pl.BlockSpec((d, 2 * d), lambda b: (0, 0))],
        out_specs=[pl.BlockSpec((None, d, m), lambda b: (b, 0, 0)),
                   pl.BlockSpec((None, m, d), lambda b: (b, 0, 0))],
        out_shape=[jax.ShapeDtypeStruct((bsz, d, m), BF16), jax.ShapeDtypeStruct((bsz, m, d), BF16)],
        compiler_params=_cparams(("parallel",)),
        name="mem_kv",
    )(mem, g, wkv)


def _xattn_kernel(x_ref, g_ref, wq_ref, kT_ref, v_ref, wo_ref, o_ref):
    d = x_ref.shape[1]
    hd = d // XA_HEADS
    x = x_ref[...]
    hq = _rms(x, g_ref[...]).astype(BF16)
    q = (jnp.dot(hq, wq_ref[...], preferred_element_type=F32) * (hd ** -0.5)).astype(BF16)
    outs = []
    for h in range(XA_HEADS):
        s = jnp.dot(q[:, h * hd:(h + 1) * hd], kT_ref[h * hd:(h + 1) * hd, :], preferred_element_type=F32)
        m = jnp.max(s, axis=-1, keepdims=True)
        p = jnp.exp(s - m)
        l = jnp.sum(p, axis=-1, keepdims=True)
        o = jnp.dot(p.astype(BF16), v_ref[:, h * hd:(h + 1) * hd], preferred_element_type=F32) / l
        outs.append(o.astype(BF16))
    o_ref[...] = x + jnp.dot(jnp.concatenate(outs, axis=-1), wo_ref[...], preferred_element_type=F32)


def _xattn(x, g, wq, kT, v, wo, *, bsz, seq, tm):
    d = x.shape[1]
    m = v.shape[1]
    ns = seq // tm
    tok = pl.BlockSpec((tm, d), lambda b, s: (b * ns + s, 0))
    return pl.pallas_call(
        _xattn_kernel,
        grid=(bsz, ns),
        in_specs=[tok,
                  pl.BlockSpec((1, d), lambda b, s: (0, 0)),
                  pl.BlockSpec((d, d), lambda b, s: (0, 0)),
                  pl.BlockSpec((None, d, m), lambda b, s: (b, 0, 0)),
                  pl.BlockSpec((None, m, d), lambda b, s: (b, 0, 0)),
                  pl.BlockSpec((d, d), lambda b, s: (0, 0))],
        out_specs=tok,
        out_shape=jax.ShapeDtypeStruct((bsz * seq, d), F32),
        compiler_params=_cparams(("parallel", "parallel")),
        name="mem_xattn",
    )(x, g, wq, kT, v, wo)


def _tiles(seq, d_ff):
    assert seq % 512 == 0, "sequence length must be a multiple of 512"
    tf = d_ff // 2 if (d_ff // 2) % LANES == 0 else d_ff
    return dict(
        tm_ffn=512, tf=tf,
        tm_proj=512,
        ts_conv=256, rc_conv=64,
        qb=256, kc=256,
        tm_out=1024, tm_xa=512, tm_rope=1024,
    )


def kernel(x, mem, positions, ffn1_norm, ffn1_w_gate, ffn1_w_up, ffn1_w_down, mix_norm, w_in,
           conf_dw, conf_dw_b, conf_ln_g, conf_ln_b, sc_dw, w_out, xa_norm, mem_norm,
           xa_wq, xa_wkv, xa_wo, ffn2_norm, ffn2_w_gate, ffn2_w_up, ffn2_w_down, final_norm):
    bsz, seq, d = x.shape
    depth = w_in.shape[0]
    t = bsz * seq
    tl = _tiles(seq, ffn1_w_gate.shape[2])
    bf = lambda a: a.astype(BF16)
    row = lambda a: a.reshape(1, -1)

    cos, sin = _rope_tables(positions, tl["tm_rope"])
    xf = x.reshape(t, d)
    w_in_pad = jnp.pad(w_in, ((0, 0), (0, 0), (0, N_IN_PAD - w_in.shape[2])))

    for l in range(depth):
        xf = _ffn(xf, row(ffn1_norm[l]), bf(ffn1_w_gate[l]), bf(ffn1_w_up[l]), bf(ffn1_w_down[l]),
                  row(final_norm), final=False, tm=tl["tm_ffn"], tf=tl["tf"])

        conv_in, qT, k, vT, qiT, kidx, kwT = _proj(
            xf, row(mix_norm[l]), bf(w_in_pad[l]), cos, sin, bsz=bsz, seq=seq, tm=tl["tm_proj"], kc=tl["kc"])
        yab = _conv_mixers(conv_in, conf_dw[l], row(conf_dw_b[l]), row(conf_ln_g[l]), row(conf_ln_b[l]),
                           sc_dw[l], ts=tl["ts_conv"], rc=tl["rc_conv"])
        yc = _dsa(qT, qiT, kwT, k, vT, kidx, qb=tl["qb"], kc=tl["kc"])
        n_ab = CONV_CH + SC_CH
        xf = _outproj(xf, yab.reshape(t, n_ab), yc.reshape(t, ATT_W),
                      bf(w_out[l, 0:n_ab]), bf(w_out[l, n_ab:]), tm=tl["tm_out"])

        kT_mem, v_mem = _memkv(mem, row(mem_norm[l]), bf(xa_wkv[l]))
        xf = _xattn(xf, row(xa_norm[l]), bf(xa_wq[l]), kT_mem, v_mem, bf(xa_wo[l]),
                    bsz=bsz, seq=seq, tm=tl["tm_xa"])

        xf = _ffn(xf, row(ffn2_norm[l]), bf(ffn2_w_gate[l]), bf(ffn2_w_up[l]), bf(ffn2_w_down[l]),
                  row(final_norm), final=(l == depth - 1), tm=tl["tm_ffn"], tf=tl["tf"])

    return xf.reshape(bsz, seq, d)
```

```python
import functools

import jax
import jax.numpy as jnp
from jax import lax
from jax.experimental import pallas as pl
from jax.experimental.pallas import tpu as pltpu

F32 = jnp.float32
BF16 = jnp.bfloat16
I32 = jnp.int32

CONV_CH = 256
CONV_W = 31
SC_CH = 256
SC_W = 3
N_HEADS = 8
HEAD_DIM = 64
ATT_W = N_HEADS * HEAD_DIM
IDX_HEADS = 4
IDX_DIM = 64
TOPK_MAX = 256
ROPE_THETA = 10000.0
XA_HEADS = 4
IDX_SCALE = (IDX_HEADS ** -0.5) * (IDX_DIM ** -0.5)
ATT_SCALE = HEAD_DIM ** -0.5

LANES = 128
VMEM_LIMIT = 58 * 1024 * 1024

CONV_COLS = 2 * CONV_CH + 3 * SC_CH
Q_OFF = CONV_COLS
K_OFF = Q_OFF + ATT_W
V_OFF = K_OFF + ATT_W
QI_OFF = V_OFF + ATT_W
KW_OFF = QI_OFF + IDX_HEADS * IDX_DIM
N_IN_PAD = KW_OFF + LANES

INT_MIN = -2 ** 31
INT_MAX = 2 ** 31 - 1
NEG_INF_KEY = -2139095041
NEG_BIG = -1e30


def _cparams(sem, vmem=VMEM_LIMIT):
    return pltpu.CompilerParams(dimension_semantics=sem, vmem_limit_bytes=vmem)


def _rms(x, g, eps=1e-6):
    return x * lax.rsqrt(jnp.mean(x * x, axis=-1, keepdims=True) + eps) * g


def _sigmoid(x):
    return 1.0 / (1.0 + jnp.exp(-x))


def _rope_kernel(pos_ref, invf_ref, sgn_ref, cos_ref, sin_ref):
    ang = pos_ref[...].astype(F32) * invf_ref[...]
    cos_ref[...] = jnp.cos(ang)
    sin_ref[...] = jnp.sin(ang) * sgn_ref[...]


def _rope_tables(positions, tm):
    t = positions.size
    half = HEAD_DIM // 2
    inv_freq = ROPE_THETA ** (-jnp.arange(0, HEAD_DIM, 2, dtype=F32) / HEAD_DIM)
    invf = jnp.tile(inv_freq, LANES // half)[None, :]
    sgn = jnp.tile(jnp.concatenate([-jnp.ones((half,), F32), jnp.ones((half,), F32)]),
                   LANES // HEAD_DIM)[None, :]
    pos = jnp.broadcast_to(positions.reshape(t, 1), (t, LANES))
    row = pl.BlockSpec((tm, LANES), lambda i: (i, 0))
    one = pl.BlockSpec((1, LANES), lambda i: (0, 0))
    return pl.pallas_call(
        _rope_kernel,
        grid=(t // tm,),
        in_specs=[row, one, one],
        out_specs=[row, row],
        out_shape=[jax.ShapeDtypeStruct((t, LANES), F32)] * 2,
        compiler_params=_cparams(("parallel",)),
        name="rope_tables",
    )(pos, invf, sgn)


def _ffn_kernel(x_ref, g_ref, wg_ref, wu_ref, wd_ref, fg_ref, o_ref, h_scr, acc_scr, *, final):
    j = pl.program_id(1)

    @pl.when(j == 0)
    def _():
        h_scr[...] = _rms(x_ref[...], g_ref[...]).astype(BF16)
        acc_scr[...] = jnp.zeros_like(acc_scr)

    h = h_scr[...]
    a = jnp.dot(h, wg_ref[...], preferred_element_type=F32)
    b = jnp.dot(h, wu_ref[...], preferred_element_type=F32)
    t = (a * _sigmoid(a)) * b
    acc_scr[...] += jnp.dot(t.astype(BF16), wd_ref[...], preferred_element_type=F32)

    @pl.when(j == pl.num_programs(1) - 1)
    def _():
        y = x_ref[...] + 0.5 * acc_scr[...]
        if final:
            y = _rms(y, fg_ref[...])
        o_ref[...] = y


def _ffn(x, g, wg, wu, wd, fg, *, final, tm, tf):
    t, d = x.shape
    f = wg.shape[1]
    return pl.pallas_call(
        functools.partial(_ffn_kernel, final=final),
        grid=(t // tm, f // tf),
        in_specs=[
            pl.BlockSpec((tm, d), lambda i, j: (i, 0)),
            pl.BlockSpec((1, d), lambda i, j: (0, 0)),
            pl.BlockSpec((d, tf), lambda i, j: (0, j)),
            pl.BlockSpec((d, tf), lambda i, j: (0, j)),
            pl.BlockSpec((tf, d), lambda i, j: (j, 0)),
            pl.BlockSpec((1, d), lambda i, j: (0, 0)),
        ],
        out_specs=pl.BlockSpec((tm, d), lambda i, j: (i, 0)),
        out_shape=jax.ShapeDtypeStruct((t, d), F32),
        scratch_shapes=[pltpu.VMEM((tm, d), BF16), pltpu.VMEM((tm, d), F32)],
        compiler_params=_cparams(("parallel", "arbitrary")),
        name="ffn_final" if final else "ffn",
    )(x, g, wg, wu, wd, fg)


def _proj_kernel(x_ref, g_ref, w_ref, cos_ref, sin_ref,
                 conv_ref, qT_ref, k_ref, vT_ref, qiT_ref, kidx_ref, kwT_ref, *, kc):
    tm = x_ref.shape[0]
    h = _rms(x_ref[...], g_ref[...]).astype(BF16)
    cos = cos_ref[...]
    sin = sin_ref[...]
    lane = lax.broadcasted_iota(I32, (tm, LANES), 1)
    first_half = (lane % HEAD_DIM) < (HEAD_DIM // 2)

    def rope(t, c, s):
        rot = jnp.where(first_half, pltpu.roll(t, LANES - HEAD_DIM // 2, 1), pltpu.roll(t, HEAD_DIM // 2, 1))
        return t * c + rot * s

    conv_ref[...] = jnp.dot(h, w_ref[:, 0:CONV_COLS], preferred_element_type=F32)

    q = jnp.dot(h, w_ref[:, Q_OFF:Q_OFF + ATT_W], preferred_element_type=F32)
    for g in range(ATT_W // LANES):
        qr = rope(q[:, g * LANES:(g + 1) * LANES], cos, sin) * ATT_SCALE
        qT_ref[g * LANES:(g + 1) * LANES, :] = qr.T.astype(BF16)

    k = jnp.dot(h, w_ref[:, K_OFF:K_OFF + ATT_W], preferred_element_type=F32)
    for g in range(ATT_W // LANES):
        k_ref[:, g * LANES:(g + 1) * LANES] = rope(k[:, g * LANES:(g + 1) * LANES], cos, sin).astype(BF16)

    v = jnp.dot(h, w_ref[:, V_OFF:V_OFF + ATT_W], preferred_element_type=F32)
    for c in range(tm // kc):
        vT_ref[c] = v[c * kc:(c + 1) * kc, :].T.astype(BF16)

    qi = jnp.dot(h, w_ref[:, QI_OFF:QI_OFF + IDX_HEADS * IDX_DIM], preferred_element_type=F32)
    for g in range(IDX_HEADS * IDX_DIM // LANES):
        qiT_ref[g * LANES:(g + 1) * LANES, :] = rope(qi[:, g * LANES:(g + 1) * LANES], cos, sin).T.astype(BF16)

    kw = jnp.dot(h, w_ref[:, KW_OFF:KW_OFF + LANES], preferred_element_type=F32)
    is_kidx = lane < IDX_DIM
    kw = rope(kw, jnp.where(is_kidx, cos, 1.0), jnp.where(is_kidx, sin, 0.0))
    kidx_ref[...] = kw.astype(BF16)
    kwT_ref[...] = kw.T


def _proj(x, g, w, cos, sin, *, bsz, seq, tm, kc):
    d = x.shape[1]
    ns = seq // tm
    tok = lambda b, s: (b * ns + s, 0)
    return pl.pallas_call(
        functools.partial(_proj_kernel, kc=kc),
        grid=(bsz, ns),
        in_specs=[
            pl.BlockSpec((tm, d), tok),
            pl.BlockSpec((1, d), lambda b, s: (0, 0)),
            pl.BlockSpec((d, N_IN_PAD), lambda b, s: (0, 0)),
            pl.BlockSpec((tm, LANES), tok),
            pl.BlockSpec((tm, LANES), tok),
        ],
        out_specs=[
            pl.BlockSpec((None, tm, CONV_COLS), lambda b, s: (b, s, 0)),
            pl.BlockSpec((None, ATT_W, tm), lambda b, s: (b, 0, s)),
            pl.BlockSpec((None, tm, ATT_W), lambda b, s: (b, s, 0)),
            pl.BlockSpec((None, tm // kc, ATT_W, kc), lambda b, s: (b, s, 0, 0)),
            pl.BlockSpec((None, IDX_HEADS * IDX_DIM, tm), lambda b, s: (b, 0, s)),
            pl.BlockSpec((None, tm, LANES), lambda b, s: (b, s, 0)),
            pl.BlockSpec((None, LANES, tm), lambda b, s: (b, 0, s)),
        ],
        out_shape=[
            jax.ShapeDtypeStruct((bsz, seq, CONV_COLS), F32),
            jax.ShapeDtypeStruct((bsz, ATT_W, seq), BF16),
            jax.ShapeDtypeStruct((bsz, seq, ATT_W), BF16),
            jax.ShapeDtypeStruct((bsz, seq // kc, ATT_W, kc), BF16),
            jax.ShapeDtypeStruct((bsz, IDX_HEADS * IDX_DIM, seq), BF16),
            jax.ShapeDtypeStruct((bsz, seq, LANES), BF16),
            jax.ShapeDtypeStruct((bsz, LANES, seq), F32),
        ],
        compiler_params=_cparams(("parallel", "parallel")),
        name="in_proj",
    )(x, g, w, cos, sin)


CONV_HALO = 32
SC_HALO = 8


def _conv_kernel(c_ref, dw_ref, dwb_ref, lng_ref, lnb_ref, sw_ref, o_ref, ha_scr, ub_scr, *, rc):
    ts = c_ref.shape[0]

    @pl.when(pl.program_id(1) == 0)
    def _():
        ha_scr[0:CONV_HALO, :] = jnp.zeros((CONV_HALO, CONV_CH), F32)
        ub_scr[0:SC_HALO, :] = jnp.zeros((SC_HALO, SC_CH), F32)

    ha_scr[CONV_HALO:CONV_HALO + ts, :] = c_ref[:, 0:CONV_CH] * _sigmoid(c_ref[:, CONV_CH:2 * CONV_CH])
    ub_scr[SC_HALO:SC_HALO + ts, :] = (c_ref[:, 2 * CONV_CH + SC_CH:2 * CONV_CH + 2 * SC_CH]
                                       * c_ref[:, 2 * CONV_CH + 2 * SC_CH:2 * CONV_CH + 3 * SC_CH])

    for r in range(ts // rc):
        base = r * rc
        acc = jnp.zeros((rc, CONV_CH), F32)
        for j in range(CONV_W):
            off = base + CONV_HALO - (CONV_W - 1) + j
            acc = acc + ha_scr[off:off + rc, :] * dw_ref[j:j + 1, :]
        acc = acc + dwb_ref[...]
        mu = jnp.mean(acc, axis=-1, keepdims=True)
        cen = acc - mu
        var = jnp.mean(cen * cen, axis=-1, keepdims=True)
        y = cen * lax.rsqrt(var + 1e-5) * lng_ref[...] + lnb_ref[...]
        o_ref[base:base + rc, 0:CONV_CH] = (y * _sigmoid(y)).astype(BF16)

        accb = jnp.zeros((rc, SC_CH), F32)
        for j in range(SC_W):
            off = base + SC_HALO - (SC_W - 1) + j
            accb = accb + ub_scr[off:off + rc, :] * sw_ref[j:j + 1, :]
        bg = c_ref[base:base + rc, 2 * CONV_CH:2 * CONV_CH + SC_CH]
        o_ref[base:base + rc, CONV_CH:CONV_CH + SC_CH] = (bg * accb).astype(BF16)

    ha_scr[0:CONV_HALO, :] = ha_scr[ts:ts + CONV_HALO, :]
    ub_scr[0:SC_HALO, :] = ub_scr[ts:ts + SC_HALO, :]


def _conv_mixers(conv_in, dw, dwb, lng, lnb, sw, *, ts, rc):
    bsz, seq, _ = conv_in.shape
    small = lambda r, c: pl.BlockSpec((r, c), lambda b, s: (0, 0))
    return pl.pallas_call(
        functools.partial(_conv_kernel, rc=rc),
        grid=(bsz, seq // ts),
        in_specs=[
            pl.BlockSpec((None, ts, CONV_COLS), lambda b, s: (b, s, 0)),
            small(CONV_W, CONV_CH), small(1, CONV_CH), small(1, CONV_CH), small(1, CONV_CH),
            small(SC_W, SC_CH),
        ],
        out_specs=pl.BlockSpec((None, ts, CONV_CH + SC_CH), lambda b, s: (b, s, 0)),
        out_shape=jax.ShapeDtypeStruct((bsz, seq, CONV_CH + SC_CH), BF16),
        scratch_shapes=[pltpu.VMEM((ts + CONV_HALO, CONV_CH), F32),
                        pltpu.VMEM((ts + SC_HALO, SC_CH), F32)],
        compiler_params=_cparams(("arbitrary", "arbitrary")),
        name="conv_mixers",
    )(conv_in, dw, dwb, lng, lnb, sw)


def _sortable(x):
    b = lax.bitcast_convert_type(x, I32)
    return b ^ ((b >> 31) & INT_MAX)


def _dsa_kernel(qT_ref, qiT_ref, wT_ref, k_ref, vT_ref, kidx_ref, o_ref,
                key_scr, jcut_scr, oT_scr, qpad_scr, m_scr, l_scr, *, kc, topk, tie_iters):
    qb = qT_ref.shape[1]
    i = pl.program_id(1)
    nk = (i * qb) // kc + 1
    q_pos = i * qb + lax.broadcasted_iota(I32, (1, qb), 1)
    row_iota = lax.broadcasted_iota(I32, (kc, qb), 0)

    w = wT_ref[...] * IDX_SCALE
    zeros_half = jnp.zeros((LANES - IDX_DIM, qb), BF16)
    qi = [jnp.concatenate([qiT_ref[h * IDX_DIM:(h + 1) * IDX_DIM, :], zeros_half], axis=0)
          for h in range(IDX_HEADS)]

    def score_body(c, carry):
        kcs = kidx_ref[pl.ds(pl.multiple_of(c * kc, kc), kc), :]
        sc = jnp.zeros((kc, qb), F32)
        for h in range(IDX_HEADS):
            lg = jnp.dot(kcs, qi[h], preferred_element_type=F32)
            sc = sc + w[h:h + 1, :] * jnp.maximum(lg, 0.0)
        causal = (c * kc + row_iota) <= q_pos
        key_scr[c] = _sortable(jnp.where(causal, sc, -jnp.inf))
        return carry

    lax.fori_loop(0, nk, score_body, 0)

    fold = 64

    def count(pred):
        def body(c, acc):
            ones = jnp.where(pred(key_scr[c], c), 1, 0)
            for r in range(kc // fold):
                acc = acc + ones[r * fold:(r + 1) * fold, :]
            return acc
        acc = lax.fori_loop(0, nk, body, jnp.zeros((fold, qb), I32))
        return jnp.sum(acc, axis=0, keepdims=True)

    def bisect(_, carry):
        lo, hi, cnt_lo = carry
        mid = (lo & hi) + ((lo ^ hi) >> 1)
        cnt = count(lambda kk, c: kk >= mid)
        ok = cnt >= topk
        return jnp.where(ok, mid, lo), jnp.where(ok, hi, mid), jnp.where(ok, cnt, cnt_lo)

    lo0 = jnp.full((1, qb), INT_MIN, I32)
    hi0 = jnp.full((1, qb), INT_MAX, I32)
    tau, _, cnt_ge = lax.fori_loop(0, 32, bisect, (lo0, hi0, nk * kc + jnp.zeros((1, qb), I32)))

    full = tau > NEG_INF_KEY
    jcut_scr[...] = jnp.where(full, INT_MAX, -1) + jnp.zeros((8, qb), I32)
    excess = jnp.logical_and(full, cnt_ge > topk)

    @pl.when(jnp.max(jnp.where(excess, 1, 0)) > 0)
    def _():
        need = topk - count(lambda kk, c: kk > tau)

        def tie_bisect(_, carry):
            lo, hi = carry
            mid = (lo + hi) >> 1
            cnt = count(lambda kk, c: jnp.logical_and(kk == tau, (c * kc + row_iota) <= mid))
            ok = cnt >= need
            return jnp.where(ok, lo, mid), jnp.where(ok, mid, hi)

        _, hi = lax.fori_loop(0, tie_iters, tie_bisect,
                              (jnp.full((1, qb), -1, I32), nk * kc - 1 + jnp.zeros((1, qb), I32)))
        jcut_scr[...] = jnp.where(full, hi, -1) + jnp.zeros((8, qb), I32)

    tau_eff = jnp.maximum(tau, NEG_INF_KEY)
    jcut = jcut_scr[0:1, :]

    def bias_body(c, carry):
        kk = key_scr[c]
        tie = jnp.where((c * kc + row_iota) <= jcut, 0.0, -jnp.inf)
        bias = jnp.where(kk > tau_eff, 0.0, jnp.where(kk == tau_eff, tie, -jnp.inf))
        key_scr[c] = lax.bitcast_convert_type(bias.astype(F32), I32)
        return carry

    lax.fori_loop(0, nk, bias_body, 0)

    zeros_head = jnp.zeros((HEAD_DIM, qb), BF16)
    for h in range(N_HEADS):
        qh = qT_ref[h * HEAD_DIM:(h + 1) * HEAD_DIM, :]
        qpad_scr[h] = jnp.concatenate([qh, zeros_head] if h % 2 == 0 else [zeros_head, qh], axis=0)
    m_scr[...] = jnp.full((N_HEADS, qb), NEG_BIG, F32)
    l_scr[...] = jnp.zeros((N_HEADS, qb), F32)
    oT_scr[...] = jnp.zeros((ATT_W, qb), F32)

    def att_body(c, carry):
        bias = lax.bitcast_convert_type(key_scr[c], F32)
        row0 = pl.multiple_of(c * kc, kc)
        m_old = m_scr[...]
        l_old = l_scr[...]
        s_all = [jnp.dot(k_ref[pl.ds(row0, kc), (h // 2) * LANES:(h // 2 + 1) * LANES], qpad_scr[h],
                         preferred_element_type=F32) for h in range(N_HEADS)]
        m_new, l_new, alpha_rows, p_all = [], [], [], []
        for h in range(N_HEADS):
            s = s_all[h] + bias
            m = m_old[h:h + 1, :]
            mn = jnp.maximum(m, jnp.max(s, axis=0, keepdims=True))
            p = jnp.exp(s - mn)
            alpha = jnp.exp(m - mn)
            m_new.append(mn)
            l_new.append(alpha * l_old[h:h + 1, :] + jnp.sum(p, axis=0, keepdims=True))
            alpha_rows.append(jnp.broadcast_to(alpha, (HEAD_DIM, qb)))
            p_all.append(p.astype(BF16))
        pv = [jnp.dot(vT_ref[c, h * HEAD_DIM:(h + 1) * HEAD_DIM, :], p_all[h], preferred_element_type=F32)
              for h in range(N_HEADS)]
        oT_scr[...] = jnp.concatenate(alpha_rows, axis=0) * oT_scr[...] + jnp.concatenate(pv, axis=0)
        m_scr[...] = jnp.concatenate(m_new, axis=0)
        l_scr[...] = jnp.concatenate(l_new, axis=0)
        return carry

    lax.fori_loop(0, nk, att_body, 0)
    for h in range(N_HEADS):
        hs = slice(h * HEAD_DIM, (h + 1) * HEAD_DIM)
        oT_scr[hs, :] = oT_scr[hs, :] / l_scr[h:h + 1, :]

    o_ref[...] = oT_scr[...].T.astype(BF16)


def _dsa(qT, qiT, kwT, k, vT, kidx, *, qb, kc):
    bsz, seq, _ = k.shape
    topk = min(TOPK_MAX, seq // 4)
    tie_iters = max(1, (seq - 1).bit_length()) + 1
    w_row_block = IDX_DIM // 8
    return pl.pallas_call(
        functools.partial(_dsa_kernel, kc=kc, topk=topk, tie_iters=tie_iters),
        grid=(bsz, seq // qb),
        in_specs=[
            pl.BlockSpec((None, ATT_W, qb), lambda b, i: (b, 0, i)),
            pl.BlockSpec((None, IDX_HEADS * IDX_DIM, qb), lambda b, i: (b, 0, i)),
            pl.BlockSpec((None, 8, qb), lambda b, i: (b, w_row_block, i)),
            pl.BlockSpec((None, seq, ATT_W), lambda b, i: (b, 0, 0)),
            pl.BlockSpec((None, seq // kc, ATT_W, kc), lambda b, i: (b, 0, 0, 0)),
            pl.BlockSpec((None, seq, LANES), lambda b, i: (b, 0, 0)),
        ],
        out_specs=pl.BlockSpec((None, qb, ATT_W), lambda b, i: (b, i, 0)),
        out_shape=jax.ShapeDtypeStruct((bsz, seq, ATT_W), BF16),
        scratch_shapes=[pltpu.VMEM((seq // kc, kc, qb), I32),
                        pltpu.VMEM((8, qb), I32),
                        pltpu.VMEM((ATT_W, qb), F32),
                        pltpu.VMEM((N_HEADS, LANES, qb), BF16),
                        pltpu.VMEM((N_HEADS, qb), F32),
                        pltpu.VMEM((N_HEADS, qb), F32)],
        compiler_params=_cparams(("parallel", "parallel")),
        name="dsa_attention",
    )(qT, qiT, kwT, k, vT, kidx)


def _outproj_kernel(x_ref, yab_ref, yc_ref, wab_ref, wc_ref, o_ref):
    o_ref[...] = (x_ref[...]
                  + jnp.dot(yab_ref[...], wab_ref[...], preferred_element_type=F32)
                  + jnp.dot(yc_ref[...], wc_ref[...], preferred_element_type=F32))


def _outproj(x, yab, yc, wab, wc, *, tm):
    t, d = x.shape
    row = lambda c: pl.BlockSpec((tm, c), lambda i: (i, 0))
    full = lambda r, c: pl.BlockSpec((r, c), lambda i: (0, 0))
    return pl.pallas_call(
        _outproj_kernel,
        grid=(t // tm,),
        in_specs=[row(d), row(yab.shape[1]), row(yc.shape[1]), full(*wab.shape), full(*wc.shape)],
        out_specs=row(d),
        out_shape=jax.ShapeDtypeStruct((t, d), F32),
        compiler_params=_cparams(("parallel",)),
        name="out_proj",
    )(x, yab, yc, wab, wc)


def _memkv_kernel(mem_ref, g_ref, wkv_ref, kT_ref, v_ref):
    d = mem_ref.shape[1]
    mn = _rms(mem_ref[...], g_ref[...]).astype(BF16)
    kv = jnp.dot(mn, wkv_ref[...], preferred_element_type=F32)
    kT_ref[...] = kv[:, 0:d].T.astype(BF16)
    v_ref[...] = kv[:, d:2 * d].astype(BF16)


def _memkv(mem, g, wkv):
    bsz, m, d = mem.shape
    return pl.pallas_call(
        _memkv_kernel,
        grid=(bsz,),
        in_specs=[pl.BlockSpec((None, m, d), lambda b: (b, 0, 0)),
                  pl.BlockSpec((1, d), lambda b: (0, 0)),
                  pl.BlockSpec((d, 2 * d), lambda b: (0, 0))],
        out_specs=[pl.BlockSpec((None, d, m), lambda b: (b, 0, 0)),
                   pl.BlockSpec((None, m, d), lambda b: (b, 0, 0))],
        out_shape=[jax.ShapeDtypeStruct((bsz, d, m), BF16), jax.ShapeDtypeStruct((bsz, m, d), BF16)],
        compiler_params=_cparams(("parallel",)),
        name="mem_kv",
    )(mem, g, wkv)


def _xattn_kernel(x_ref, g_ref, wq_ref, kT_ref, v_ref, wo_ref, o_ref):
    d = x_ref.shape[1]
    hd = d // XA_HEADS
    x = x_ref[...]
    hq = _rms(x, g_ref[...]).astype(BF16)
    q = (jnp.dot(hq, wq_ref[...], preferred_element_type=F32) * (hd ** -0.5)).astype(BF16)
    outs = []
    for h in range(XA_HEADS):
        s = jnp.dot(q[:, h * hd:(h + 1) * hd], kT_ref[h * hd:(h + 1) * hd, :], preferred_element_type=F32)
        m = jnp.max(s, axis=-1, keepdims=True)
        p = jnp.exp(s - m)
        l = jnp.sum(p, axis=-1, keepdims=True)
        o = jnp.dot(p.astype(BF16), v_ref[:, h * hd:(h + 1) * hd], preferred_element_type=F32) / l
        outs.append(o.astype(BF16))
    o_ref[...] = x + jnp.dot(jnp.concatenate(outs, axis=-1), wo_ref[...], preferred_element_type=F32)


def _xattn(x, g, wq, kT, v, wo, *, bsz, seq, tm):
    d = x.shape[1]
    m = v.shape[1]
    ns = seq // tm
    tok = pl.BlockSpec((tm, d), lambda b, s: (b * ns + s, 0))
    return pl.pallas_call(
        _xattn_kernel,
        grid=(bsz, ns),
        in_specs=[tok,
                  pl.BlockSpec((1, d), lambda b, s: (0, 0)),
                  pl.BlockSpec((d, d), lambda b, s: (0, 0)),
                  pl.BlockSpec((None, d, m), lambda b, s: (b, 0, 0)),
                  pl.BlockSpec((None, m, d), lambda b, s: (b, 0, 0)),
                  pl.BlockSpec((d, d), lambda b, s: (0, 0))],
        out_specs=tok,
        out_shape=jax.ShapeDtypeStruct((bsz * seq, d), F32),
        compiler_params=_cparams(("parallel", "parallel")),
        name="mem_xattn",
    )(x, g, wq, kT, v, wo)


def _tiles(seq, d_ff):
    assert seq % 512 == 0, "sequence length must be a multiple of 512"
    tf = d_ff // 2 if (d_ff // 2) % LANES == 0 else d_ff
    return dict(
        tm_ffn=512, tf=tf,
        tm_proj=512,
        ts_conv=256, rc_conv=64,
        qb=256, kc=256,
        tm_out=1024, tm_xa=512, tm_rope=1024,
    )


def kernel(x, mem, positions, ffn1_norm, ffn1_w_gate, ffn1_w_up, ffn1_w_down, mix_norm, w_in,
           conf_dw, conf_dw_b, conf_ln_g, conf_ln_b, sc_dw, w_out, xa_norm, mem_norm,
           xa_wq, xa_wkv, xa_wo, ffn2_norm, ffn2_w_gate, ffn2_w_up, ffn2_w_down, final_norm):
    bsz, seq, d = x.shape
    depth = w_in.shape[0]
    t = bsz * seq
    tl = _tiles(seq, ffn1_w_gate.shape[2])
    bf = lambda a: a.astype(BF16)
    row = lambda a: a.reshape(1, -1)

    cos, sin = _rope_tables(positions, tl["tm_rope"])
    xf = x.reshape(t, d)
    w_in_pad = jnp.pad(w_in, ((0, 0), (0, 0), (0, N_IN_PAD - w_in.shape[2])))

    for l in range(depth):
        xf = _ffn(xf, row(ffn1_norm[l]), bf(ffn1_w_gate[l]), bf(ffn1_w_up[l]), bf(ffn1_w_down[l]),
                  row(final_norm), final=False, tm=tl["tm_ffn"], tf=tl["tf"])

        conv_in, qT, k, vT, qiT, kidx, kwT = _proj(
            xf, row(mix_norm[l]), bf(w_in_pad[l]), cos, sin, bsz=bsz, seq=seq, tm=tl["tm_proj"], kc=tl["kc"])
        yab = _conv_mixers(conv_in, conf_dw[l], row(conf_dw_b[l]), row(conf_ln_g[l]), row(conf_ln_b[l]),
                           sc_dw[l], ts=tl["ts_conv"], rc=tl["rc_conv"])
        yc = _dsa(qT, qiT, kwT, k, vT, kidx, qb=tl["qb"], kc=tl["kc"])
        n_ab = CONV_CH + SC_CH
        xf = _outproj(xf, yab.reshape(t, n_ab), yc.reshape(t, ATT_W),
                      bf(w_out[l, 0:n_ab]), bf(w_out[l, n_ab:]), tm=tl["tm_out"])

        kT_mem, v_mem = _memkv(mem, row(mem_norm[l]), bf(xa_wkv[l]))
        xf = _xattn(xf, row(xa_norm[l]), bf(xa_wq[l]), kT_mem, v_mem, bf(xa_wo[l]),
                    bsz=bsz, seq=seq, tm=tl["tm_xa"])

        xf = _ffn(xf, row(ffn2_norm[l]), bf(ffn2_w_gate[l]), bf(ffn2_w_up[l]), bf(ffn2_w_down[l]),
                  row(final_norm), final=(l == depth - 1), tm=tl["tm_ffn"], tf=tl["tf"])

    return xf.reshape(bsz, seq, d)
```

```python
import functools

import jax
import jax.numpy as jnp
from jax import lax
from jax.experimental import pallas as pl
from jax.experimental.pallas import tpu as pltpu

F32 = jnp.float32
BF16 = jnp.bfloat16
I32 = jnp.int32
I16 = jnp.int16

CONV_CH = 256
CONV_W = 31
SC_CH = 256
SC_W = 3
N_HEADS = 8
HEAD_DIM = 64
ATT_W = N_HEADS * HEAD_DIM
IDX_HEADS = 4
IDX_DIM = 64
TOPK_MAX = 256
ROPE_THETA = 10000.0
XA_HEADS = 4
IDX_SCALE = (IDX_HEADS ** -0.5) * (IDX_DIM ** -0.5)
ATT_SCALE = HEAD_DIM ** -0.5

LANES = 128
VMEM_LIMIT = 58 * 1024 * 1024

CONV_COLS = 2 * CONV_CH + 3 * SC_CH
Q_OFF = CONV_COLS
K_OFF = Q_OFF + ATT_W
V_OFF = K_OFF + ATT_W
QI_OFF = V_OFF + ATT_W
KW_OFF = QI_OFF + IDX_HEADS * IDX_DIM
N_IN_PAD = KW_OFF + LANES

HALF = 2 ** 15
INT_MAX = 2 ** 31 - 1
NEG_INF_KEY = -2139095041
NEG_BIG = -1e30


def _cparams(sem, vmem=VMEM_LIMIT):
    return pltpu.CompilerParams(dimension_semantics=sem, vmem_limit_bytes=vmem)


def _rms(x, g, eps=1e-6):
    return x * lax.rsqrt(jnp.mean(x * x, axis=-1, keepdims=True) + eps) * g


def _sigmoid(x):
    return 1.0 / (1.0 + jnp.exp(-x))


def _rope_kernel(pos_ref, invf_ref, sgn_ref, cos_ref, sin_ref):
    ang = pos_ref[...].astype(F32) * invf_ref[...]
    cos_ref[...] = jnp.cos(ang)
    sin_ref[...] = jnp.sin(ang) * sgn_ref[...]


def _rope_tables(positions, tm):
    t = positions.size
    half = HEAD_DIM // 2
    inv_freq = ROPE_THETA ** (-jnp.arange(0, HEAD_DIM, 2, dtype=F32) / HEAD_DIM)
    invf = jnp.tile(inv_freq, LANES // half)[None, :]
    sgn = jnp.tile(jnp.concatenate([-jnp.ones((half,), F32), jnp.ones((half,), F32)]),
                   LANES // HEAD_DIM)[None, :]
    pos = jnp.broadcast_to(positions.reshape(t, 1), (t, LANES))
    row = pl.BlockSpec((tm, LANES), lambda i: (i, 0))
    one = pl.BlockSpec((1, LANES), lambda i: (0, 0))
    return pl.pallas_call(
        _rope_kernel,
        grid=(t // tm,),
        in_specs=[row, one, one],
        out_specs=[row, row],
        out_shape=[jax.ShapeDtypeStruct((t, LANES), F32)] * 2,
        compiler_params=_cparams(("parallel",)),
        name="rope_tables",
    )(pos, invf, sgn)


def _ffn_kernel(x_ref, g_ref, wg_ref, wu_ref, wd_ref, fg_ref, o_ref, h_scr, acc_scr, *, final):
    j = pl.program_id(1)

    @pl.when(j == 0)
    def _():
        h_scr[...] = _rms(x_ref[...], g_ref[...]).astype(BF16)
        acc_scr[...] = jnp.zeros_like(acc_scr)

    h = h_scr[...]
    a = jnp.dot(h, wg_ref[...], preferred_element_type=F32)
    b = jnp.dot(h, wu_ref[...], preferred_element_type=F32)
    t = (a * _sigmoid(a)) * b
    acc_scr[...] += jnp.dot(t.astype(BF16), wd_ref[...], preferred_element_type=F32)

    @pl.when(j == pl.num_programs(1) - 1)
    def _():
        y = x_ref[...] + 0.5 * acc_scr[...]
        if final:
            y = _rms(y, fg_ref[...])
        o_ref[...] = y


def _ffn(x, g, wg, wu, wd, fg, *, final, tm, tf):
    t, d = x.shape
    f = wg.shape[1]
    return pl.pallas_call(
        functools.partial(_ffn_kernel, final=final),
        grid=(t // tm, f // tf),
        in_specs=[
            pl.BlockSpec((tm, d), lambda i, j: (i, 0)),
            pl.BlockSpec((1, d), lambda i, j: (0, 0)),
            pl.BlockSpec((d, tf), lambda i, j: (0, j)),
            pl.BlockSpec((d, tf), lambda i, j: (0, j)),
            pl.BlockSpec((tf, d), lambda i, j: (j, 0)),
            pl.BlockSpec((1, d), lambda i, j: (0, 0)),
        ],
        out_specs=pl.BlockSpec((tm, d), lambda i, j: (i, 0)),
        out_shape=jax.ShapeDtypeStruct((t, d), F32),
        scratch_shapes=[pltpu.VMEM((tm, d), BF16), pltpu.VMEM((tm, d), F32)],
        compiler_params=_cparams(("parallel", "arbitrary")),
        name="ffn_final" if final else "ffn",
    )(x, g, wg, wu, wd, fg)


def _proj_kernel(x_ref, g_ref, w_ref, cos_ref, sin_ref,
                 conv_ref, qT_ref, k_ref, vT_ref, qiT_ref, kidx_ref, kwT_ref, *, kc):
    tm = x_ref.shape[0]
    h = _rms(x_ref[...], g_ref[...]).astype(BF16)
    cos = cos_ref[...]
    sin = sin_ref[...]
    lane = lax.broadcasted_iota(I32, (tm, LANES), 1)
    first_half = (lane % HEAD_DIM) < (HEAD_DIM // 2)

    def rope(t, c, s):
        rot = jnp.where(first_half, pltpu.roll(t, LANES - HEAD_DIM // 2, 1), pltpu.roll(t, HEAD_DIM // 2, 1))
        return t * c + rot * s

    conv_ref[...] = jnp.dot(h, w_ref[:, 0:CONV_COLS], preferred_element_type=F32)

    q = jnp.dot(h, w_ref[:, Q_OFF:Q_OFF + ATT_W], preferred_element_type=F32)
    for g in range(ATT_W // LANES):
        qr = rope(q[:, g * LANES:(g + 1) * LANES], cos, sin) * ATT_SCALE
        qT_ref[g * LANES:(g + 1) * LANES, :] = qr.T.astype(BF16)

    k = jnp.dot(h, w_ref[:, K_OFF:K_OFF + ATT_W], preferred_element_type=F32)
    for g in range(ATT_W // LANES):
        k_ref[:, g * LANES:(g + 1) * LANES] = rope(k[:, g * LANES:(g + 1) * LANES], cos, sin).astype(BF16)

    v = jnp.dot(h, w_ref[:, V_OFF:V_OFF + ATT_W], preferred_element_type=F32)
    for c in range(tm // kc):
        vT_ref[c] = v[c * kc:(c + 1) * kc, :].T.astype(BF16)

    qi = jnp.dot(h, w_ref[:, QI_OFF:QI_OFF + IDX_HEADS * IDX_DIM], preferred_element_type=F32)
    for g in range(IDX_HEADS * IDX_DIM // LANES):
        qiT_ref[g * LANES:(g + 1) * LANES, :] = rope(qi[:, g * LANES:(g + 1) * LANES], cos, sin).T.astype(BF16)

    kw = jnp.dot(h, w_ref[:, KW_OFF:KW_OFF + LANES], preferred_element_type=F32)
    is_kidx = lane < IDX_DIM
    kw = rope(kw, jnp.where(is_kidx, cos, 1.0), jnp.where(is_kidx, sin, 0.0))
    kidx_ref[...] = kw.astype(BF16)
    kwT_ref[...] = kw.T


def _proj(x, g, w, cos, sin, *, bsz, seq, tm, kc):
    d = x.shape[1]
    ns = seq // tm
    tok = lambda b, s: (b * ns + s, 0)
    return pl.pallas_call(
        functools.partial(_proj_kernel, kc=kc),
        grid=(bsz, ns),
        in_specs=[
            pl.BlockSpec((tm, d), tok),
            pl.BlockSpec((1, d), lambda b, s: (0, 0)),
            pl.BlockSpec((d, N_IN_PAD), lambda b, s: (0, 0)),
            pl.BlockSpec((tm, LANES), tok),
            pl.BlockSpec((tm, LANES), tok),
        ],
        out_specs=[
            pl.BlockSpec((None, tm, CONV_COLS), lambda b, s: (b, s, 0)),
            pl.BlockSpec((None, ATT_W, tm), lambda b, s: (b, 0, s)),
            pl.BlockSpec((None, tm, ATT_W), lambda b, s: (b, s, 0)),
            pl.BlockSpec((None, tm // kc, ATT_W, kc), lambda b, s: (b, s, 0, 0)),
            pl.BlockSpec((None, IDX_HEADS * IDX_DIM, tm), lambda b, s: (b, 0, s)),
            pl.BlockSpec((None, tm, LANES), lambda b, s: (b, s, 0)),
            pl.BlockSpec((None, LANES, tm), lambda b, s: (b, 0, s)),
        ],
        out_shape=[
            jax.ShapeDtypeStruct((bsz, seq, CONV_COLS), F32),
            jax.ShapeDtypeStruct((bsz, ATT_W, seq), BF16),
            jax.ShapeDtypeStruct((bsz, seq, ATT_W), BF16),
            jax.ShapeDtypeStruct((bsz, seq // kc, ATT_W, kc), BF16),
            jax.ShapeDtypeStruct((bsz, IDX_HEADS * IDX_DIM, seq), BF16),
            jax.ShapeDtypeStruct((bsz, seq, LANES), BF16),
            jax.ShapeDtypeStruct((bsz, LANES, seq), F32),
        ],
        compiler_params=_cparams(("parallel", "parallel")),
        name="in_proj",
    )(x, g, w, cos, sin)


CONV_HALO = 32
SC_HALO = 8


def _conv_kernel(c_ref, dw_ref, dwb_ref, lng_ref, lnb_ref, sw_ref, o_ref, ha_scr, ub_scr, *, rc):
    ts = c_ref.shape[0]

    @pl.when(pl.program_id(1) == 0)
    def _():
        ha_scr[0:CONV_HALO, :] = jnp.zeros((CONV_HALO, CONV_CH), F32)
        ub_scr[0:SC_HALO, :] = jnp.zeros((SC_HALO, SC_CH), F32)

    ha_scr[CONV_HALO:CONV_HALO + ts, :] = c_ref[:, 0:CONV_CH] * _sigmoid(c_ref[:, CONV_CH:2 * CONV_CH])
    ub_scr[SC_HALO:SC_HALO + ts, :] = (c_ref[:, 2 * CONV_CH + SC_CH:2 * CONV_CH + 2 * SC_CH]
                                       * c_ref[:, 2 * CONV_CH + 2 * SC_CH:2 * CONV_CH + 3 * SC_CH])

    for r in range(ts // rc):
        base = r * rc
        acc = jnp.zeros((rc, CONV_CH), F32)
        for j in range(CONV_W):
            off = base + CONV_HALO - (CONV_W - 1) + j
            acc = acc + ha_scr[off:off + rc, :] * dw_ref[j:j + 1, :]
        acc = acc + dwb_ref[...]
        mu = jnp.mean(acc, axis=-1, keepdims=True)
        cen = acc - mu
        var = jnp.mean(cen * cen, axis=-1, keepdims=True)
        y = cen * lax.rsqrt(var + 1e-5) * lng_ref[...] + lnb_ref[...]
        o_ref[base:base + rc, 0:CONV_CH] = (y * _sigmoid(y)).astype(BF16)

        accb = jnp.zeros((rc, SC_CH), F32)
        for j in range(SC_W):
            off = base + SC_HALO - (SC_W - 1) + j
            accb = accb + ub_scr[off:off + rc, :] * sw_ref[j:j + 1, :]
        bg = c_ref[base:base + rc, 2 * CONV_CH:2 * CONV_CH + SC_CH]
        o_ref[base:base + rc, CONV_CH:CONV_CH + SC_CH] = (bg * accb).astype(BF16)

    ha_scr[0:CONV_HALO, :] = ha_scr[ts:ts + CONV_HALO, :]
    ub_scr[0:SC_HALO, :] = ub_scr[ts:ts + SC_HALO, :]


def _conv_mixers(conv_in, dw, dwb, lng, lnb, sw, *, ts, rc):
    bsz, seq, _ = conv_in.shape
    small = lambda r, c: pl.BlockSpec((r, c), lambda b, s: (0, 0))
    return pl.pallas_call(
        functools.partial(_conv_kernel, rc=rc),
        grid=(bsz, seq // ts),
        in_specs=[
            pl.BlockSpec((None, ts, CONV_COLS), lambda b, s: (b, s, 0)),
            small(CONV_W, CONV_CH), small(1, CONV_CH), small(1, CONV_CH), small(1, CONV_CH),
            small(SC_W, SC_CH),
        ],
        out_specs=pl.BlockSpec((None, ts, CONV_CH + SC_CH), lambda b, s: (b, s, 0)),
        out_shape=jax.ShapeDtypeStruct((bsz, seq, CONV_CH + SC_CH), BF16),
        scratch_shapes=[pltpu.VMEM((ts + CONV_HALO, CONV_CH), F32),
                        pltpu.VMEM((ts + SC_HALO, SC_CH), F32)],
        compiler_params=_cparams(("arbitrary", "arbitrary")),
        name="conv_mixers",
    )(conv_in, dw, dwb, lng, lnb, sw)


def _sortable(x):
    b = lax.bitcast_convert_type(x, I32)
    return b ^ ((b >> 31) & INT_MAX)


def _dsa_kernel(qT_ref, qiT_ref, wT_ref, k_ref, vT_ref, kidx_ref, o_ref,
                key_scr, k16_scr, oT_scr, qpad_scr, m_scr, l_scr, *, kc, topk):
    qb = qT_ref.shape[1]
    i = pl.program_id(1)
    nk = (i * qb) // kc + 1
    q_pos = i * qb + lax.broadcasted_iota(I32, (1, qb), 1)
    row_iota = lax.broadcasted_iota(I32, (kc, qb), 0)

    w = wT_ref[...] * IDX_SCALE
    zeros_half = jnp.zeros((LANES - IDX_DIM, qb), BF16)
    qi = [jnp.concatenate([qiT_ref[h * IDX_DIM:(h + 1) * IDX_DIM, :], zeros_half], axis=0)
          for h in range(IDX_HEADS)]

    def score_chunk(c, diagonal):
        kcs = kidx_ref[pl.ds(pl.multiple_of(c * kc, kc), kc), :]
        sc = jnp.zeros((kc, qb), F32)
        for h in range(IDX_HEADS):
            lg = jnp.dot(kcs, qi[h], preferred_element_type=F32)
            sc = sc + w[h:h + 1, :] * jnp.maximum(lg, 0.0)
        if diagonal:
            sc = jnp.where((c * kc + row_iota) <= q_pos, sc, -jnp.inf)
        key = _sortable(sc)
        key_scr[c] = key
        k16_scr[c] = (key >> 16).astype(I16)

    def score_body(c, carry):
        score_chunk(c, False)
        return carry

    lax.fori_loop(0, nk - 1, score_body, 0)
    score_chunk(nk - 1, True)

    fold = 128

    def count16(mid):
        midb = jnp.broadcast_to(mid, (kc, qb)).astype(I16)

        def body(c, acc):
            ones = jnp.where(k16_scr[c] >= midb, jnp.int16(1), jnp.int16(0))
            for r in range(kc // fold):
                acc = acc + ones[r * fold:(r + 1) * fold, :]
            return acc

        acc = lax.fori_loop(0, nk, body, jnp.zeros((fold, qb), I16))
        return jnp.sum(acc.astype(I32), axis=0, keepdims=True)

    def bisect16(want, cnt_all):
        def step(_, carry):
            lo, hi, cnt_lo, cnt_hi = carry
            mid = (lo + hi) >> 1
            cnt = jnp.where(mid == -HALF, cnt_all, count16(mid))
            ok = cnt >= want
            return (jnp.where(ok, mid, lo), jnp.where(ok, hi, mid),
                    jnp.where(ok, cnt, cnt_lo), jnp.where(ok, cnt_hi, cnt))
        zero = jnp.zeros((1, qb), I32)
        t, _, cnt_t, cnt_above = lax.fori_loop(0, 16, step, (zero - HALF, zero + HALF, cnt_all, zero))
        return t, cnt_t, cnt_above

    n_stored = nk * kc + jnp.zeros((1, qb), I32)
    t_hi, cge_hi, cgt_hi = bisect16(topk, n_stored)

    def low_body(c, carry):
        kk = key_scr[c]
        low = (kk & (2 * HALF - 1)) - HALF
        k16_scr[c] = jnp.where((kk >> 16) == t_hi, low, -HALF).astype(I16)
        return carry

    lax.fori_loop(0, nk, low_body, 0)
    t_lo, cge_lo, cgt_lo = bisect16(topk - cgt_hi, cge_hi - cgt_hi)

    tau = t_hi * (2 * HALF) + (t_lo + HALF)
    full = tau > NEG_INF_KEY
    tau_eff = jnp.maximum(tau, NEG_INF_KEY)
    need = jnp.where(full, topk - (cgt_hi + cgt_lo), 0).astype(F32)
    ltri = jnp.where(lax.broadcasted_iota(I32, (kc, kc), 0) >= lax.broadcasted_iota(I32, (kc, kc), 1),
                     1.0, 0.0).astype(BF16)

    def bias_body(c, run):
        kk = key_scr[c]
        eq = kk == tau_eff
        rank = jnp.dot(ltri, jnp.where(eq, 1.0, 0.0).astype(BF16), preferred_element_type=F32) + run
        tie = jnp.where(rank <= need, 0.0, -jnp.inf)
        bias = jnp.where(kk > tau_eff, 0.0, jnp.where(eq, tie, -jnp.inf))
        key_scr[c] = lax.bitcast_convert_type(bias.astype(F32), I32)
        return rank[kc - 1:kc, :]

    lax.fori_loop(0, nk, bias_body, jnp.zeros((1, qb), F32))

    zeros_head = jnp.zeros((HEAD_DIM, qb), BF16)
    for h in range(N_HEADS):
        qh = qT_ref[h * HEAD_DIM:(h + 1) * HEAD_DIM, :]
        qpad_scr[h] = jnp.concatenate([qh, zeros_head] if h % 2 == 0 else [zeros_head, qh], axis=0)
    m_scr[...] = jnp.full((N_HEADS, qb), NEG_BIG, F32)
    l_scr[...] = jnp.zeros((N_HEADS, qb), F32)
    oT_scr[...] = jnp.zeros((ATT_W, qb), F32)

    def att_body(c, carry):
        bias = lax.bitcast_convert_type(key_scr[c], F32)
        row0 = pl.multiple_of(c * kc, kc)
        m_old = m_scr[...]
        l_old = l_scr[...]
        s_all = [jnp.dot(k_ref[pl.ds(row0, kc), (h // 2) * LANES:(h // 2 + 1) * LANES], qpad_scr[h],
                         preferred_element_type=F32) + bias for h in range(N_HEADS)]
        m_new, l_new, alpha_rows, p_all = [], [], [], []
        for h in range(N_HEADS):
            m = m_old[h:h + 1, :]
            mn = jnp.maximum(m, jnp.max(s_all[h], axis=0, keepdims=True))
            p = jnp.exp(s_all[h] - mn)
            alpha = jnp.exp(m - mn)
            m_new.append(mn)
            l_new.append(alpha * l_old[h:h + 1, :] + jnp.sum(p, axis=0, keepdims=True))
            alpha_rows.append(jnp.broadcast_to(alpha, (HEAD_DIM, qb)))
            p_all.append(p.astype(BF16))
        pv = [jnp.dot(vT_ref[c, h * HEAD_DIM:(h + 1) * HEAD_DIM, :], p_all[h], preferred_element_type=F32)
              for h in range(N_HEADS)]
        oT_scr[...] = jnp.concatenate(alpha_rows, axis=0) * oT_scr[...] + jnp.concatenate(pv, axis=0)
        m_scr[...] = jnp.concatenate(m_new, axis=0)
        l_scr[...] = jnp.concatenate(l_new, axis=0)
        return carry

    lax.fori_loop(0, nk, att_body, 0)
    for h in range(N_HEADS):
        hs = slice(h * HEAD_DIM, (h + 1) * HEAD_DIM)
        oT_scr[hs, :] = oT_scr[hs, :] / l_scr[h:h + 1, :]

    o_ref[...] = oT_scr[...].T.astype(BF16)


def _dsa(qT, qiT, kwT, k, vT, kidx, *, qb, kc):
    bsz, seq, _ = k.shape
    assert kc % qb == 0 and seq % kc == 0
    topk = min(TOPK_MAX, seq // 4)
    w_row_block = IDX_DIM // 8
    return pl.pallas_call(
        functools.partial(_dsa_kernel, kc=kc, topk=topk),
        grid=(bsz, seq // qb),
        in_specs=[
            pl.BlockSpec((None, ATT_W, qb), lambda b, i: (b, 0, i)),
            pl.BlockSpec((None, IDX_HEADS * IDX_DIM, qb), lambda b, i: (b, 0, i)),
            pl.BlockSpec((None, 8, qb), lambda b, i: (b, w_row_block, i)),
            pl.BlockSpec((None, seq, ATT_W), lambda b, i: (b, 0, 0)),
            pl.BlockSpec((None, seq // kc, ATT_W, kc), lambda b, i: (b, 0, 0, 0)),
            pl.BlockSpec((None, seq, LANES), lambda b, i: (b, 0, 0)),
        ],
        out_specs=pl.BlockSpec((None, qb, ATT_W), lambda b, i: (b, i, 0)),
        out_shape=jax.ShapeDtypeStruct((bsz, seq, ATT_W), BF16),
        scratch_shapes=[pltpu.VMEM((seq // kc, kc, qb), I32),
                        pltpu.VMEM((seq // kc, kc, qb), I16),
                        pltpu.VMEM((ATT_W, qb), F32),
                        pltpu.VMEM((N_HEADS, LANES, qb), BF16),
                        pltpu.VMEM((N_HEADS, qb), F32),
                        pltpu.VMEM((N_HEADS, qb), F32)],
        compiler_params=_cparams(("parallel", "parallel")),
        name="dsa_attention",
    )(qT, qiT, kwT, k, vT, kidx)


def _outproj_kernel(x_ref, yab_ref, yc_ref, wab_ref, wc_ref, o_ref):
    o_ref[...] = (x_ref[...]
                  + jnp.dot(yab_ref[...], wab_ref[...], preferred_element_type=F32)
                  + jnp.dot(yc_ref[...], wc_ref[...], preferred_element_type=F32))


def _outproj(x, yab, yc, wab, wc, *, tm):
    t, d = x.shape
    row = lambda c: pl.BlockSpec((tm, c), lambda i: (i, 0))
    full = lambda r, c: pl.BlockSpec((r, c), lambda i: (0, 0))
    return pl.pallas_call(
        _outproj_kernel,
        grid=(t // tm,),
        in_specs=[row(d), row(yab.shape[1]), row(yc.shape[1]), full(*wab.shape), full(*wc.shape)],
        out_specs=row(d),
        out_shape=jax.ShapeDtypeStruct((t, d), F32),
        compiler_params=_cparams(("parallel",)),
        name="out_proj",
    )(x, yab, yc, wab, wc)


def _memkv_kernel(mem_ref, g_ref, wkv_ref, kT_ref, v_ref):
    d = mem_ref.shape[1]
    mn = _rms(mem_ref[...], g_ref[...]).astype(BF16)
    kv = jnp.dot(mn, wkv_ref[...], preferred_element_type=F32)
    kT_ref[...] = kv[:, 0:d].T.astype(BF16)
    v_ref[...] = kv[:, d:2 * d].astype(BF16)


def _memkv(mem, g, wkv):
    bsz, m, d = mem.shape
    return pl.pallas_call(
        _memkv_kernel,
        grid=(bsz,),
        in_specs=[pl.BlockSpec((None, m, d), lambda b: (b, 0, 0)),
                  pl.BlockSpec((1, d), lambda b: (0, 0)),
                  pl.BlockSpec((d, 2 * d), lambda b: (0, 0))],
        out_specs=[pl.BlockSpec((None, d, m), lambda b: (b, 0, 0)),
                   pl.BlockSpec((None, m, d), lambda b: (b, 0, 0))],
        out_shape=[jax.ShapeDtypeStruct((bsz, d, m), BF16), jax.ShapeDtypeStruct((bsz, m, d), BF16)],
        compiler_params=_cparams(("parallel",)),
        name="mem_kv",
    )(mem, g, wkv)


def _xattn_kernel(x_ref, g_ref, wq_ref, kT_ref, v_ref, wo_ref, o_ref):
    d = x_ref.shape[1]
    hd = d // XA_HEADS
    x = x_ref[...]
    hq = _rms(x, g_ref[...]).astype(BF16)
    q = (jnp.dot(hq, wq_ref[...], preferred_element_type=F32) * (hd ** -0.5)).astype(BF16)
    outs = []
    for h in range(XA_HEADS):
        s = jnp.dot(q[:, h * hd:(h + 1) * hd], kT_ref[h * hd:(h + 1) * hd, :], preferred_element_type=F32)
        m = jnp.max(s, axis=-1, keepdims=True)
        p = jnp.exp(s - m)
        l = jnp.sum(p, axis=-1, keepdims=True)
        o = jnp.dot(p.astype(BF16), v_ref[:, h * hd:(h + 1) * hd], preferred_element_type=F32) / l
        outs.append(o.astype(BF16))
    o_ref[...] = x + jnp.dot(jnp.concatenate(outs, axis=-1), wo_ref[...], preferred_element_type=F32)


def _xattn(x, g, wq, kT, v, wo, *, bsz, seq, tm):
    d = x.shape[1]
    m = v.shape[1]
    ns = seq // tm
    tok = pl.BlockSpec((tm, d), lambda b, s: (b * ns + s, 0))
    return pl.pallas_call(
        _xattn_kernel,
        grid=(bsz, ns),
        in_specs=[tok,
                  pl.BlockSpec((1, d), lambda b, s: (0, 0)),
                  pl.BlockSpec((d, d), lambda b, s: (0, 0)),
                  pl.BlockSpec((None, d, m), lambda b, s: (b, 0, 0)),
                  pl.BlockSpec((None, m, d), lambda b, s: (b, 0, 0)),
                  pl.BlockSpec((d, d), lambda b, s: (0, 0))],
        out_specs=tok,
        out_shape=jax.ShapeDtypeStruct((bsz * seq, d), F32),
        compiler_params=_cparams(("parallel", "parallel")),
        name="mem_xattn",
    )(x, g, wq, kT, v, wo)


def _tiles(seq, d_ff):
    assert seq % 512 == 0, "sequence length must be a multiple of 512"
    tf = d_ff // 2 if (d_ff // 2) % LANES == 0 else d_ff
    return dict(
        tm_ffn=512, tf=tf,
        tm_proj=512,
        ts_conv=256, rc_conv=64,
        qb=256, kc=512,
        tm_out=1024, tm_xa=512, tm_rope=1024,
    )


def kernel(x, mem, positions, ffn1_norm, ffn1_w_gate, ffn1_w_up, ffn1_w_down, mix_norm, w_in,
           conf_dw, conf_dw_b, conf_ln_g, conf_ln_b, sc_dw, w_out, xa_norm, mem_norm,
           xa_wq, xa_wkv, xa_wo, ffn2_norm, ffn2_w_gate, ffn2_w_up, ffn2_w_down, final_norm):
    bsz, seq, d = x.shape
    depth = w_in.shape[0]
    t = bsz * seq
    tl = _tiles(seq, ffn1_w_gate.shape[2])
    bf = lambda a: a.astype(BF16)
    row = lambda a: a.reshape(1, -1)

    cos, sin = _rope_tables(positions, tl["tm_rope"])
    xf = x.reshape(t, d)
    w_in_pad = jnp.pad(w_in, ((0, 0), (0, 0), (0, N_IN_PAD - w_in.shape[2])))

    for l in range(depth):
        xf = _ffn(xf, row(ffn1_norm[l]), bf(ffn1_w_gate[l]), bf(ffn1_w_up[l]), bf(ffn1_w_down[l]),
                  row(final_norm), final=False, tm=tl["tm_ffn"], tf=tl["tf"])

        conv_in, qT, k, vT, qiT, kidx, kwT = _proj(
            xf, row(mix_norm[l]), bf(w_in_pad[l]), cos, sin, bsz=bsz, seq=seq, tm=tl["tm_proj"], kc=tl["kc"])
        yab = _conv_mixers(conv_in, conf_dw[l], row(conf_dw_b[l]), row(conf_ln_g[l]), row(conf_ln_b[l]),
                           sc_dw[l], ts=tl["ts_conv"], rc=tl["rc_conv"])
        yc = _dsa(qT, qiT, kwT, k, vT, kidx, qb=tl["qb"], kc=tl["kc"])
        n_ab = CONV_CH + SC_CH
        xf = _outproj(xf, yab.reshape(t, n_ab), yc.reshape(t, ATT_W),
                      bf(w_out[l, 0:n_ab]), bf(w_out[l, n_ab:]), tm=tl["tm_out"])

        kT_mem, v_mem = _memkv(mem, row(mem_norm[l]), bf(xa_wkv[l]))
        xf = _xattn(xf, row(xa_norm[l]), bf(xa_wq[l]), kT_mem, v_mem, bf(xa_wo[l]),
                    bsz=bsz, seq=seq, tm=tl["tm_xa"])

        xf = _ffn(xf, row(ffn2_norm[l]), bf(ffn2_w_gate[l]), bf(ffn2_w_up[l]), bf(ffn2_w_down[l]),
                  row(final_norm), final=(l == depth - 1), tm=tl["tm_ffn"], tf=tl["tf"])

    return xf.reshape(bsz, seq, d)
```

```python
import functools
import math

import jax
import jax.numpy as jnp
from jax import lax
from jax.experimental import pallas as pl
from jax.experimental.pallas import tpu as pltpu

F32 = jnp.float32
BF16 = jnp.bfloat16
I32 = jnp.int32
I16 = jnp.int16

CONV_CH = 256
CONV_W = 31
SC_CH = 256
SC_W = 3
N_HEADS = 8
HEAD_GROUP = 8
HEAD_DIM = 64
ATT_W = N_HEADS * HEAD_DIM
IDX_HEADS = 4
IDX_DIM = 64
TOPK_MAX = 256
ROPE_THETA = 10000.0
XA_HEADS = 4
IDX_SCALE = (IDX_HEADS ** -0.5) * (IDX_DIM ** -0.5)
ATT_SCALE = HEAD_DIM ** -0.5
QK_SCALE = ATT_SCALE * math.log2(math.e)

LANES = 128
SUBLANES = 8
VMEM_LIMIT = 58 * 1024 * 1024

CONV_COLS = 2 * CONV_CH + 3 * SC_CH
Q_OFF = CONV_COLS
K_OFF = Q_OFF + ATT_W
V_OFF = K_OFF + ATT_W
QI_OFF = V_OFF + ATT_W
KW_OFF = QI_OFF + IDX_HEADS * IDX_DIM
N_IN_PAD = KW_OFF + LANES

HALF = 2 ** 15
INT_MAX = 2 ** 31 - 1
NEG_INF_KEY = -2139095041
NEG_BIG = -1e30


def _cparams(sem, vmem=VMEM_LIMIT):
    return pltpu.CompilerParams(dimension_semantics=sem, vmem_limit_bytes=vmem)


def _rms(x, g, eps=1e-6):
    return x * lax.rsqrt(jnp.mean(x * x, axis=-1, keepdims=True) + eps) * g


def _sigmoid(x):
    return 1.0 / (1.0 + jnp.exp(-x))


def _rope_kernel(pos_ref, invf_ref, sgn_ref, cos_ref, sin_ref):
    ang = pos_ref[...].astype(F32) * invf_ref[...]
    cos_ref[...] = jnp.cos(ang)
    sin_ref[...] = jnp.sin(ang) * sgn_ref[...]


def _rope_tables(positions, tm):
    t = positions.size
    half = HEAD_DIM // 2
    inv_freq = ROPE_THETA ** (-jnp.arange(0, HEAD_DIM, 2, dtype=F32) / HEAD_DIM)
    invf = jnp.tile(inv_freq, LANES // half)[None, :]
    sgn = jnp.tile(jnp.concatenate([-jnp.ones((half,), F32), jnp.ones((half,), F32)]),
                   LANES // HEAD_DIM)[None, :]
    pos = jnp.broadcast_to(positions.reshape(t, 1), (t, LANES))
    row = pl.BlockSpec((tm, LANES), lambda i: (i, 0))
    one = pl.BlockSpec((1, LANES), lambda i: (0, 0))
    return pl.pallas_call(
        _rope_kernel,
        grid=(t // tm,),
        in_specs=[row, one, one],
        out_specs=[row, row],
        out_shape=[jax.ShapeDtypeStruct((t, LANES), F32)] * 2,
        compiler_params=_cparams(("parallel",)),
        name="rope_tables",
    )(pos, invf, sgn)


def _ffn_kernel(x_ref, g_ref, wg_ref, wu_ref, wd_ref, fg_ref, o_ref, h_scr, acc_scr, *, final):
    j = pl.program_id(1)

    @pl.when(j == 0)
    def _():
        h_scr[...] = _rms(x_ref[...], g_ref[...]).astype(BF16)
        acc_scr[...] = jnp.zeros_like(acc_scr)

    h = h_scr[...]
    a = jnp.dot(h, wg_ref[...], preferred_element_type=F32)
    b = jnp.dot(h, wu_ref[...], preferred_element_type=F32)
    t = (a * _sigmoid(a)) * b
    acc_scr[...] += jnp.dot(t.astype(BF16), wd_ref[...], preferred_element_type=F32)

    @pl.when(j == pl.num_programs(1) - 1)
    def _():
        y = x_ref[...] + 0.5 * acc_scr[...]
        if final:
            y = _rms(y, fg_ref[...])
        o_ref[...] = y


def _ffn(x, g, wg, wu, wd, fg, *, final, tm, tf):
    t, d = x.shape
    f = wg.shape[1]
    return pl.pallas_call(
        functools.partial(_ffn_kernel, final=final),
        grid=(t // tm, f // tf),
        in_specs=[
            pl.BlockSpec((tm, d), lambda i, j: (i, 0)),
            pl.BlockSpec((1, d), lambda i, j: (0, 0)),
            pl.BlockSpec((d, tf), lambda i, j: (0, j)),
            pl.BlockSpec((d, tf), lambda i, j: (0, j)),
            pl.BlockSpec((tf, d), lambda i, j: (j, 0)),
            pl.BlockSpec((1, d), lambda i, j: (0, 0)),
        ],
        out_specs=pl.BlockSpec((tm, d), lambda i, j: (i, 0)),
        out_shape=jax.ShapeDtypeStruct((t, d), F32),
        scratch_shapes=[pltpu.VMEM((tm, d), BF16), pltpu.VMEM((tm, d), F32)],
        compiler_params=_cparams(("parallel", "arbitrary")),
        name="ffn_final" if final else "ffn",
    )(x, g, wg, wu, wd, fg)


def _proj_kernel(x_ref, g_ref, w_ref, cos_ref, sin_ref,
                 conv_ref, qT_ref, k_ref, vT_ref, qiT_ref, kidx_ref, kwT_ref, *, kc):
    tm = x_ref.shape[0]
    h = _rms(x_ref[...], g_ref[...]).astype(BF16)
    cos = cos_ref[...]
    sin = sin_ref[...]
    lane = lax.broadcasted_iota(I32, (tm, LANES), 1)
    first_half = (lane % HEAD_DIM) < (HEAD_DIM // 2)

    def rope(t, c, s):
        rot = jnp.where(first_half, pltpu.roll(t, LANES - HEAD_DIM // 2, 1), pltpu.roll(t, HEAD_DIM // 2, 1))
        return t * c + rot * s

    conv_ref[...] = jnp.dot(h, w_ref[:, 0:CONV_COLS], preferred_element_type=F32)

    q = jnp.dot(h, w_ref[:, Q_OFF:Q_OFF + ATT_W], preferred_element_type=F32)
    for g in range(ATT_W // LANES):
        qr = rope(q[:, g * LANES:(g + 1) * LANES], cos, sin) * QK_SCALE
        qT_ref[g * LANES:(g + 1) * LANES, :] = qr.T.astype(BF16)

    k = jnp.dot(h, w_ref[:, K_OFF:K_OFF + ATT_W], preferred_element_type=F32)
    for g in range(ATT_W // LANES):
        k_ref[:, g * LANES:(g + 1) * LANES] = rope(k[:, g * LANES:(g + 1) * LANES], cos, sin).astype(BF16)

    v = jnp.dot(h, w_ref[:, V_OFF:V_OFF + ATT_W], preferred_element_type=F32)
    for c in range(tm // kc):
        vT_ref[c] = v[c * kc:(c + 1) * kc, :].T.astype(BF16)

    qi = jnp.dot(h, w_ref[:, QI_OFF:QI_OFF + IDX_HEADS * IDX_DIM], preferred_element_type=F32)
    for g in range(IDX_HEADS * IDX_DIM // LANES):
        qiT_ref[g * LANES:(g + 1) * LANES, :] = rope(qi[:, g * LANES:(g + 1) * LANES], cos, sin).T.astype(BF16)

    kw = jnp.dot(h, w_ref[:, KW_OFF:KW_OFF + LANES], preferred_element_type=F32)
    is_kidx = lane < IDX_DIM
    kw = rope(kw, jnp.where(is_kidx, cos, 1.0), jnp.where(is_kidx, sin, 0.0))
    kidx_ref[...] = kw.astype(BF16)
    kwT_ref[...] = kw.T


def _proj(x, g, w, cos, sin, *, bsz, seq, tm, kc):
    d = x.shape[1]
    ns = seq // tm
    tok = lambda b, s: (b * ns + s, 0)
    return pl.pallas_call(
        functools.partial(_proj_kernel, kc=kc),
        grid=(bsz, ns),
        in_specs=[
            pl.BlockSpec((tm, d), tok),
            pl.BlockSpec((1, d), lambda b, s: (0, 0)),
            pl.BlockSpec((d, N_IN_PAD), lambda b, s: (0, 0)),
            pl.BlockSpec((tm, LANES), tok),
            pl.BlockSpec((tm, LANES), tok),
        ],
        out_specs=[
            pl.BlockSpec((None, tm, CONV_COLS), lambda b, s: (b, s, 0)),
            pl.BlockSpec((None, ATT_W, tm), lambda b, s: (b, 0, s)),
            pl.BlockSpec((None, tm, ATT_W), lambda b, s: (b, s, 0)),
            pl.BlockSpec((None, tm // kc, ATT_W, kc), lambda b, s: (b, s, 0, 0)),
            pl.BlockSpec((None, IDX_HEADS * IDX_DIM, tm), lambda b, s: (b, 0, s)),
            pl.BlockSpec((None, tm, LANES), lambda b, s: (b, s, 0)),
            pl.BlockSpec((None, LANES, tm), lambda b, s: (b, 0, s)),
        ],
        out_shape=[
            jax.ShapeDtypeStruct((bsz, seq, CONV_COLS), F32),
            jax.ShapeDtypeStruct((bsz, ATT_W, seq), BF16),
            jax.ShapeDtypeStruct((bsz, seq, ATT_W), BF16),
            jax.ShapeDtypeStruct((bsz, seq // kc, ATT_W, kc), BF16),
            jax.ShapeDtypeStruct((bsz, IDX_HEADS * IDX_DIM, seq), BF16),
            jax.ShapeDtypeStruct((bsz, seq, LANES), BF16),
            jax.ShapeDtypeStruct((bsz, LANES, seq), F32),
        ],
        compiler_params=_cparams(("parallel", "parallel")),
        name="in_proj",
    )(x, g, w, cos, sin)


CONV_HALO = 32
SC_HALO = 8


def _conv_kernel(c_ref, dw_ref, dwb_ref, lng_ref, lnb_ref, sw_ref, o_ref, ha_scr, ub_scr, sh_scr, *, rc):
    ts = c_ref.shape[0]
    sub = SUBLANES
    first_tap = CONV_HALO - (CONV_W - 1)

    @pl.when(pl.program_id(1) == 0)
    def _():
        ha_scr[0:CONV_HALO, :] = jnp.zeros((CONV_HALO, CONV_CH), F32)
        ub_scr[0:SC_HALO, :] = jnp.zeros((SC_HALO, SC_CH), F32)

    ha_scr[CONV_HALO:CONV_HALO + ts, :] = c_ref[:, 0:CONV_CH] * _sigmoid(c_ref[:, CONV_CH:2 * CONV_CH])
    ub_scr[SC_HALO:SC_HALO + ts, :] = (c_ref[:, 2 * CONV_CH + SC_CH:2 * CONV_CH + 2 * SC_CH]
                                       * c_ref[:, 2 * CONV_CH + 2 * SC_CH:2 * CONV_CH + 3 * SC_CH])
    span = sh_scr.shape[1]
    for ph in range(1, sub):
        sh_scr[ph - 1] = ha_scr[ph:ph + span, :]

    for r in range(ts // rc):
        base = r * rc
        acc = jnp.zeros((rc, CONV_CH), F32)
        for j in range(CONV_W):
            off = base + first_tap + j
            ph = off % sub
            src = ha_scr[off:off + rc, :] if ph == 0 else sh_scr[ph - 1, off - ph:off - ph + rc, :]
            acc = acc + src * dw_ref[j:j + 1, :]
        acc = acc + dwb_ref[...]
        mu = jnp.mean(acc, axis=-1, keepdims=True)
        cen = acc - mu
        var = jnp.mean(cen * cen, axis=-1, keepdims=True)
        y = cen * lax.rsqrt(var + 1e-5) * lng_ref[...] + lnb_ref[...]
        o_ref[base:base + rc, 0:CONV_CH] = (y * _sigmoid(y)).astype(BF16)

        accb = jnp.zeros((rc, SC_CH), F32)
        for j in range(SC_W):
            off = base + SC_HALO - (SC_W - 1) + j
            accb = accb + ub_scr[off:off + rc, :] * sw_ref[j:j + 1, :]
        bg = c_ref[base:base + rc, 2 * CONV_CH:2 * CONV_CH + SC_CH]
        o_ref[base:base + rc, CONV_CH:CONV_CH + SC_CH] = (bg * accb).astype(BF16)

    ha_scr[0:CONV_HALO, :] = ha_scr[ts:ts + CONV_HALO, :]
    ub_scr[0:SC_HALO, :] = ub_scr[ts:ts + SC_HALO, :]


def _conv_mixers(conv_in, dw, dwb, lng, lnb, sw, *, ts, rc):
    bsz, seq, _ = conv_in.shape
    small = lambda r, c: pl.BlockSpec((r, c), lambda b, s: (0, 0))
    return pl.pallas_call(
        functools.partial(_conv_kernel, rc=rc),
        grid=(bsz, seq // ts),
        in_specs=[
            pl.BlockSpec((None, ts, CONV_COLS), lambda b, s: (b, s, 0)),
            small(CONV_W, CONV_CH), small(1, CONV_CH), small(1, CONV_CH), small(1, CONV_CH),
            small(SC_W, SC_CH),
        ],
        out_specs=pl.BlockSpec((None, ts, CONV_CH + SC_CH), lambda b, s: (b, s, 0)),
        out_shape=jax.ShapeDtypeStruct((bsz, seq, CONV_CH + SC_CH), BF16),
        scratch_shapes=[pltpu.VMEM((ts + CONV_HALO, CONV_CH), F32),
                        pltpu.VMEM((ts + SC_HALO, SC_CH), F32),
                        pltpu.VMEM((SUBLANES - 1, ts + CONV_HALO - SUBLANES, CONV_CH), F32)],
        compiler_params=_cparams(("arbitrary", "arbitrary")),
        name="conv_mixers",
    )(conv_in, dw, dwb, lng, lnb, sw)


def _sortable(x):
    b = lax.bitcast_convert_type(x, I32)
    return b ^ ((b >> 31) & INT_MAX)


def _dsa_kernel(qT_ref, qiT_ref, wT_ref, k_ref, vT_ref, kidx_ref, o_ref,
                key_scr, k16_scr, oT_scr, qpad_scr, m_scr, l_scr, *, kc, topk):
    qb = qT_ref.shape[1]
    i = pl.program_id(1)
    nk = (i * qb) // kc + 1
    q_pos = i * qb + lax.broadcasted_iota(I32, (1, qb), 1)
    row_iota = lax.broadcasted_iota(I32, (kc, qb), 0)

    w = wT_ref[...] * IDX_SCALE
    zeros_half = jnp.zeros((LANES - IDX_DIM, qb), BF16)
    qi = [jnp.concatenate([qiT_ref[h * IDX_DIM:(h + 1) * IDX_DIM, :], zeros_half], axis=0)
          for h in range(IDX_HEADS)]

    def score_chunk(c, diagonal):
        kcs = kidx_ref[pl.ds(pl.multiple_of(c * kc, kc), kc), :]
        sc = jnp.zeros((kc, qb), F32)
        for h in range(IDX_HEADS):
            lg = jnp.dot(kcs, qi[h], preferred_element_type=F32)
            sc = sc + w[h:h + 1, :] * jnp.maximum(lg, 0.0)
        if diagonal:
            sc = jnp.where((c * kc + row_iota) <= q_pos, sc, -jnp.inf)
        key = _sortable(sc)
        key_scr[c] = key
        k16_scr[c] = (key >> 16).astype(I16)

    def score_body(c, carry):
        score_chunk(c, False)
        return carry

    lax.fori_loop(0, nk - 1, score_body, 0)
    score_chunk(nk - 1, True)

    fold = 128

    def count16(mid):
        midb = jnp.broadcast_to(mid, (kc, qb)).astype(I16)

        def body(c, acc):
            ones = jnp.where(k16_scr[c] >= midb, jnp.int16(1), jnp.int16(0))
            for r in range(kc // fold):
                acc = acc + ones[r * fold:(r + 1) * fold, :]
            return acc

        acc = lax.fori_loop(0, nk, body, jnp.zeros((fold, qb), I16))
        return jnp.sum(acc.astype(I32), axis=0, keepdims=True)

    def bisect16(want, cnt_all):
        def step(_, carry):
            lo, hi, cnt_lo, cnt_hi = carry
            mid = (lo + hi) >> 1
            cnt = jnp.where(mid == -HALF, cnt_all, count16(mid))
            ok = cnt >= want
            return (jnp.where(ok, mid, lo), jnp.where(ok, hi, mid),
                    jnp.where(ok, cnt, cnt_lo), jnp.where(ok, cnt_hi, cnt))
        zero = jnp.zeros((1, qb), I32)
        t, _, cnt_t, cnt_above = lax.fori_loop(0, 16, step, (zero - HALF, zero + HALF, cnt_all, zero))
        return t, cnt_t, cnt_above

    n_stored = nk * kc + jnp.zeros((1, qb), I32)
    t_hi, cge_hi, cgt_hi = bisect16(topk, n_stored)

    def low_body(c, carry):
        kk = key_scr[c]
        low = (kk & (2 * HALF - 1)) - HALF
        k16_scr[c] = jnp.where((kk >> 16) == t_hi, low, -HALF).astype(I16)
        return carry

    lax.fori_loop(0, nk, low_body, 0)
    t_lo, cge_lo, cgt_lo = bisect16(topk - cgt_hi, cge_hi - cgt_hi)

    tau = t_hi * (2 * HALF) + (t_lo + HALF)
    full = tau > NEG_INF_KEY
    tau_eff = jnp.maximum(tau, NEG_INF_KEY)
    need = jnp.where(full, topk - (cgt_hi + cgt_lo), 0).astype(F32)
    ltri = jnp.where(lax.broadcasted_iota(I32, (kc, kc), 0) >= lax.broadcasted_iota(I32, (kc, kc), 1),
                     1.0, 0.0).astype(BF16)

    zeros_head = jnp.zeros((HEAD_DIM, qb), BF16)
    for h in range(N_HEADS):
        qh = qT_ref[h * HEAD_DIM:(h + 1) * HEAD_DIM, :]
        qpad_scr[h] = jnp.concatenate([qh, zeros_head] if h % 2 == 0 else [zeros_head, qh], axis=0)
    m_scr[...] = jnp.full((N_HEADS, qb), NEG_BIG, F32)
    l_scr[...] = jnp.zeros((N_HEADS, qb), F32)
    oT_scr[...] = jnp.zeros((ATT_W, qb), F32)

    ones_rows = jnp.ones((16, kc), BF16)

    def att_body(c, run):
        kk = key_scr[c]
        eq = kk == tau_eff
        rank = jnp.dot(ltri, jnp.where(eq, 1.0, 0.0).astype(BF16), preferred_element_type=F32) + run
        tie = jnp.where(rank <= need, 0.0, -jnp.inf)
        bias = jnp.where(kk > tau_eff, 0.0, jnp.where(eq, tie, -jnp.inf))
        row0 = pl.multiple_of(c * kc, kc)
        m_old = m_scr[...]
        l_old = l_scr[...]
        m_new, alphas, pv = [], [], []
        for g in range(N_HEADS // HEAD_GROUP):
            heads = range(g * HEAD_GROUP, (g + 1) * HEAD_GROUP)
            s_grp = [jnp.dot(k_ref[pl.ds(row0, kc), (h // 2) * LANES:(h // 2 + 1) * LANES], qpad_scr[h],
                             preferred_element_type=F32) + bias for h in heads]
            p_grp = []
            for h, s in zip(heads, s_grp):
                m = m_old[h:h + 1, :]
                mn = jnp.maximum(m, jnp.max(s, axis=0, keepdims=True))
                p_grp.append(jnp.exp2(s - mn).astype(BF16))
                alphas.append(jnp.exp2(m - mn))
                m_new.append(mn)
            pv += [jnp.dot(jnp.concatenate([vT_ref[c, h * HEAD_DIM:(h + 1) * HEAD_DIM, :], ones_rows], axis=0),
                           p, preferred_element_type=F32) for h, p in zip(heads, p_grp)]
        alpha_rows = [jnp.broadcast_to(a, (HEAD_DIM, qb)) for a in alphas]
        oT_scr[...] = (jnp.concatenate(alpha_rows, axis=0) * oT_scr[...]
                       + jnp.concatenate([x[0:HEAD_DIM] for x in pv], axis=0))
        m_scr[...] = jnp.concatenate(m_new, axis=0)
        l_scr[...] = (jnp.concatenate(alphas, axis=0) * l_old
                      + jnp.concatenate([x[HEAD_DIM:HEAD_DIM + 1] for x in pv], axis=0))
        return rank[kc - 1:kc, :]

    lax.fori_loop(0, nk, att_body, jnp.zeros((1, qb), F32))
    for h in range(N_HEADS):
        hs = slice(h * HEAD_DIM, (h + 1) * HEAD_DIM)
        oT_scr[hs, :] = oT_scr[hs, :] / l_scr[h:h + 1, :]

    o_ref[...] = oT_scr[...].T.astype(BF16)


def _dsa(qT, qiT, kwT, k, vT, kidx, *, qb, kc):
    bsz, seq, _ = k.shape
    assert kc % qb == 0 and seq % kc == 0
    topk = min(TOPK_MAX, seq // 4)
    w_row_block = IDX_DIM // 8
    return pl.pallas_call(
        functools.partial(_dsa_kernel, kc=kc, topk=topk),
        grid=(bsz, seq // qb),
        in_specs=[
            pl.BlockSpec((None, ATT_W, qb), lambda b, i: (b, 0, i)),
            pl.BlockSpec((None, IDX_HEADS * IDX_DIM, qb), lambda b, i: (b, 0, i)),
            pl.BlockSpec((None, 8, qb), lambda b, i: (b, w_row_block, i)),
            pl.BlockSpec((None, seq, ATT_W), lambda b, i: (b, 0, 0)),
            pl.BlockSpec((None, seq // kc, ATT_W, kc), lambda b, i: (b, 0, 0, 0)),
            pl.BlockSpec((None, seq, LANES), lambda b, i: (b, 0, 0)),
        ],
        out_specs=pl.BlockSpec((None, qb, ATT_W), lambda b, i: (b, i, 0)),
        out_shape=jax.ShapeDtypeStruct((bsz, seq, ATT_W), BF16),
        scratch_shapes=[pltpu.VMEM((seq // kc, kc, qb), I32),
                        pltpu.VMEM((seq // kc, kc, qb), I16),
                        pltpu.VMEM((ATT_W, qb), F32),
                        pltpu.VMEM((N_HEADS, LANES, qb), BF16),
                        pltpu.VMEM((N_HEADS, qb), F32),
                        pltpu.VMEM((N_HEADS, qb), F32)],
        compiler_params=_cparams(("parallel", "parallel")),
        name="dsa_attention",
    )(qT, qiT, kwT, k, vT, kidx)


def _outproj_kernel(x_ref, yab_ref, yc_ref, wab_ref, wc_ref, o_ref):
    o_ref[...] = (x_ref[...]
                  + jnp.dot(yab_ref[...], wab_ref[...], preferred_element_type=F32)
                  + jnp.dot(yc_ref[...], wc_ref[...], preferred_element_type=F32))


def _outproj(x, yab, yc, wab, wc, *, tm):
    t, d = x.shape
    row = lambda c: pl.BlockSpec((tm, c), lambda i: (i, 0))
    full = lambda r, c: pl.BlockSpec((r, c), lambda i: (0, 0))
    return pl.pallas_call(
        _outproj_kernel,
        grid=(t // tm,),
        in_specs=[row(d), row(yab.shape[1]), row(yc.shape[1]), full(*wab.shape), full(*wc.shape)],
        out_specs=row(d),
        out_shape=jax.ShapeDtypeStruct((t, d), F32),
        compiler_params=_cparams(("parallel",)),
        name="out_proj",
    )(x, yab, yc, wab, wc)


def _memkv_kernel(mem_ref, g_ref, wkv_ref, kT_ref, v_ref):
    d = mem_ref.shape[1]
    mn = _rms(mem_ref[...], g_ref[...]).astype(BF16)
    kv = jnp.dot(mn, wkv_ref[...], preferred_element_type=F32)
    kT_ref[...] = kv[:, 0:d].T.astype(BF16)
    v_ref[...] = kv[:, d:2 * d].astype(BF16)


def _memkv(mem, g, wkv):
    bsz, m, d = mem.shape
    return pl.pallas_call(
        _memkv_kernel,
        grid=(bsz,),
        in_specs=[pl.BlockSpec((None, m, d), lambda b: (b, 0, 0)),
                  pl.BlockSpec((1, d), lambda b: (0, 0)),
                  pl.BlockSpec((d, 2 * d), lambda b: (0, 0))],
        out_specs=[pl.BlockSpec((None, d, m), lambda b: (b, 0, 0)),
                   pl.BlockSpec((None, m, d), lambda b: (b, 0, 0))],
        out_shape=[jax.ShapeDtypeStruct((bsz, d, m), BF16), jax.ShapeDtypeStruct((bsz, m, d), BF16)],
        compiler_params=_cparams(("parallel",)),
        name="mem_kv",
    )(mem, g, wkv)


def _xattn_kernel(x_ref, g_ref, wq_ref, kT_ref, v_ref, wo_ref, o_ref):
    d = x_ref.shape[1]
    hd = d // XA_HEADS
    x = x_ref[...]
    hq = _rms(x, g_ref[...]).astype(BF16)
    q = (jnp.dot(hq, wq_ref[...], preferred_element_type=F32) * (hd ** -0.5)).astype(BF16)
    outs = []
    for h in range(XA_HEADS):
        s = jnp.dot(q[:, h * hd:(h + 1) * hd], kT_ref[h * hd:(h + 1) * hd, :], preferred_element_type=F32)
        m = jnp.max(s, axis=-1, keepdims=True)
        p = jnp.exp(s - m)
        l = jnp.sum(p, axis=-1, keepdims=True)
        o = jnp.dot(p.astype(BF16), v_ref[:, h * hd:(h + 1) * hd], preferred_element_type=F32) / l
        outs.append(o.astype(BF16))
    o_ref[...] = x + jnp.dot(jnp.concatenate(outs, axis=-1), wo_ref[...], preferred_element_type=F32)


def _xattn(x, g, wq, kT, v, wo, *, bsz, seq, tm):
    d = x.shape[1]
    m = v.shape[1]
    ns = seq // tm
    tok = pl.BlockSpec((tm, d), lambda b, s: (b * ns + s, 0))
    return pl.pallas_call(
        _xattn_kernel,
        grid=(bsz, ns),
        in_specs=[tok,
                  pl.BlockSpec((1, d), lambda b, s: (0, 0)),
                  pl.BlockSpec((d, d), lambda b, s: (0, 0)),
                  pl.BlockSpec((None, d, m), lambda b, s: (b, 0, 0)),
                  pl.BlockSpec((None, m, d), lambda b, s: (b, 0, 0)),
                  pl.BlockSpec((d, d), lambda b, s: (0, 0))],
        out_specs=tok,
        out_shape=jax.ShapeDtypeStruct((bsz * seq, d), F32),
        compiler_params=_cparams(("parallel", "parallel")),
        name="mem_xattn",
    )(x, g, wq, kT, v, wo)


def _tiles(seq, d_ff):
    assert seq % 512 == 0, "sequence length must be a multiple of 512"
    tf = d_ff // 2 if (d_ff // 2) % LANES == 0 else d_ff
    return dict(
        tm_ffn=512, tf=tf,
        tm_proj=512,
        ts_conv=256, rc_conv=64,
        qb=256, kc=512,
        tm_out=1024, tm_xa=512, tm_rope=1024,
    )


def kernel(x, mem, positions, ffn1_norm, ffn1_w_gate, ffn1_w_up, ffn1_w_down, mix_norm, w_in,
           conf_dw, conf_dw_b, conf_ln_g, conf_ln_b, sc_dw, w_out, xa_norm, mem_norm,
           xa_wq, xa_wkv, xa_wo, ffn2_norm, ffn2_w_gate, ffn2_w_up, ffn2_w_down, final_norm):
    bsz, seq, d = x.shape
    depth = w_in.shape[0]
    t = bsz * seq
    tl = _tiles(seq, ffn1_w_gate.shape[2])
    bf = lambda a: a.astype(BF16)
    row = lambda a: a.reshape(1, -1)

    cos, sin = _rope_tables(positions, tl["tm_rope"])
    xf = x.reshape(t, d)
    w_in_pad = jnp.pad(w_in, ((0, 0), (0, 0), (0, N_IN_PAD - w_in.shape[2])))

    for l in range(depth):
        xf = _ffn(xf, row(ffn1_norm[l]), bf(ffn1_w_gate[l]), bf(ffn1_w_up[l]), bf(ffn1_w_down[l]),
                  row(final_norm), final=False, tm=tl["tm_ffn"], tf=tl["tf"])

        conv_in, qT, k, vT, qiT, kidx, kwT = _proj(
            xf, row(mix_norm[l]), bf(w_in_pad[l]), cos, sin, bsz=bsz, seq=seq, tm=tl["tm_proj"], kc=tl["kc"])
        yab = _conv_mixers(conv_in, conf_dw[l], row(conf_dw_b[l]), row(conf_ln_g[l]), row(conf_ln_b[l]),
                           sc_dw[l], ts=tl["ts_conv"], rc=tl["rc_conv"])
        yc = _dsa(qT, qiT, kwT, k, vT, kidx, qb=tl["qb"], kc=tl["kc"])
        n_ab = CONV_CH + SC_CH
        xf = _outproj(xf, yab.reshape(t, n_ab), yc.reshape(t, ATT_W),
                      bf(w_out[l, 0:n_ab]), bf(w_out[l, n_ab:]), tm=tl["tm_out"])

        kT_mem, v_mem = _memkv(mem, row(mem_norm[l]), bf(xa_wkv[l]))
        xf = _xattn(xf, row(xa_norm[l]), bf(xa_wq[l]), kT_mem, v_mem, bf(xa_wo[l]),
                    bsz=bsz, seq=seq, tm=tl["tm_xa"])

        xf = _ffn(xf, row(ffn2_norm[l]), bf(ffn2_w_gate[l]), bf(ffn2_w_up[l]), bf(ffn2_w_down[l]),
                  row(final_norm), final=(l == depth - 1), tm=tl["tm_ffn"], tf=tl["tf"])

    return xf.reshape(bsz, seq, d)
```

```python
import functools
import math

import jax
import jax.numpy as jnp
from jax import lax
from jax.experimental import pallas as pl
from jax.experimental.pallas import tpu as pltpu

F32 = jnp.float32
BF16 = jnp.bfloat16
I32 = jnp.int32
I16 = jnp.int16

CONV_CH = 256
CONV_W = 31
SC_CH = 256
SC_W = 3
N_HEADS = 8
HEAD_DIM = 64
ATT_W = N_HEADS * HEAD_DIM
IDX_HEADS = 4
IDX_DIM = 64
TOPK_MAX = 256
ROPE_THETA = 10000.0
XA_HEADS = 4
IDX_SCALE = (IDX_HEADS ** -0.5) * (IDX_DIM ** -0.5)
ATT_SCALE = HEAD_DIM ** -0.5
QK_SCALE = ATT_SCALE * math.log2(math.e)

LANES = 128
SUBLANES = 8
VMEM_LIMIT = 58 * 1024 * 1024

CONV_COLS = 2 * CONV_CH + 3 * SC_CH
Q_OFF = CONV_COLS
K_OFF = Q_OFF + ATT_W
V_OFF = K_OFF + ATT_W
QI_OFF = V_OFF + ATT_W
KW_OFF = QI_OFF + IDX_HEADS * IDX_DIM
N_IN_PAD = KW_OFF + LANES

HALF = 2 ** 15
INT_MAX = 2 ** 31 - 1
NEG_INF_KEY = -2139095041
NEG_BIG = -1e30


def _cparams(sem, vmem=VMEM_LIMIT):
    return pltpu.CompilerParams(dimension_semantics=sem, vmem_limit_bytes=vmem)


def _rms(x, g, eps=1e-6):
    return x * lax.rsqrt(jnp.mean(x * x, axis=-1, keepdims=True) + eps) * g


def _sigmoid(x):
    return 1.0 / (1.0 + jnp.exp(-x))


def _rope_kernel(pos_ref, invf_ref, sgn_ref, cos_ref, sin_ref):
    ang = pos_ref[...].astype(F32) * invf_ref[...]
    cos_ref[...] = jnp.cos(ang)
    sin_ref[...] = jnp.sin(ang) * sgn_ref[...]


def _rope_tables(positions, tm):
    t = positions.size
    half = HEAD_DIM // 2
    inv_freq = ROPE_THETA ** (-jnp.arange(0, HEAD_DIM, 2, dtype=F32) / HEAD_DIM)
    invf = jnp.tile(inv_freq, LANES // half)[None, :]
    sgn = jnp.tile(jnp.concatenate([-jnp.ones((half,), F32), jnp.ones((half,), F32)]),
                   LANES // HEAD_DIM)[None, :]
    pos = jnp.broadcast_to(positions.reshape(t, 1), (t, LANES))
    row = pl.BlockSpec((tm, LANES), lambda i: (i, 0))
    one = pl.BlockSpec((1, LANES), lambda i: (0, 0))
    return pl.pallas_call(
        _rope_kernel,
        grid=(t // tm,),
        in_specs=[row, one, one],
        out_specs=[row, row],
        out_shape=[jax.ShapeDtypeStruct((t, LANES), F32)] * 2,
        compiler_params=_cparams(("parallel",)),
        name="rope_tables",
    )(pos, invf, sgn)


def _ffn_kernel(x_ref, g_ref, wg_ref, wu_ref, wd_ref, fg_ref, o_ref, h_scr, acc_scr, *, final):
    j = pl.program_id(1)

    @pl.when(j == 0)
    def _():
        h_scr[...] = _rms(x_ref[...], g_ref[...]).astype(BF16)
        acc_scr[...] = jnp.zeros_like(acc_scr)

    h = h_scr[...]
    a = jnp.dot(h, wg_ref[...], preferred_element_type=F32)
    b = jnp.dot(h, wu_ref[...], preferred_element_type=F32)
    t = (a * _sigmoid(a)) * b
    acc_scr[...] += jnp.dot(t.astype(BF16), wd_ref[...], preferred_element_type=F32)

    @pl.when(j == pl.num_programs(1) - 1)
    def _():
        y = x_ref[...] + 0.5 * acc_scr[...]
        if final:
            y = _rms(y, fg_ref[...])
        o_ref[...] = y


def _ffn(x, g, wg, wu, wd, fg, *, final, tm, tf):
    t, d = x.shape
    f = wg.shape[1]
    return pl.pallas_call(
        functools.partial(_ffn_kernel, final=final),
        grid=(t // tm, f // tf),
        in_specs=[
            pl.BlockSpec((tm, d), lambda i, j: (i, 0)),
            pl.BlockSpec((1, d), lambda i, j: (0, 0)),
            pl.BlockSpec((d, tf), lambda i, j: (0, j)),
            pl.BlockSpec((d, tf), lambda i, j: (0, j)),
            pl.BlockSpec((tf, d), lambda i, j: (j, 0)),
            pl.BlockSpec((1, d), lambda i, j: (0, 0)),
        ],
        out_specs=pl.BlockSpec((tm, d), lambda i, j: (i, 0)),
        out_shape=jax.ShapeDtypeStruct((t, d), F32),
        scratch_shapes=[pltpu.VMEM((tm, d), BF16), pltpu.VMEM((tm, d), F32)],
        compiler_params=_cparams(("parallel", "arbitrary")),
        name="ffn_final" if final else "ffn",
    )(x, g, wg, wu, wd, fg)


CONV_HALO = 32
SC_HALO = 8


def _conv_mixers(cv, dw_ref, dwb_ref, lng_ref, lnb_ref, sw_ref, yab_ref, ha_scr, ub_scr, sh_scr, *, rc):
    ts = cv.shape[0]
    sub = SUBLANES
    first_tap = CONV_HALO - (CONV_W - 1)

    @pl.when(pl.program_id(1) == 0)
    def _():
        ha_scr[0:CONV_HALO, :] = jnp.zeros((CONV_HALO, CONV_CH), F32)
        ub_scr[0:SC_HALO, :] = jnp.zeros((SC_HALO, SC_CH), F32)

    b_off = 2 * CONV_CH
    ha_scr[CONV_HALO:CONV_HALO + ts, :] = cv[:, 0:CONV_CH] * _sigmoid(cv[:, CONV_CH:2 * CONV_CH])
    ub_scr[SC_HALO:SC_HALO + ts, :] = cv[:, b_off + SC_CH:b_off + 2 * SC_CH] * cv[:, b_off + 2 * SC_CH:b_off + 3 * SC_CH]
    span = sh_scr.shape[1]
    for ph in range(1, sub):
        sh_scr[ph - 1] = ha_scr[ph:ph + span, :]

    for r in range(ts // rc):
        base = r * rc
        acc = jnp.zeros((rc, CONV_CH), F32)
        for j in range(CONV_W):
            off = base + first_tap + j
            ph = off % sub
            src = ha_scr[off:off + rc, :] if ph == 0 else sh_scr[ph - 1, off - ph:off - ph + rc, :]
            acc = acc + src * dw_ref[j:j + 1, :]
        acc = acc + dwb_ref[...]
        mu = jnp.mean(acc, axis=-1, keepdims=True)
        cen = acc - mu
        var = jnp.mean(cen * cen, axis=-1, keepdims=True)
        y = cen * lax.rsqrt(var + 1e-5) * lng_ref[...] + lnb_ref[...]
        yab_ref[base:base + rc, 0:CONV_CH] = (y * _sigmoid(y)).astype(BF16)

        accb = jnp.zeros((rc, SC_CH), F32)
        for j in range(SC_W):
            off = base + SC_HALO - (SC_W - 1) + j
            accb = accb + ub_scr[off:off + rc, :] * sw_ref[j:j + 1, :]
        yab_ref[base:base + rc, CONV_CH:CONV_CH + SC_CH] = (cv[base:base + rc, b_off:b_off + SC_CH] * accb).astype(BF16)

    ha_scr[0:CONV_HALO, :] = ha_scr[ts:ts + CONV_HALO, :]
    ub_scr[0:SC_HALO, :] = ub_scr[ts:ts + SC_HALO, :]


def _proj_kernel(x_ref, g_ref, w_ref, cos_ref, sin_ref, dw_ref, dwb_ref, lng_ref, lnb_ref, sw_ref,
                 yab_ref, qT_ref, k_ref, vT_ref, qiT_ref, kidx_ref, kwT_ref,
                 ha_scr, ub_scr, sh_scr, *, kc, rc):
    tm = x_ref.shape[0]
    h = _rms(x_ref[...], g_ref[...]).astype(BF16)
    cos = cos_ref[...]
    sin = sin_ref[...]
    lane = lax.broadcasted_iota(I32, (tm, LANES), 1)
    first_half = (lane % HEAD_DIM) < (HEAD_DIM // 2)

    def rope(t, c, s):
        rot = jnp.where(first_half, pltpu.roll(t, LANES - HEAD_DIM // 2, 1), pltpu.roll(t, HEAD_DIM // 2, 1))
        return t * c + rot * s

    q = jnp.dot(h, w_ref[:, Q_OFF:Q_OFF + ATT_W], preferred_element_type=F32)
    for g in range(ATT_W // LANES):
        qr = rope(q[:, g * LANES:(g + 1) * LANES], cos, sin) * QK_SCALE
        qT_ref[g * LANES:(g + 1) * LANES, :] = qr.T.astype(BF16)

    k = jnp.dot(h, w_ref[:, K_OFF:K_OFF + ATT_W], preferred_element_type=F32)
    for g in range(ATT_W // LANES):
        k_ref[:, g * LANES:(g + 1) * LANES] = rope(k[:, g * LANES:(g + 1) * LANES], cos, sin).astype(BF16)

    v = jnp.dot(h, w_ref[:, V_OFF:V_OFF + ATT_W], preferred_element_type=F32)
    for c in range(tm // kc):
        vT_ref[c] = v[c * kc:(c + 1) * kc, :].T.astype(BF16)

    qi = jnp.dot(h, w_ref[:, QI_OFF:QI_OFF + IDX_HEADS * IDX_DIM], preferred_element_type=F32)
    for g in range(IDX_HEADS * IDX_DIM // LANES):
        qiT_ref[g * LANES:(g + 1) * LANES, :] = rope(qi[:, g * LANES:(g + 1) * LANES], cos, sin).T.astype(BF16)

    kw = jnp.dot(h, w_ref[:, KW_OFF:KW_OFF + LANES], preferred_element_type=F32)
    is_kidx = lane < IDX_DIM
    kw = rope(kw, jnp.where(is_kidx, cos, 1.0), jnp.where(is_kidx, sin, 0.0))
    kidx_ref[...] = kw.astype(BF16)
    kwT_ref[...] = kw.T

    cv = jnp.dot(h, w_ref[:, 0:CONV_COLS], preferred_element_type=F32)
    _conv_mixers(cv, dw_ref, dwb_ref, lng_ref, lnb_ref, sw_ref, yab_ref, ha_scr, ub_scr, sh_scr, rc=rc)


def _proj(x, g, w, cos, sin, dw, dwb, lng, lnb, sw, *, bsz, seq, tm, kc, rc):
    d = x.shape[1]
    ns = seq // tm
    tok = lambda b, s: (b * ns + s, 0)
    small = lambda r, c: pl.BlockSpec((r, c), lambda b, s: (0, 0))
    return pl.pallas_call(
        functools.partial(_proj_kernel, kc=kc, rc=rc),
        grid=(bsz, ns),
        in_specs=[
            pl.BlockSpec((tm, d), tok),
            small(1, d),
            small(d, N_IN_PAD),
            pl.BlockSpec((tm, LANES), tok),
            pl.BlockSpec((tm, LANES), tok),
            small(CONV_W, CONV_CH), small(1, CONV_CH), small(1, CONV_CH), small(1, CONV_CH),
            small(SC_W, SC_CH),
        ],
        out_specs=[
            pl.BlockSpec((None, tm, CONV_CH + SC_CH), lambda b, s: (b, s, 0)),
            pl.BlockSpec((None, ATT_W, tm), lambda b, s: (b, 0, s)),
            pl.BlockSpec((None, tm, ATT_W), lambda b, s: (b, s, 0)),
            pl.BlockSpec((None, tm // kc, ATT_W, kc), lambda b, s: (b, s, 0, 0)),
            pl.BlockSpec((None, IDX_HEADS * IDX_DIM, tm), lambda b, s: (b, 0, s)),
            pl.BlockSpec((None, tm, LANES), lambda b, s: (b, s, 0)),
            pl.BlockSpec((None, LANES, tm), lambda b, s: (b, 0, s)),
        ],
        out_shape=[
            jax.ShapeDtypeStruct((bsz, seq, CONV_CH + SC_CH), BF16),
            jax.ShapeDtypeStruct((bsz, ATT_W, seq), BF16),
            jax.ShapeDtypeStruct((bsz, seq, ATT_W), BF16),
            jax.ShapeDtypeStruct((bsz, seq // kc, ATT_W, kc), BF16),
            jax.ShapeDtypeStruct((bsz, IDX_HEADS * IDX_DIM, seq), BF16),
            jax.ShapeDtypeStruct((bsz, seq, LANES), BF16),
            jax.ShapeDtypeStruct((bsz, LANES, seq), F32),
        ],
        scratch_shapes=[pltpu.VMEM((tm + CONV_HALO, CONV_CH), F32),
                        pltpu.VMEM((tm + SC_HALO, SC_CH), F32),
                        pltpu.VMEM((SUBLANES - 1, tm + CONV_HALO - SUBLANES, CONV_CH), F32)],
        compiler_params=_cparams(("arbitrary", "arbitrary")),
        name="in_proj",
    )(x, g, w, cos, sin, dw, dwb, lng, lnb, sw)


def _sortable(x):
    b = lax.bitcast_convert_type(x, I32)
    return b ^ ((b >> 31) & INT_MAX)


def _dsa_kernel(qT_ref, qiT_ref, wT_ref, k_ref, vT_ref, kidx_ref, o_ref,
                key_scr, k16_scr, oT_scr, qpad_scr, m_scr, l_scr, *, kc, topk):
    qb = qT_ref.shape[1]
    i = pl.program_id(1)
    nk = (i * qb) // kc + 1
    q_pos = i * qb + lax.broadcasted_iota(I32, (1, qb), 1)
    row_iota = lax.broadcasted_iota(I32, (kc, qb), 0)

    w = wT_ref[...] * IDX_SCALE
    zeros_half = jnp.zeros((LANES - IDX_DIM, qb), BF16)
    qi = [jnp.concatenate([qiT_ref[h * IDX_DIM:(h + 1) * IDX_DIM, :], zeros_half], axis=0)
          for h in range(IDX_HEADS)]

    def score_chunk(c, diagonal):
        kcs = kidx_ref[pl.ds(pl.multiple_of(c * kc, kc), kc), :]
        sc = jnp.zeros((kc, qb), F32)
        for h in range(IDX_HEADS):
            lg = jnp.dot(kcs, qi[h], preferred_element_type=F32)
            sc = sc + w[h:h + 1, :] * jnp.maximum(lg, 0.0)
        if diagonal:
            sc = jnp.where((c * kc + row_iota) <= q_pos, sc, -jnp.inf)
        key = _sortable(sc)
        key_scr[c] = key
        k16_scr[c] = (key >> 16).astype(I16)

    def score_body(c, carry):
        score_chunk(c, False)
        return carry

    lax.fori_loop(0, nk - 1, score_body, 0)
    score_chunk(nk - 1, True)

    fold = 128

    def count16(mid):
        midb = jnp.broadcast_to(mid, (fold, qb)).astype(I16)

        def body(c, acc):
            for r in range(kc // fold):
                acc = acc + jnp.where(k16_scr[c, r * fold:(r + 1) * fold, :] >= midb, jnp.int16(1), jnp.int16(0))
            return acc

        acc = lax.fori_loop(0, nk, body, jnp.zeros((fold, qb), I16))
        return jnp.sum(acc.astype(I32), axis=0, keepdims=True)

    def bisect16(want, cnt_all):
        def step(_, carry):
            lo, hi, cnt_lo, cnt_hi = carry
            mid = (lo + hi) >> 1
            cnt = jnp.where(mid == -HALF, cnt_all, count16(mid))
            ok = cnt >= want
            return (jnp.where(ok, mid, lo), jnp.where(ok, hi, mid),
                    jnp.where(ok, cnt, cnt_lo), jnp.where(ok, cnt_hi, cnt))
        zero = jnp.zeros((1, qb), I32)
        t, _, cnt_t, cnt_above = lax.fori_loop(0, 16, step, (zero - HALF, zero + HALF, cnt_all, zero))
        return t, cnt_t, cnt_above

    n_stored = nk * kc + jnp.zeros((1, qb), I32)
    t_hi, cge_hi, cgt_hi = bisect16(topk, n_stored)

    def low_body(c, carry):
        kk = key_scr[c]
        low = (kk & (2 * HALF - 1)) - HALF
        k16_scr[c] = jnp.where((kk >> 16) == t_hi, low, -HALF).astype(I16)
        return carry

    lax.fori_loop(0, nk, low_body, 0)
    t_lo, cge_lo, cgt_lo = bisect16(topk - cgt_hi, cge_hi - cgt_hi)

    tau = t_hi * (2 * HALF) + (t_lo + HALF)
    full = tau > NEG_INF_KEY
    tau_eff = jnp.maximum(tau, NEG_INF_KEY)
    need = jnp.where(full, topk - (cgt_hi + cgt_lo), 0).astype(F32)
    ltri = jnp.where(lax.broadcasted_iota(I32, (kc, kc), 0) >= lax.broadcasted_iota(I32, (kc, kc), 1),
                     1.0, 0.0).astype(BF16)

    zeros_head = jnp.zeros((HEAD_DIM, qb), BF16)
    for h in range(N_HEADS):
        qh = qT_ref[h * HEAD_DIM:(h + 1) * HEAD_DIM, :]
        qpad_scr[h] = jnp.concatenate([qh, zeros_head] if h % 2 == 0 else [zeros_head, qh], axis=0)
    m_scr[...] = jnp.full((N_HEADS, qb), NEG_BIG, F32)
    l_scr[...] = jnp.zeros((N_HEADS, qb), F32)
    oT_scr[...] = jnp.zeros((ATT_W, qb), F32)

    ones_rows = jnp.ones((16, kc), BF16)

    def att_body(c, run):
        kk = key_scr[c]
        eq = kk == tau_eff
        rank = jnp.dot(ltri, jnp.where(eq, 1.0, 0.0).astype(BF16), preferred_element_type=F32)
        tie = jnp.where(rank <= need - run, 0.0, -jnp.inf)
        bias = jnp.where(kk > tau_eff, 0.0, jnp.where(eq, tie, -jnp.inf))
        row0 = pl.multiple_of(c * kc, kc)
        m_old = m_scr[...]
        l_old = l_scr[...]
        s_all = [jnp.dot(k_ref[pl.ds(row0, kc), (h // 2) * LANES:(h // 2 + 1) * LANES], qpad_scr[h],
                         preferred_element_type=F32) + bias for h in range(N_HEADS)]
        m_new, alphas, p_all = [], [], []
        for h in range(N_HEADS):
            m = m_old[h:h + 1, :]
            mn = jnp.maximum(m, jnp.max(s_all[h], axis=0, keepdims=True))
            p_all.append(jnp.exp2(s_all[h] - mn).astype(BF16))
            alphas.append(jnp.exp2(m - mn))
            m_new.append(mn)
        pv = [jnp.dot(jnp.concatenate([vT_ref[c, h * HEAD_DIM:(h + 1) * HEAD_DIM, :], ones_rows], axis=0),
                      p_all[h], preferred_element_type=F32) for h in range(N_HEADS)]
        alpha_rows = [jnp.broadcast_to(a, (HEAD_DIM, qb)) for a in alphas]
        oT_scr[...] = (jnp.concatenate(alpha_rows, axis=0) * oT_scr[...]
                       + jnp.concatenate([x[0:HEAD_DIM] for x in pv], axis=0))
        m_scr[...] = jnp.concatenate(m_new, axis=0)
        l_scr[...] = (jnp.concatenate(alphas, axis=0) * l_old
                      + jnp.concatenate([x[HEAD_DIM:HEAD_DIM + 1] for x in pv], axis=0))
        return run + rank[kc - 1:kc, :]

    lax.fori_loop(0, nk, att_body, jnp.zeros((1, qb), F32))
    for h in range(N_HEADS):
        hs = slice(h * HEAD_DIM, (h + 1) * HEAD_DIM)
        oT_scr[hs, :] = oT_scr[hs, :] / l_scr[h:h + 1, :]

    o_ref[...] = oT_scr[...].T.astype(BF16)


def _dsa(qT, qiT, kwT, k, vT, kidx, *, qb, kc):
    bsz, seq, _ = k.shape
    assert kc % qb == 0 and seq % kc == 0
    topk = min(TOPK_MAX, seq // 4)
    w_row_block = IDX_DIM // 8
    return pl.pallas_call(
        functools.partial(_dsa_kernel, kc=kc, topk=topk),
        grid=(bsz, seq // qb),
        in_specs=[
            pl.BlockSpec((None, ATT_W, qb), lambda b, i: (b, 0, i)),
            pl.BlockSpec((None, IDX_HEADS * IDX_DIM, qb), lambda b, i: (b, 0, i)),
            pl.BlockSpec((None, 8, qb), lambda b, i: (b, w_row_block, i)),
            pl.BlockSpec((None, seq, ATT_W), lambda b, i: (b, 0, 0)),
            pl.BlockSpec((None, seq // kc, ATT_W, kc), lambda b, i: (b, 0, 0, 0)),
            pl.BlockSpec((None, seq, LANES), lambda b, i: (b, 0, 0)),
        ],
        out_specs=pl.BlockSpec((None, qb, ATT_W), lambda b, i: (b, i, 0)),
        out_shape=jax.ShapeDtypeStruct((bsz, seq, ATT_W), BF16),
        scratch_shapes=[pltpu.VMEM((seq // kc, kc, qb), I32),
                        pltpu.VMEM((seq // kc, kc, qb), I16),
                        pltpu.VMEM((ATT_W, qb), F32),
                        pltpu.VMEM((N_HEADS, LANES, qb), BF16),
                        pltpu.VMEM((N_HEADS, qb), F32),
                        pltpu.VMEM((N_HEADS, qb), F32)],
        compiler_params=_cparams(("parallel", "parallel")),
        name="dsa_attention",
    )(qT, qiT, kwT, k, vT, kidx)


def _memkv_kernel(mem_ref, g_ref, wkv_ref, kT_ref, v_ref):
    d = mem_ref.shape[1]
    mn = _rms(mem_ref[...], g_ref[...]).astype(BF16)
    kv = jnp.dot(mn, wkv_ref[...], preferred_element_type=F32)
    kT_ref[...] = kv[:, 0:d].T.astype(BF16)
    v_ref[...] = kv[:, d:2 * d].astype(BF16)


def _memkv(mem, g, wkv):
    bsz, m, d = mem.shape
    return pl.pallas_call(
        _memkv_kernel,
        grid=(bsz,),
        in_specs=[pl.BlockSpec((None, m, d), lambda b: (b, 0, 0)),
                  pl.BlockSpec((1, d), lambda b: (0, 0)),
                  pl.BlockSpec((d, 2 * d), lambda b: (0, 0))],
        out_specs=[pl.BlockSpec((None, d, m), lambda b: (b, 0, 0)),
                   pl.BlockSpec((None, m, d), lambda b: (b, 0, 0))],
        out_shape=[jax.ShapeDtypeStruct((bsz, d, m), BF16), jax.ShapeDtypeStruct((bsz, m, d), BF16)],
        compiler_params=_cparams(("parallel",)),
        name="mem_kv",
    )(mem, g, wkv)


def _xattn_kernel(x_ref, yab_ref, yc_ref, wab_ref, wc_ref, g_ref, wq_ref, kT_ref, v_ref, wo_ref, o_ref):
    d = x_ref.shape[1]
    hd = d // XA_HEADS
    x = (x_ref[...]
         + jnp.dot(yab_ref[...], wab_ref[...], preferred_element_type=F32)
         + jnp.dot(yc_ref[...], wc_ref[...], preferred_element_type=F32))
    hq = _rms(x, g_ref[...]).astype(BF16)
    q = (jnp.dot(hq, wq_ref[...], preferred_element_type=F32) * (hd ** -0.5)).astype(BF16)
    outs = []
    for h in range(XA_HEADS):
        s = jnp.dot(q[:, h * hd:(h + 1) * hd], kT_ref[h * hd:(h + 1) * hd, :], preferred_element_type=F32)
        m = jnp.max(s, axis=-1, keepdims=True)
        p = jnp.exp(s - m)
        l = jnp.sum(p, axis=-1, keepdims=True)
        o = jnp.dot(p.astype(BF16), v_ref[:, h * hd:(h + 1) * hd], preferred_element_type=F32) / l
        outs.append(o.astype(BF16))
    o_ref[...] = x + jnp.dot(jnp.concatenate(outs, axis=-1), wo_ref[...], preferred_element_type=F32)


def _xattn(x, yab, yc, wab, wc, g, wq, kT, v, wo, *, bsz, seq, tm):
    d = x.shape[1]
    m = v.shape[1]
    ns = seq // tm
    tok_c = lambda c: pl.BlockSpec((tm, c), lambda b, s: (b * ns + s, 0))
    tok = tok_c(d)
    full = lambda a: pl.BlockSpec(a.shape, lambda b, s: (0, 0))
    return pl.pallas_call(
        _xattn_kernel,
        grid=(bsz, ns),
        in_specs=[tok, tok_c(yab.shape[1]), tok_c(yc.shape[1]), full(wab), full(wc),
                  pl.BlockSpec((1, d), lambda b, s: (0, 0)),
                  pl.BlockSpec((d, d), lambda b, s: (0, 0)),
                  pl.BlockSpec((None, d, m), lambda b, s: (b, 0, 0)),
                  pl.BlockSpec((None, m, d), lambda b, s: (b, 0, 0)),
                  pl.BlockSpec((d, d), lambda b, s: (0, 0))],
        out_specs=tok,
        out_shape=jax.ShapeDtypeStruct((bsz * seq, d), F32),
        compiler_params=_cparams(("parallel", "parallel")),
        name="mem_xattn",
    )(x, yab, yc, wab, wc, g, wq, kT, v, wo)


def _tiles(seq, d_ff):
    assert seq % 512 == 0, "sequence length must be a multiple of 512"
    tf = d_ff // 2 if (d_ff // 2) % LANES == 0 else d_ff
    return dict(
        tm_ffn=1024, tf=tf,
        tm_proj=512, rc_conv=64,
        qb=256, kc=512,
        tm_xa=512, tm_rope=1024,
    )


def kernel(x, mem, positions, ffn1_norm, ffn1_w_gate, ffn1_w_up, ffn1_w_down, mix_norm, w_in,
           conf_dw, conf_dw_b, conf_ln_g, conf_ln_b, sc_dw, w_out, xa_norm, mem_norm,
           xa_wq, xa_wkv, xa_wo, ffn2_norm, ffn2_w_gate, ffn2_w_up, ffn2_w_down, final_norm):
    bsz, seq, d = x.shape
    depth = w_in.shape[0]
    t = bsz * seq
    tl = _tiles(seq, ffn1_w_gate.shape[2])
    bf = lambda a: a.astype(BF16)
    row = lambda a: a.reshape(1, -1)

    cos, sin = _rope_tables(positions, tl["tm_rope"])
    xf = x.reshape(t, d)
    w_in_pad = jnp.pad(w_in, ((0, 0), (0, 0), (0, N_IN_PAD - w_in.shape[2])))

    for l in range(depth):
        xf = _ffn(xf, row(ffn1_norm[l]), bf(ffn1_w_gate[l]), bf(ffn1_w_up[l]), bf(ffn1_w_down[l]),
                  row(final_norm), final=False, tm=tl["tm_ffn"], tf=tl["tf"])

        yab, qT, k, vT, qiT, kidx, kwT = _proj(
            xf, row(mix_norm[l]), bf(w_in_pad[l]), cos, sin,
            conf_dw[l], row(conf_dw_b[l]), row(conf_ln_g[l]), row(conf_ln_b[l]), sc_dw[l],
            bsz=bsz, seq=seq, tm=tl["tm_proj"], kc=tl["kc"], rc=tl["rc_conv"])
        yc = _dsa(qT, qiT, kwT, k, vT, kidx, qb=tl["qb"], kc=tl["kc"])

        n_ab = CONV_CH + SC_CH
        kT_mem, v_mem = _memkv(mem, row(mem_norm[l]), bf(xa_wkv[l]))
        xf = _xattn(xf, yab.reshape(t, n_ab), yc.reshape(t, ATT_W), bf(w_out[l, 0:n_ab]), bf(w_out[l, n_ab:]),
                    row(xa_norm[l]), bf(xa_wq[l]), kT_mem, v_mem, bf(xa_wo[l]),
                    bsz=bsz, seq=seq, tm=tl["tm_xa"])

        xf = _ffn(xf, row(ffn2_norm[l]), bf(ffn2_w_gate[l]), bf(ffn2_w_up[l]), bf(ffn2_w_down[l]),
                  row(final_norm), final=(l == depth - 1), tm=tl["tm_ffn"], tf=tl["tf"])

    return xf.reshape(bsz, seq, d)
```

```python
import functools
import math

import jax
import jax.numpy as jnp
from jax import lax
from jax.experimental import pallas as pl
from jax.experimental.pallas import tpu as pltpu

F32 = jnp.float32
BF16 = jnp.bfloat16
I32 = jnp.int32
I16 = jnp.int16

CONV_CH = 256
CONV_W = 31
SC_CH = 256
SC_W = 3
N_HEADS = 8
HEAD_DIM = 64
ATT_W = N_HEADS * HEAD_DIM
IDX_HEADS = 4
IDX_DIM = 64
TOPK_MAX = 256
ROPE_THETA = 10000.0
XA_HEADS = 4
IDX_SCALE = (IDX_HEADS ** -0.5) * (IDX_DIM ** -0.5)
ATT_SCALE = HEAD_DIM ** -0.5
QK_SCALE = ATT_SCALE * math.log2(math.e)

LANES = 128
SUBLANES = 8
VMEM_LIMIT = 58 * 1024 * 1024

CONV_COLS = 2 * CONV_CH + 3 * SC_CH
Q_OFF = CONV_COLS
K_OFF = Q_OFF + ATT_W
V_OFF = K_OFF + ATT_W
QI_OFF = V_OFF + ATT_W
KW_OFF = QI_OFF + IDX_HEADS * IDX_DIM
N_IN_PAD = KW_OFF + LANES

HALF = 2 ** 15
INT_MAX = 2 ** 31 - 1
NEG_INF_KEY = -2139095041
NEG_BIG = -1e30


def _cparams(sem, vmem=VMEM_LIMIT):
    return pltpu.CompilerParams(dimension_semantics=sem, vmem_limit_bytes=vmem)


def _rms(x, g, eps=1e-6):
    return x * lax.rsqrt(jnp.mean(x * x, axis=-1, keepdims=True) + eps) * g


def _sigmoid(x):
    return 1.0 / (1.0 + jnp.exp(-x))


def _rope_kernel(pos_ref, invf_ref, sgn_ref, cos_ref, sin_ref):
    ang = pos_ref[...].astype(F32) * invf_ref[...]
    cos_ref[...] = jnp.cos(ang)
    sin_ref[...] = jnp.sin(ang) * sgn_ref[...]


def _rope_tables(positions, tm):
    t = positions.size
    half = HEAD_DIM // 2
    inv_freq = ROPE_THETA ** (-jnp.arange(0, HEAD_DIM, 2, dtype=F32) / HEAD_DIM)
    invf = jnp.tile(inv_freq, LANES // half)[None, :]
    sgn = jnp.tile(jnp.concatenate([-jnp.ones((half,), F32), jnp.ones((half,), F32)]),
                   LANES // HEAD_DIM)[None, :]
    pos = jnp.broadcast_to(positions.reshape(t, 1), (t, LANES))
    row = pl.BlockSpec((tm, LANES), lambda i: (i, 0))
    one = pl.BlockSpec((1, LANES), lambda i: (0, 0))
    return pl.pallas_call(
        _rope_kernel,
        grid=(t // tm,),
        in_specs=[row, one, one],
        out_specs=[row, row],
        out_shape=[jax.ShapeDtypeStruct((t, LANES), F32)] * 2,
        compiler_params=_cparams(("parallel",)),
        name="rope_tables",
    )(pos, invf, sgn)


def _ffn_kernel(x_ref, g_ref, wg_ref, wu_ref, wd_ref, fg_ref, o_ref, h_scr, acc_scr, *, final):
    j = pl.program_id(1)

    @pl.when(j == 0)
    def _():
        h_scr[...] = _rms(x_ref[...], g_ref[...]).astype(BF16)
        acc_scr[...] = jnp.zeros_like(acc_scr)

    h = h_scr[...]
    a = jnp.dot(h, wg_ref[...], preferred_element_type=F32)
    b = jnp.dot(h, wu_ref[...], preferred_element_type=F32)
    t = (a * _sigmoid(a)) * b
    acc_scr[...] += jnp.dot(t.astype(BF16), wd_ref[...], preferred_element_type=F32)

    @pl.when(j == pl.num_programs(1) - 1)
    def _():
        y = x_ref[...] + 0.5 * acc_scr[...]
        if final:
            y = _rms(y, fg_ref[...])
        o_ref[...] = y


def _ffn(x, g, wg, wu, wd, fg, *, final, tm, tf):
    t, d = x.shape
    f = wg.shape[1]
    return pl.pallas_call(
        functools.partial(_ffn_kernel, final=final),
        grid=(t // tm, f // tf),
        in_specs=[
            pl.BlockSpec((tm, d), lambda i, j: (i, 0)),
            pl.BlockSpec((1, d), lambda i, j: (0, 0)),
            pl.BlockSpec((d, tf), lambda i, j: (0, j)),
            pl.BlockSpec((d, tf), lambda i, j: (0, j)),
            pl.BlockSpec((tf, d), lambda i, j: (j, 0)),
            pl.BlockSpec((1, d), lambda i, j: (0, 0)),
        ],
        out_specs=pl.BlockSpec((tm, d), lambda i, j: (i, 0)),
        out_shape=jax.ShapeDtypeStruct((t, d), F32),
        scratch_shapes=[pltpu.VMEM((tm, d), BF16), pltpu.VMEM((tm, d), F32)],
        compiler_params=_cparams(("parallel", "arbitrary")),
        name="ffn_final" if final else "ffn",
    )(x, g, wg, wu, wd, fg)


CONV_HALO = 32
SC_HALO = 8


def _conv_mixers(cv, dw_ref, dwb_ref, lng_ref, lnb_ref, sw_ref, yab_ref, ha_scr, ub_scr, sh_scr, *, rc):
    ts = cv.shape[0]
    sub = SUBLANES
    first_tap = CONV_HALO - (CONV_W - 1)

    @pl.when(pl.program_id(1) == 0)
    def _():
        ha_scr[0:CONV_HALO, :] = jnp.zeros((CONV_HALO, CONV_CH), F32)
        ub_scr[0:SC_HALO, :] = jnp.zeros((SC_HALO, SC_CH), F32)

    b_off = 2 * CONV_CH
    ha_scr[CONV_HALO:CONV_HALO + ts, :] = cv[:, 0:CONV_CH] * _sigmoid(cv[:, CONV_CH:2 * CONV_CH])
    ub_scr[SC_HALO:SC_HALO + ts, :] = cv[:, b_off + SC_CH:b_off + 2 * SC_CH] * cv[:, b_off + 2 * SC_CH:b_off + 3 * SC_CH]
    span = sh_scr.shape[1]
    for ph in range(1, sub):
        sh_scr[ph - 1] = ha_scr[ph:ph + span, :]

    for r in range(ts // rc):
        base = r * rc
        acc = jnp.zeros((rc, CONV_CH), F32)
        for j in range(CONV_W):
            off = base + first_tap + j
            ph = off % sub
            src = ha_scr[off:off + rc, :] if ph == 0 else sh_scr[ph - 1, off - ph:off - ph + rc, :]
            acc = acc + src * dw_ref[j:j + 1, :]
        acc = acc + dwb_ref[...]
        mu = jnp.mean(acc, axis=-1, keepdims=True)
        cen = acc - mu
        var = jnp.mean(cen * cen, axis=-1, keepdims=True)
        y = cen * lax.rsqrt(var + 1e-5) * lng_ref[...] + lnb_ref[...]
        yab_ref[base:base + rc, 0:CONV_CH] = (y * _sigmoid(y)).astype(BF16)

        accb = jnp.zeros((rc, SC_CH), F32)
        for j in range(SC_W):
            off = base + SC_HALO - (SC_W - 1) + j
            accb = accb + ub_scr[off:off + rc, :] * sw_ref[j:j + 1, :]
        yab_ref[base:base + rc, CONV_CH:CONV_CH + SC_CH] = (cv[base:base + rc, b_off:b_off + SC_CH] * accb).astype(BF16)

    ha_scr[0:CONV_HALO, :] = ha_scr[ts:ts + CONV_HALO, :]
    ub_scr[0:SC_HALO, :] = ub_scr[ts:ts + SC_HALO, :]


def _proj_kernel(x_ref, g_ref, w_ref, cos_ref, sin_ref, dw_ref, dwb_ref, lng_ref, lnb_ref, sw_ref,
                 yab_ref, qT_ref, k_ref, vT_ref, qiT_ref, kidx_ref, kwT_ref,
                 ha_scr, ub_scr, sh_scr, *, kc, rc):
    tm = x_ref.shape[0]
    h = _rms(x_ref[...], g_ref[...]).astype(BF16)
    cos = cos_ref[...]
    sin = sin_ref[...]
    lane = lax.broadcasted_iota(I32, (tm, LANES), 1)
    first_half = (lane % HEAD_DIM) < (HEAD_DIM // 2)

    def rope(t, c, s):
        rot = jnp.where(first_half, pltpu.roll(t, LANES - HEAD_DIM // 2, 1), pltpu.roll(t, HEAD_DIM // 2, 1))
        return t * c + rot * s

    q = jnp.dot(h, w_ref[:, Q_OFF:Q_OFF + ATT_W], preferred_element_type=F32)
    for g in range(ATT_W // LANES):
        qr = rope(q[:, g * LANES:(g + 1) * LANES], cos, sin) * QK_SCALE
        qT_ref[g * LANES:(g + 1) * LANES, :] = qr.T.astype(BF16)

    k = jnp.dot(h, w_ref[:, K_OFF:K_OFF + ATT_W], preferred_element_type=F32)
    for g in range(ATT_W // LANES):
        k_ref[:, g * LANES:(g + 1) * LANES] = rope(k[:, g * LANES:(g + 1) * LANES], cos, sin).astype(BF16)

    v = jnp.dot(h, w_ref[:, V_OFF:V_OFF + ATT_W], preferred_element_type=F32)
    for c in range(tm // kc):
        vT_ref[c] = v[c * kc:(c + 1) * kc, :].T.astype(BF16)

    qi = jnp.dot(h, w_ref[:, QI_OFF:QI_OFF + IDX_HEADS * IDX_DIM], preferred_element_type=F32)
    for g in range(IDX_HEADS * IDX_DIM // LANES):
        qiT_ref[g * LANES:(g + 1) * LANES, :] = rope(qi[:, g * LANES:(g + 1) * LANES], cos, sin).T.astype(BF16)

    kw = jnp.dot(h, w_ref[:, KW_OFF:KW_OFF + LANES], preferred_element_type=F32)
    is_kidx = lane < IDX_DIM
    kw = rope(kw, jnp.where(is_kidx, cos, 1.0), jnp.where(is_kidx, sin, 0.0))
    kidx_ref[...] = kw.astype(BF16)
    kwT_ref[...] = kw.T

    cv = jnp.dot(h, w_ref[:, 0:CONV_COLS], preferred_element_type=F32)
    _conv_mixers(cv, dw_ref, dwb_ref, lng_ref, lnb_ref, sw_ref, yab_ref, ha_scr, ub_scr, sh_scr, rc=rc)


def _proj(x, g, w, cos, sin, dw, dwb, lng, lnb, sw, *, bsz, seq, tm, kc, rc):
    d = x.shape[1]
    ns = seq // tm
    tok = lambda b, s: (b * ns + s, 0)
    small = lambda r, c: pl.BlockSpec((r, c), lambda b, s: (0, 0))
    return pl.pallas_call(
        functools.partial(_proj_kernel, kc=kc, rc=rc),
        grid=(bsz, ns),
        in_specs=[
            pl.BlockSpec((tm, d), tok),
            small(1, d),
            small(d, N_IN_PAD),
            pl.BlockSpec((tm, LANES), tok),
            pl.BlockSpec((tm, LANES), tok),
            small(CONV_W, CONV_CH), small(1, CONV_CH), small(1, CONV_CH), small(1, CONV_CH),
            small(SC_W, SC_CH),
        ],
        out_specs=[
            pl.BlockSpec((None, tm, CONV_CH + SC_CH), lambda b, s: (b, s, 0)),
            pl.BlockSpec((None, ATT_W, tm), lambda b, s: (b, 0, s)),
            pl.BlockSpec((None, tm, ATT_W), lambda b, s: (b, s, 0)),
            pl.BlockSpec((None, tm // kc, ATT_W, kc), lambda b, s: (b, s, 0, 0)),
            pl.BlockSpec((None, IDX_HEADS * IDX_DIM, tm), lambda b, s: (b, 0, s)),
            pl.BlockSpec((None, tm, LANES), lambda b, s: (b, s, 0)),
            pl.BlockSpec((None, LANES, tm), lambda b, s: (b, 0, s)),
        ],
        out_shape=[
            jax.ShapeDtypeStruct((bsz, seq, CONV_CH + SC_CH), BF16),
            jax.ShapeDtypeStruct((bsz, ATT_W, seq), BF16),
            jax.ShapeDtypeStruct((bsz, seq, ATT_W), BF16),
            jax.ShapeDtypeStruct((bsz, seq // kc, ATT_W, kc), BF16),
            jax.ShapeDtypeStruct((bsz, IDX_HEADS * IDX_DIM, seq), BF16),
            jax.ShapeDtypeStruct((bsz, seq, LANES), BF16),
            jax.ShapeDtypeStruct((bsz, LANES, seq), F32),
        ],
        scratch_shapes=[pltpu.VMEM((tm + CONV_HALO, CONV_CH), F32),
                        pltpu.VMEM((tm + SC_HALO, SC_CH), F32),
                        pltpu.VMEM((SUBLANES - 1, tm + CONV_HALO - SUBLANES, CONV_CH), F32)],
        compiler_params=_cparams(("arbitrary", "arbitrary")),
        name="in_proj",
    )(x, g, w, cos, sin, dw, dwb, lng, lnb, sw)


def _sortable(x):
    b = lax.bitcast_convert_type(x, I32)
    return b ^ ((b >> 31) & INT_MAX)


def _dsa_kernel(qT_ref, qiT_ref, wT_ref, k_ref, vT_ref, kidx_ref, o_ref,
                key_scr, k16_scr, oT_scr, qpad_scr, m_scr, l_scr, *, kc, topk):
    qb = qT_ref.shape[1]
    i = pl.program_id(1)
    nk = (i * qb) // kc + 1
    q_pos = i * qb + lax.broadcasted_iota(I32, (1, qb), 1)
    row_iota = lax.broadcasted_iota(I32, (kc, qb), 0)

    w = wT_ref[...] * IDX_SCALE
    zeros_half = jnp.zeros((LANES - IDX_DIM, qb), BF16)
    qi = [jnp.concatenate([qiT_ref[h * IDX_DIM:(h + 1) * IDX_DIM, :], zeros_half], axis=0)
          for h in range(IDX_HEADS)]

    def score_chunk(c, diagonal):
        kcs = kidx_ref[pl.ds(pl.multiple_of(c * kc, kc), kc), :]
        sc = jnp.zeros((kc, qb), F32)
        for h in range(IDX_HEADS):
            lg = jnp.dot(kcs, qi[h], preferred_element_type=F32)
            sc = sc + w[h:h + 1, :] * jnp.maximum(lg, 0.0)
        if diagonal:
            sc = jnp.where((c * kc + row_iota) <= q_pos, sc, -jnp.inf)
        key = _sortable(sc)
        key_scr[c] = key
        k16_scr[c] = (key >> 16).astype(I16)

    def score_body(c, carry):
        score_chunk(c, False)
        return carry

    lax.fori_loop(0, nk - 1, score_body, 0)
    score_chunk(nk - 1, True)

    fold = 16 * 2 * SUBLANES * LANES // qb

    def count16(mid):
        midb = jnp.broadcast_to(mid, (fold, qb)).astype(I16)

        def body(c, acc):
            for r in range(kc // fold):
                acc = acc + jnp.where(k16_scr[c, r * fold:(r + 1) * fold, :] >= midb, jnp.int16(1), jnp.int16(0))
            return acc

        acc = lax.fori_loop(0, nk, body, jnp.zeros((fold, qb), I16))
        return jnp.sum(acc.astype(I32), axis=0, keepdims=True)

    def bisect16(want, cnt_all):
        def step(_, carry):
            lo, hi, cnt_lo, cnt_hi = carry
            mid = (lo + hi) >> 1
            cnt = jnp.where(mid == -HALF, cnt_all, count16(mid))
            ok = cnt >= want
            return (jnp.where(ok, mid, lo), jnp.where(ok, hi, mid),
                    jnp.where(ok, cnt, cnt_lo), jnp.where(ok, cnt_hi, cnt))
        zero = jnp.zeros((1, qb), I32)
        t, _, cnt_t, cnt_above = lax.fori_loop(0, 16, step, (zero - HALF, zero + HALF, cnt_all, zero))
        return t, cnt_t, cnt_above

    n_stored = nk * kc + jnp.zeros((1, qb), I32)
    t_hi, cge_hi, cgt_hi = bisect16(topk, n_stored)

    def low_body(c, carry):
        kk = key_scr[c]
        low = (kk & (2 * HALF - 1)) - HALF
        k16_scr[c] = jnp.where((kk >> 16) == t_hi, low, -HALF).astype(I16)
        return carry

    lax.fori_loop(0, nk, low_body, 0)
    t_lo, cge_lo, cgt_lo = bisect16(topk - cgt_hi, cge_hi - cgt_hi)

    tau = t_hi * (2 * HALF) + (t_lo + HALF)
    full = tau > NEG_INF_KEY
    tau_eff = jnp.maximum(tau, NEG_INF_KEY)
    need = jnp.where(full, topk - (cgt_hi + cgt_lo), 0).astype(F32)
    ltri = jnp.where(lax.broadcasted_iota(I32, (kc, kc), 0) >= lax.broadcasted_iota(I32, (kc, kc), 1),
                     1.0, 0.0).astype(BF16)

    zeros_head = jnp.zeros((HEAD_DIM, qb), BF16)
    for h in range(N_HEADS):
        qh = qT_ref[h * HEAD_DIM:(h + 1) * HEAD_DIM, :]
        qpad_scr[h] = jnp.concatenate([qh, zeros_head] if h % 2 == 0 else [zeros_head, qh], axis=0)
    m_scr[...] = jnp.full((N_HEADS, qb), NEG_BIG, F32)
    l_scr[...] = jnp.zeros((N_HEADS, qb), F32)
    oT_scr[...] = jnp.zeros((ATT_W, qb), F32)

    ones_rows = jnp.ones((16, kc), BF16)

    def att_body(c, run):
        kk = key_scr[c]
        eq = kk == tau_eff
        rank = jnp.dot(ltri, jnp.where(eq, 1.0, 0.0).astype(BF16), preferred_element_type=F32)
        tie = jnp.where(rank <= need - run, 0.0, -jnp.inf)
        bias = jnp.where(kk > tau_eff, 0.0, jnp.where(eq, tie, -jnp.inf))
        row0 = pl.multiple_of(c * kc, kc)
        m_old = m_scr[...]
        l_old = l_scr[...]
        s_all = [jnp.dot(k_ref[pl.ds(row0, kc), (h // 2) * LANES:(h // 2 + 1) * LANES], qpad_scr[h],
                         preferred_element_type=F32) + bias for h in range(N_HEADS)]
        m_new, alphas, p_all = [], [], []
        for h in range(N_HEADS):
            m = m_old[h:h + 1, :]
            mn = jnp.maximum(m, jnp.max(s_all[h], axis=0, keepdims=True))
            p_all.append(jnp.exp2(s_all[h] - mn).astype(BF16))
            alphas.append(jnp.exp2(m - mn))
            m_new.append(mn)
        pv = [jnp.dot(jnp.concatenate([vT_ref[c, h * HEAD_DIM:(h + 1) * HEAD_DIM, :], ones_rows], axis=0),
                      p_all[h], preferred_element_type=F32) for h in range(N_HEADS)]
        alpha_rows = [jnp.broadcast_to(a, (HEAD_DIM, qb)) for a in alphas]
        oT_scr[...] = (jnp.concatenate(alpha_rows, axis=0) * oT_scr[...]
                       + jnp.concatenate([x[0:HEAD_DIM] for x in pv], axis=0))
        m_scr[...] = jnp.concatenate(m_new, axis=0)
        l_scr[...] = (jnp.concatenate(alphas, axis=0) * l_old
                      + jnp.concatenate([x[HEAD_DIM:HEAD_DIM + 1] for x in pv], axis=0))
        return run + rank[kc - 1:kc, :]

    lax.fori_loop(0, nk, att_body, jnp.zeros((1, qb), F32))
    for h in range(N_HEADS):
        hs = slice(h * HEAD_DIM, (h + 1) * HEAD_DIM)
        oT_scr[hs, :] = oT_scr[hs, :] / l_scr[h:h + 1, :]

    o_ref[...] = oT_scr[...].T.astype(BF16)


def _dsa(qT, qiT, kwT, k, vT, kidx, *, qb, kc):
    bsz, seq, _ = k.shape
    assert kc % qb == 0 and seq % kc == 0
    topk = min(TOPK_MAX, seq // 4)
    w_row_block = IDX_DIM // 8
    return pl.pallas_call(
        functools.partial(_dsa_kernel, kc=kc, topk=topk),
        grid=(bsz, seq // qb),
        in_specs=[
            pl.BlockSpec((None, ATT_W, qb), lambda b, i: (b, 0, i)),
            pl.BlockSpec((None, IDX_HEADS * IDX_DIM, qb), lambda b, i: (b, 0, i)),
            pl.BlockSpec((None, 8, qb), lambda b, i: (b, w_row_block, i)),
            pl.BlockSpec((None, seq, ATT_W), lambda b, i: (b, 0, 0), pipeline_mode=pl.Buffered(1)),
            pl.BlockSpec((None, seq // kc, ATT_W, kc), lambda b, i: (b, 0, 0, 0), pipeline_mode=pl.Buffered(1)),
            pl.BlockSpec((None, seq, LANES), lambda b, i: (b, 0, 0), pipeline_mode=pl.Buffered(1)),
        ],
        out_specs=pl.BlockSpec((None, qb, ATT_W), lambda b, i: (b, i, 0)),
        out_shape=jax.ShapeDtypeStruct((bsz, seq, ATT_W), BF16),
        scratch_shapes=[pltpu.VMEM((seq // kc, kc, qb), I32),
                        pltpu.VMEM((seq // kc, kc, qb), I16),
                        pltpu.VMEM((ATT_W, qb), F32),
                        pltpu.VMEM((N_HEADS, LANES, qb), BF16),
                        pltpu.VMEM((N_HEADS, qb), F32),
                        pltpu.VMEM((N_HEADS, qb), F32)],
        compiler_params=_cparams(("parallel", "parallel")),
        name="dsa_attention",
    )(qT, qiT, kwT, k, vT, kidx)


def _memkv_kernel(mem_ref, g_ref, wkv_ref, kT_ref, v_ref):
    d = mem_ref.shape[1]
    mn = _rms(mem_ref[...], g_ref[...]).astype(BF16)
    kv = jnp.dot(mn, wkv_ref[...], preferred_element_type=F32)
    kT_ref[...] = kv[:, 0:d].T.astype(BF16)
    v_ref[...] = kv[:, d:2 * d].astype(BF16)


def _memkv(mem, g, wkv):
    bsz, m, d = mem.shape
    return pl.pallas_call(
        _memkv_kernel,
        grid=(bsz,),
        in_specs=[pl.BlockSpec((None, m, d), lambda b: (b, 0, 0)),
                  pl.BlockSpec((1, d), lambda b: (0, 0)),
                  pl.BlockSpec((d, 2 * d), lambda b: (0, 0))],
        out_specs=[pl.BlockSpec((None, d, m), lambda b: (b, 0, 0)),
                   pl.BlockSpec((None, m, d), lambda b: (b, 0, 0))],
        out_shape=[jax.ShapeDtypeStruct((bsz, d, m), BF16), jax.ShapeDtypeStruct((bsz, m, d), BF16)],
        compiler_params=_cparams(("parallel",)),
        name="mem_kv",
    )(mem, g, wkv)


def _xattn_kernel(x_ref, yab_ref, yc_ref, wab_ref, wc_ref, g_ref, wq_ref, kT_ref, v_ref, wo_ref, o_ref):
    d = x_ref.shape[1]
    hd = d // XA_HEADS
    x = (x_ref[...]
         + jnp.dot(yab_ref[...], wab_ref[...], preferred_element_type=F32)
         + jnp.dot(yc_ref[...], wc_ref[...], preferred_element_type=F32))
    hq = _rms(x, g_ref[...]).astype(BF16)
    q = (jnp.dot(hq, wq_ref[...], preferred_element_type=F32) * (hd ** -0.5)).astype(BF16)
    outs = []
    for h in range(XA_HEADS):
        s = jnp.dot(q[:, h * hd:(h + 1) * hd], kT_ref[h * hd:(h + 1) * hd, :], preferred_element_type=F32)
        m = jnp.max(s, axis=-1, keepdims=True)
        p = jnp.exp(s - m)
        l = jnp.sum(p, axis=-1, keepdims=True)
        o = jnp.dot(p.astype(BF16), v_ref[:, h * hd:(h + 1) * hd], preferred_element_type=F32) / l
        outs.append(o.astype(BF16))
    o_ref[...] = x + jnp.dot(jnp.concatenate(outs, axis=-1), wo_ref[...], preferred_element_type=F32)


def _xattn(x, yab, yc, wab, wc, g, wq, kT, v, wo, *, bsz, seq, tm):
    d = x.shape[1]
    m = v.shape[1]
    ns = seq // tm
    tok_c = lambda c: pl.BlockSpec((tm, c), lambda b, s: (b * ns + s, 0))
    tok = tok_c(d)
    full = lambda a: pl.BlockSpec(a.shape, lambda b, s: (0, 0))
    return pl.pallas_call(
        _xattn_kernel,
        grid=(bsz, ns),
        in_specs=[tok, tok_c(yab.shape[1]), tok_c(yc.shape[1]), full(wab), full(wc),
                  pl.BlockSpec((1, d), lambda b, s: (0, 0)),
                  pl.BlockSpec((d, d), lambda b, s: (0, 0)),
                  pl.BlockSpec((None, d, m), lambda b, s: (b, 0, 0)),
                  pl.BlockSpec((None, m, d), lambda b, s: (b, 0, 0)),
                  pl.BlockSpec((d, d), lambda b, s: (0, 0))],
        out_specs=tok,
        out_shape=jax.ShapeDtypeStruct((bsz * seq, d), F32),
        compiler_params=_cparams(("parallel", "parallel")),
        name="mem_xattn",
    )(x, yab, yc, wab, wc, g, wq, kT, v, wo)


def _tiles(seq, d_ff):
    assert seq % 512 == 0, "sequence length must be a multiple of 512"
    tf = d_ff // 2 if (d_ff // 2) % LANES == 0 else d_ff
    return dict(
        tm_ffn=1024, tf=tf,
        tm_proj=512, rc_conv=64,
        qb=512, kc=512,
        tm_xa=512, tm_rope=1024,
    )


def kernel(x, mem, positions, ffn1_norm, ffn1_w_gate, ffn1_w_up, ffn1_w_down, mix_norm, w_in,
           conf_dw, conf_dw_b, conf_ln_g, conf_ln_b, sc_dw, w_out, xa_norm, mem_norm,
           xa_wq, xa_wkv, xa_wo, ffn2_norm, ffn2_w_gate, ffn2_w_up, ffn2_w_down, final_norm):
    bsz, seq, d = x.shape
    depth = w_in.shape[0]
    t = bsz * seq
    tl = _tiles(seq, ffn1_w_gate.shape[2])
    bf = lambda a: a.astype(BF16)
    row = lambda a: a.reshape(1, -1)

    cos, sin = _rope_tables(positions, tl["tm_rope"])
    xf = x.reshape(t, d)
    w_in_pad = jnp.pad(w_in, ((0, 0), (0, 0), (0, N_IN_PAD - w_in.shape[2])))

    for l in range(depth):
        xf = _ffn(xf, row(ffn1_norm[l]), bf(ffn1_w_gate[l]), bf(ffn1_w_up[l]), bf(ffn1_w_down[l]),
                  row(final_norm), final=False, tm=tl["tm_ffn"], tf=tl["tf"])

        yab, qT, k, vT, qiT, kidx, kwT = _proj(
            xf, row(mix_norm[l]), bf(w_in_pad[l]), cos, sin,
            conf_dw[l], row(conf_dw_b[l]), row(conf_ln_g[l]), row(conf_ln_b[l]), sc_dw[l],
            bsz=bsz, seq=seq, tm=tl["tm_proj"], kc=tl["kc"], rc=tl["rc_conv"])
        yc = _dsa(qT, qiT, kwT, k, vT, kidx, qb=tl["qb"], kc=tl["kc"])

        n_ab = CONV_CH + SC_CH
        kT_mem, v_mem = _memkv(mem, row(mem_norm[l]), bf(xa_wkv[l]))
        xf = _xattn(xf, yab.reshape(t, n_ab), yc.reshape(t, ATT_W), bf(w_out[l, 0:n_ab]), bf(w_out[l, n_ab:]),
                    row(xa_norm[l]), bf(xa_wq[l]), kT_mem, v_mem, bf(xa_wo[l]),
                    bsz=bsz, seq=seq, tm=tl["tm_xa"])

        xf = _ffn(xf, row(ffn2_norm[l]), bf(ffn2_w_gate[l]), bf(ffn2_w_up[l]), bf(ffn2_w_down[l]),
                  row(final_norm), final=(l == depth - 1), tm=tl["tm_ffn"], tf=tl["tf"])

    return xf.reshape(bsz, seq, d)
```

```python
import functools
import math

import jax
import jax.numpy as jnp
from jax import lax
from jax.experimental import pallas as pl
from jax.experimental.pallas import tpu as pltpu

F32 = jnp.float32
BF16 = jnp.bfloat16
I32 = jnp.int32
I16 = jnp.int16

CONV_CH = 256
CONV_W = 31
SC_CH = 256
SC_W = 3
N_HEADS = 8
HEAD_DIM = 64
ATT_W = N_HEADS * HEAD_DIM
IDX_HEADS = 4
IDX_DIM = 64
TOPK_MAX = 256
ROPE_THETA = 10000.0
XA_HEADS = 4
IDX_SCALE = (IDX_HEADS ** -0.5) * (IDX_DIM ** -0.5)
ATT_SCALE = HEAD_DIM ** -0.5
QK_SCALE = ATT_SCALE * math.log2(math.e)

LANES = 128
SUBLANES = 8
VMEM_LIMIT = 58 * 1024 * 1024

CONV_COLS = 2 * CONV_CH + 3 * SC_CH
Q_OFF = CONV_COLS
K_OFF = Q_OFF + ATT_W
V_OFF = K_OFF + ATT_W
QI_OFF = V_OFF + ATT_W
KW_OFF = QI_OFF + IDX_HEADS * IDX_DIM
N_IN_PAD = KW_OFF + LANES

HALF = 2 ** 15
INT_MAX = 2 ** 31 - 1
NEG_INF_KEY = -2139095041
NEG_BIG = -1e30


def _cparams(sem, vmem=VMEM_LIMIT):
    return pltpu.CompilerParams(dimension_semantics=sem, vmem_limit_bytes=vmem)


def _rms(x, g, eps=1e-6):
    return x * lax.rsqrt(jnp.mean(x * x, axis=-1, keepdims=True) + eps) * g


def _sigmoid(x):
    return 1.0 / (1.0 + jnp.exp(-x))


def _rope_kernel(pos_ref, invf_ref, sgn_ref, cos_ref, sin_ref):
    ang = pos_ref[...].astype(F32) * invf_ref[...]
    cos_ref[...] = jnp.cos(ang)
    sin_ref[...] = jnp.sin(ang) * sgn_ref[...]


def _rope_tables(positions, tm):
    t = positions.size
    half = HEAD_DIM // 2
    inv_freq = ROPE_THETA ** (-jnp.arange(0, HEAD_DIM, 2, dtype=F32) / HEAD_DIM)
    invf = jnp.tile(inv_freq, LANES // half)[None, :]
    sgn = jnp.tile(jnp.concatenate([-jnp.ones((half,), F32), jnp.ones((half,), F32)]),
                   LANES // HEAD_DIM)[None, :]
    pos = jnp.broadcast_to(positions.reshape(t, 1), (t, LANES))
    row = pl.BlockSpec((tm, LANES), lambda i: (i, 0))
    one = pl.BlockSpec((1, LANES), lambda i: (0, 0))
    return pl.pallas_call(
        _rope_kernel,
        grid=(t // tm,),
        in_specs=[row, one, one],
        out_specs=[row, row],
        out_shape=[jax.ShapeDtypeStruct((t, LANES), F32)] * 2,
        compiler_params=_cparams(("parallel",)),
        name="rope_tables",
    )(pos, invf, sgn)


def _ffn_kernel(x_ref, g_ref, wg_ref, wu_ref, wd_ref, fg_ref, o_ref, h_scr, acc_scr, *, final):
    j = pl.program_id(1)

    @pl.when(j == 0)
    def _():
        h_scr[...] = _rms(x_ref[...], g_ref[...]).astype(BF16)
        acc_scr[...] = jnp.zeros_like(acc_scr)

    h = h_scr[...]
    a = jnp.dot(h, wg_ref[...], preferred_element_type=F32)
    b = jnp.dot(h, wu_ref[...], preferred_element_type=F32)
    t = (a * _sigmoid(a)) * b
    acc_scr[...] += jnp.dot(t.astype(BF16), wd_ref[...], preferred_element_type=F32)

    @pl.when(j == pl.num_programs(1) - 1)
    def _():
        y = x_ref[...] + 0.5 * acc_scr[...]
        if final:
            y = _rms(y, fg_ref[...])
        o_ref[...] = y


def _ffn(x, g, wg, wu, wd, fg, *, final, tm, tf):
    t, d = x.shape
    f = wg.shape[1]
    return pl.pallas_call(
        functools.partial(_ffn_kernel, final=final),
        grid=(t // tm, f // tf),
        in_specs=[
            pl.BlockSpec((tm, d), lambda i, j: (i, 0)),
            pl.BlockSpec((1, d), lambda i, j: (0, 0)),
            pl.BlockSpec((d, tf), lambda i, j: (0, j), pipeline_mode=pl.Buffered(1)),
            pl.BlockSpec((d, tf), lambda i, j: (0, j), pipeline_mode=pl.Buffered(1)),
            pl.BlockSpec((tf, d), lambda i, j: (j, 0), pipeline_mode=pl.Buffered(1)),
            pl.BlockSpec((1, d), lambda i, j: (0, 0)),
        ],
        out_specs=pl.BlockSpec((tm, d), lambda i, j: (i, 0)),
        out_shape=jax.ShapeDtypeStruct((t, d), F32),
        scratch_shapes=[pltpu.VMEM((tm, d), BF16), pltpu.VMEM((tm, d), F32)],
        compiler_params=_cparams(("parallel", "arbitrary")),
        name="ffn_final" if final else "ffn",
    )(x, g, wg, wu, wd, fg)


CONV_HALO = 32
SC_HALO = 8


def _conv_mixers(cv, dw_ref, dwb_ref, lng_ref, lnb_ref, sw_ref, yab_ref, ha_scr, ub_scr, sh_scr, *, rc):
    ts = cv.shape[0]
    sub = SUBLANES
    first_tap = CONV_HALO - (CONV_W - 1)

    @pl.when(pl.program_id(1) == 0)
    def _():
        ha_scr[0:CONV_HALO, :] = jnp.zeros((CONV_HALO, CONV_CH), F32)
        ub_scr[0:SC_HALO, :] = jnp.zeros((SC_HALO, SC_CH), F32)

    b_off = 2 * CONV_CH
    ha_scr[CONV_HALO:CONV_HALO + ts, :] = cv[:, 0:CONV_CH] * _sigmoid(cv[:, CONV_CH:2 * CONV_CH])
    ub_scr[SC_HALO:SC_HALO + ts, :] = cv[:, b_off + SC_CH:b_off + 2 * SC_CH] * cv[:, b_off + 2 * SC_CH:b_off + 3 * SC_CH]
    span = sh_scr.shape[1]
    for ph in range(1, sub):
        sh_scr[ph - 1] = ha_scr[ph:ph + span, :]

    for r in range(ts // rc):
        base = r * rc
        acc = jnp.zeros((rc, CONV_CH), F32)
        for j in range(CONV_W):
            off = base + first_tap + j
            ph = off % sub
            src = ha_scr[off:off + rc, :] if ph == 0 else sh_scr[ph - 1, off - ph:off - ph + rc, :]
            acc = acc + src * dw_ref[j:j + 1, :]
        acc = acc + dwb_ref[...]
        mu = jnp.mean(acc, axis=-1, keepdims=True)
        cen = acc - mu
        var = jnp.mean(cen * cen, axis=-1, keepdims=True)
        y = cen * lax.rsqrt(var + 1e-5) * lng_ref[...] + lnb_ref[...]
        yab_ref[base:base + rc, 0:CONV_CH] = (y * _sigmoid(y)).astype(BF16)

        accb = jnp.zeros((rc, SC_CH), F32)
        for j in range(SC_W):
            off = base + SC_HALO - (SC_W - 1) + j
            accb = accb + ub_scr[off:off + rc, :] * sw_ref[j:j + 1, :]
        yab_ref[base:base + rc, CONV_CH:CONV_CH + SC_CH] = (cv[base:base + rc, b_off:b_off + SC_CH] * accb).astype(BF16)

    ha_scr[0:CONV_HALO, :] = ha_scr[ts:ts + CONV_HALO, :]
    ub_scr[0:SC_HALO, :] = ub_scr[ts:ts + SC_HALO, :]


def _proj_kernel(x_ref, g_ref, w_ref, cos_ref, sin_ref, dw_ref, dwb_ref, lng_ref, lnb_ref, sw_ref,
                 yab_ref, qT_ref, k_ref, vT_ref, qiT_ref, kidx_ref, kwT_ref,
                 ha_scr, ub_scr, sh_scr, *, kc, rc):
    tm = x_ref.shape[0]
    h = _rms(x_ref[...], g_ref[...]).astype(BF16)
    cos = cos_ref[...]
    sin = sin_ref[...]
    lane = lax.broadcasted_iota(I32, (tm, LANES), 1)
    first_half = (lane % HEAD_DIM) < (HEAD_DIM // 2)

    def rope(t, c, s):
        rot = jnp.where(first_half, pltpu.roll(t, LANES - HEAD_DIM // 2, 1), pltpu.roll(t, HEAD_DIM // 2, 1))
        return t * c + rot * s

    q = jnp.dot(h, w_ref[:, Q_OFF:Q_OFF + ATT_W], preferred_element_type=F32)
    for g in range(ATT_W // LANES):
        qr = rope(q[:, g * LANES:(g + 1) * LANES], cos, sin) * QK_SCALE
        qT_ref[g * LANES:(g + 1) * LANES, :] = qr.T.astype(BF16)

    k = jnp.dot(h, w_ref[:, K_OFF:K_OFF + ATT_W], preferred_element_type=F32)
    for g in range(ATT_W // LANES):
        k_ref[:, g * LANES:(g + 1) * LANES] = rope(k[:, g * LANES:(g + 1) * LANES], cos, sin).astype(BF16)

    v = jnp.dot(h, w_ref[:, V_OFF:V_OFF + ATT_W], preferred_element_type=F32)
    for c in range(tm // kc):
        vT_ref[c] = v[c * kc:(c + 1) * kc, :].T.astype(BF16)

    qi = jnp.dot(h, w_ref[:, QI_OFF:QI_OFF + IDX_HEADS * IDX_DIM], preferred_element_type=F32)
    for g in range(IDX_HEADS * IDX_DIM // LANES):
        qiT_ref[g * LANES:(g + 1) * LANES, :] = rope(qi[:, g * LANES:(g + 1) * LANES], cos, sin).T.astype(BF16)

    kw = jnp.dot(h, w_ref[:, KW_OFF:KW_OFF + LANES], preferred_element_type=F32)
    is_kidx = lane < IDX_DIM
    kw = rope(kw, jnp.where(is_kidx, cos, 1.0), jnp.where(is_kidx, sin, 0.0))
    kidx_ref[...] = kw.astype(BF16)
    kwT_ref[...] = kw.T

    cv = jnp.dot(h, w_ref[:, 0:CONV_COLS], preferred_element_type=F32)
    _conv_mixers(cv, dw_ref, dwb_ref, lng_ref, lnb_ref, sw_ref, yab_ref, ha_scr, ub_scr, sh_scr, rc=rc)


def _proj(x, g, w, cos, sin, dw, dwb, lng, lnb, sw, *, bsz, seq, tm, kc, rc):
    d = x.shape[1]
    ns = seq // tm
    tok = lambda b, s: (b * ns + s, 0)
    small = lambda r, c: pl.BlockSpec((r, c), lambda b, s: (0, 0))
    return pl.pallas_call(
        functools.partial(_proj_kernel, kc=kc, rc=rc),
        grid=(bsz, ns),
        in_specs=[
            pl.BlockSpec((tm, d), tok),
            small(1, d),
            small(d, N_IN_PAD),
            pl.BlockSpec((tm, LANES), tok),
            pl.BlockSpec((tm, LANES), tok),
            small(CONV_W, CONV_CH), small(1, CONV_CH), small(1, CONV_CH), small(1, CONV_CH),
            small(SC_W, SC_CH),
        ],
        out_specs=[
            pl.BlockSpec((None, tm, CONV_CH + SC_CH), lambda b, s: (b, s, 0)),
            pl.BlockSpec((None, ATT_W, tm), lambda b, s: (b, 0, s)),
            pl.BlockSpec((None, tm, ATT_W), lambda b, s: (b, s, 0)),
            pl.BlockSpec((None, tm // kc, ATT_W, kc), lambda b, s: (b, s, 0, 0)),
            pl.BlockSpec((None, IDX_HEADS * IDX_DIM, tm), lambda b, s: (b, 0, s)),
            pl.BlockSpec((None, tm, LANES), lambda b, s: (b, s, 0)),
            pl.BlockSpec((None, LANES, tm), lambda b, s: (b, 0, s)),
        ],
        out_shape=[
            jax.ShapeDtypeStruct((bsz, seq, CONV_CH + SC_CH), BF16),
            jax.ShapeDtypeStruct((bsz, ATT_W, seq), BF16),
            jax.ShapeDtypeStruct((bsz, seq, ATT_W), BF16),
            jax.ShapeDtypeStruct((bsz, seq // kc, ATT_W, kc), BF16),
            jax.ShapeDtypeStruct((bsz, IDX_HEADS * IDX_DIM, seq), BF16),
            jax.ShapeDtypeStruct((bsz, seq, LANES), BF16),
            jax.ShapeDtypeStruct((bsz, LANES, seq), F32),
        ],
        scratch_shapes=[pltpu.VMEM((tm + CONV_HALO, CONV_CH), F32),
                        pltpu.VMEM((tm + SC_HALO, SC_CH), F32),
                        pltpu.VMEM((SUBLANES - 1, tm + CONV_HALO - SUBLANES, CONV_CH), F32)],
        compiler_params=_cparams(("arbitrary", "arbitrary")),
        name="in_proj",
    )(x, g, w, cos, sin, dw, dwb, lng, lnb, sw)


def _sortable(x):
    b = lax.bitcast_convert_type(x, I32)
    return b ^ ((b >> 31) & INT_MAX)


def _dsa_kernel(qT_ref, qiT_ref, wT_ref, k_ref, vT_ref, kidx_ref, o_ref,
                key_scr, k16_scr, oT_scr, qpad_scr, m_scr, l_scr, *, kc, topk):
    qb = qT_ref.shape[1]
    i = pl.program_id(1)
    nk = (i * qb) // kc + 1
    q_pos = i * qb + lax.broadcasted_iota(I32, (1, qb), 1)
    row_iota = lax.broadcasted_iota(I32, (kc, qb), 0)

    w = wT_ref[...] * IDX_SCALE
    zeros_half = jnp.zeros((LANES - IDX_DIM, qb), BF16)
    qi = [jnp.concatenate([qiT_ref[h * IDX_DIM:(h + 1) * IDX_DIM, :], zeros_half], axis=0)
          for h in range(IDX_HEADS)]

    def score_chunk(c, diagonal):
        kcs = kidx_ref[pl.ds(pl.multiple_of(c * kc, kc), kc), :]
        sc = jnp.zeros((kc, qb), F32)
        for h in range(IDX_HEADS):
            lg = jnp.dot(kcs, qi[h], preferred_element_type=F32)
            sc = sc + w[h:h + 1, :] * jnp.maximum(lg, 0.0)
        if diagonal:
            sc = jnp.where((c * kc + row_iota) <= q_pos, sc, -jnp.inf)
        key = _sortable(sc)
        key_scr[c] = key
        k16_scr[c] = (key >> 16).astype(I16)

    def score_body(c, carry):
        score_chunk(c, False)
        return carry

    lax.fori_loop(0, nk - 1, score_body, 0)
    score_chunk(nk - 1, True)

    fold = 16 * 2 * SUBLANES * LANES // qb

    def count16(mid):
        midb = jnp.broadcast_to(mid, (fold, qb)).astype(I16)

        def body(c, acc):
            for r in range(kc // fold):
                acc = acc + jnp.where(k16_scr[c, r * fold:(r + 1) * fold, :] >= midb, jnp.int16(1), jnp.int16(0))
            return acc

        acc = lax.fori_loop(0, nk, body, jnp.zeros((fold, qb), I16))
        return jnp.sum(acc.astype(I32), axis=0, keepdims=True)

    def bisect16(want, cnt_all):
        def step(_, carry):
            lo, hi, cnt_lo, cnt_hi = carry
            mid = (lo + hi) >> 1
            cnt = jnp.where(mid == -HALF, cnt_all, count16(mid))
            ok = cnt >= want
            return (jnp.where(ok, mid, lo), jnp.where(ok, hi, mid),
                    jnp.where(ok, cnt, cnt_lo), jnp.where(ok, cnt_hi, cnt))
        zero = jnp.zeros((1, qb), I32)
        t, _, cnt_t, cnt_above = lax.fori_loop(0, 16, step, (zero - HALF, zero + HALF, cnt_all, zero))
        return t, cnt_t, cnt_above

    n_stored = nk * kc + jnp.zeros((1, qb), I32)
    t_hi, cge_hi, cgt_hi = bisect16(topk, n_stored)

    def low_body(c, carry):
        kk = key_scr[c]
        low = (kk & (2 * HALF - 1)) - HALF
        k16_scr[c] = jnp.where((kk >> 16) == t_hi, low, -HALF).astype(I16)
        return carry

    lax.fori_loop(0, nk, low_body, 0)
    t_lo, cge_lo, cgt_lo = bisect16(topk - cgt_hi, cge_hi - cgt_hi)

    tau = t_hi * (2 * HALF) + (t_lo + HALF)
    full = tau > NEG_INF_KEY
    tau_eff = jnp.maximum(tau, NEG_INF_KEY)
    need = jnp.where(full, topk - (cgt_hi + cgt_lo), 0).astype(F32)
    ltri = jnp.where(lax.broadcasted_iota(I32, (kc, kc), 0) >= lax.broadcasted_iota(I32, (kc, kc), 1),
                     1.0, 0.0).astype(BF16)

    zeros_head = jnp.zeros((HEAD_DIM, qb), BF16)
    for h in range(N_HEADS):
        qh = qT_ref[h * HEAD_DIM:(h + 1) * HEAD_DIM, :]
        qpad_scr[h] = jnp.concatenate([qh, zeros_head] if h % 2 == 0 else [zeros_head, qh], axis=0)
    m_scr[...] = jnp.full((N_HEADS, qb), NEG_BIG, F32)
    l_scr[...] = jnp.zeros((N_HEADS, qb), F32)
    oT_scr[...] = jnp.zeros((ATT_W, qb), F32)

    ones_rows = jnp.ones((16, kc), BF16)

    def att_body(c, run):
        kk = key_scr[c]
        eq = kk == tau_eff
        rank = jnp.dot(ltri, jnp.where(eq, 1.0, 0.0).astype(BF16), preferred_element_type=F32)
        tie = jnp.where(rank <= need - run, 0.0, -jnp.inf)
        bias = jnp.where(kk > tau_eff, 0.0, jnp.where(eq, tie, -jnp.inf))
        row0 = pl.multiple_of(c * kc, kc)
        m_old = m_scr[...]
        l_old = l_scr[...]
        s_all = [jnp.dot(k_ref[pl.ds(row0, kc), (h // 2) * LANES:(h // 2 + 1) * LANES], qpad_scr[h],
                         preferred_element_type=F32) + bias for h in range(N_HEADS)]
        m_new, alphas, p_all = [], [], []
        for h in range(N_HEADS):
            m = m_old[h:h + 1, :]
            mn = jnp.maximum(m, jnp.max(s_all[h], axis=0, keepdims=True))
            p_all.append(jnp.exp2(s_all[h] - mn).astype(BF16))
            alphas.append(jnp.exp2(m - mn))
            m_new.append(mn)
        pv = [jnp.dot(jnp.concatenate([vT_ref[c, h * HEAD_DIM:(h + 1) * HEAD_DIM, :], ones_rows], axis=0),
                      p_all[h], preferred_element_type=F32) for h in range(N_HEADS)]
        alpha_rows = [jnp.broadcast_to(a, (HEAD_DIM, qb)) for a in alphas]
        oT_scr[...] = (jnp.concatenate(alpha_rows, axis=0) * oT_scr[...]
                       + jnp.concatenate([x[0:HEAD_DIM] for x in pv], axis=0))
        m_scr[...] = jnp.concatenate(m_new, axis=0)
        l_scr[...] = (jnp.concatenate(alphas, axis=0) * l_old
                      + jnp.concatenate([x[HEAD_DIM:HEAD_DIM + 1] for x in pv], axis=0))
        return run + rank[kc - 1:kc, :]

    lax.fori_loop(0, nk, att_body, jnp.zeros((1, qb), F32))
    for h in range(N_HEADS):
        hs = slice(h * HEAD_DIM, (h + 1) * HEAD_DIM)
        oT_scr[hs, :] = oT_scr[hs, :] / l_scr[h:h + 1, :]

    o_ref[...] = oT_scr[...].T.astype(BF16)


def _dsa(qT, qiT, kwT, k, vT, kidx, *, qb, kc):
    bsz, seq, _ = k.shape
    assert kc % qb == 0 and seq % kc == 0
    topk = min(TOPK_MAX, seq // 4)
    w_row_block = IDX_DIM // 8
    return pl.pallas_call(
        functools.partial(_dsa_kernel, kc=kc, topk=topk),
        grid=(bsz, seq // qb),
        in_specs=[
            pl.BlockSpec((None, ATT_W, qb), lambda b, i: (b, 0, i)),
            pl.BlockSpec((None, IDX_HEADS * IDX_DIM, qb), lambda b, i: (b, 0, i)),
            pl.BlockSpec((None, 8, qb), lambda b, i: (b, w_row_block, i)),
            pl.BlockSpec((None, seq, ATT_W), lambda b, i: (b, 0, 0), pipeline_mode=pl.Buffered(1)),
            pl.BlockSpec((None, seq // kc, ATT_W, kc), lambda b, i: (b, 0, 0, 0), pipeline_mode=pl.Buffered(1)),
            pl.BlockSpec((None, seq, LANES), lambda b, i: (b, 0, 0), pipeline_mode=pl.Buffered(1)),
        ],
        out_specs=pl.BlockSpec((None, qb, ATT_W), lambda b, i: (b, i, 0)),
        out_shape=jax.ShapeDtypeStruct((bsz, seq, ATT_W), BF16),
        scratch_shapes=[pltpu.VMEM((seq // kc, kc, qb), I32),
                        pltpu.VMEM((seq // kc, kc, qb), I16),
                        pltpu.VMEM((ATT_W, qb), F32),
                        pltpu.VMEM((N_HEADS, LANES, qb), BF16),
                        pltpu.VMEM((N_HEADS, qb), F32),
                        pltpu.VMEM((N_HEADS, qb), F32)],
        compiler_params=_cparams(("parallel", "parallel")),
        name="dsa_attention",
    )(qT, qiT, kwT, k, vT, kidx)


def _memkv_kernel(mem_ref, g_ref, wkv_ref, kT_ref, v_ref):
    d = mem_ref.shape[1]
    mn = _rms(mem_ref[...], g_ref[...]).astype(BF16)
    kv = jnp.dot(mn, wkv_ref[...], preferred_element_type=F32)
    kT_ref[...] = kv[:, 0:d].T.astype(BF16)
    v_ref[...] = kv[:, d:2 * d].astype(BF16)


def _memkv(mem, g, wkv):
    bsz, m, d = mem.shape
    return pl.pallas_call(
        _memkv_kernel,
        grid=(bsz,),
        in_specs=[pl.BlockSpec((None, m, d), lambda b: (b, 0, 0)),
                  pl.BlockSpec((1, d), lambda b: (0, 0)),
                  pl.BlockSpec((d, 2 * d), lambda b: (0, 0))],
        out_specs=[pl.BlockSpec((None, d, m), lambda b: (b, 0, 0)),
                   pl.BlockSpec((None, m, d), lambda b: (b, 0, 0))],
        out_shape=[jax.ShapeDtypeStruct((bsz, d, m), BF16), jax.ShapeDtypeStruct((bsz, m, d), BF16)],
        compiler_params=_cparams(("parallel",)),
        name="mem_kv",
    )(mem, g, wkv)


def _xattn_kernel(x_ref, yab_ref, yc_ref, wab_ref, wc_ref, g_ref, wq_ref, kT_ref, v_ref, wo_ref, o_ref):
    d = x_ref.shape[1]
    hd = d // XA_HEADS
    x = (x_ref[...]
         + jnp.dot(yab_ref[...], wab_ref[...], preferred_element_type=F32)
         + jnp.dot(yc_ref[...], wc_ref[...], preferred_element_type=F32))
    hq = _rms(x, g_ref[...]).astype(BF16)
    q = (jnp.dot(hq, wq_ref[...], preferred_element_type=F32) * (hd ** -0.5)).astype(BF16)
    outs = []
    for h in range(XA_HEADS):
        s = jnp.dot(q[:, h * hd:(h + 1) * hd], kT_ref[h * hd:(h + 1) * hd, :], preferred_element_type=F32)
        m = jnp.max(s, axis=-1, keepdims=True)
        p = jnp.exp(s - m)
        l = jnp.sum(p, axis=-1, keepdims=True)
        o = jnp.dot(p.astype(BF16), v_ref[:, h * hd:(h + 1) * hd], preferred_element_type=F32) / l
        outs.append(o.astype(BF16))
    o_ref[...] = x + jnp.dot(jnp.concatenate(outs, axis=-1), wo_ref[...], preferred_element_type=F32)


def _xattn(x, yab, yc, wab, wc, g, wq, kT, v, wo, *, bsz, seq, tm):
    d = x.shape[1]
    m = v.shape[1]
    ns = seq // tm
    tok_c = lambda c: pl.BlockSpec((tm, c), lambda b, s: (b * ns + s, 0))
    tok = tok_c(d)
    full = lambda a: pl.BlockSpec(a.shape, lambda b, s: (0, 0))
    return pl.pallas_call(
        _xattn_kernel,
        grid=(bsz, ns),
        in_specs=[tok, tok_c(yab.shape[1]), tok_c(yc.shape[1]), full(wab), full(wc),
                  pl.BlockSpec((1, d), lambda b, s: (0, 0)),
                  pl.BlockSpec((d, d), lambda b, s: (0, 0)),
                  pl.BlockSpec((None, d, m), lambda b, s: (b, 0, 0)),
                  pl.BlockSpec((None, m, d), lambda b, s: (b, 0, 0)),
                  pl.BlockSpec((d, d), lambda b, s: (0, 0))],
        out_specs=tok,
        out_shape=jax.ShapeDtypeStruct((bsz * seq, d), F32),
        compiler_params=_cparams(("parallel", "parallel")),
        name="mem_xattn",
    )(x, yab, yc, wab, wc, g, wq, kT, v, wo)


def _tiles(seq, d_ff):
    assert seq % 512 == 0, "sequence length must be a multiple of 512"
    tf = d_ff
    return dict(
        tm_ffn=512, tf=tf,
        tm_proj=512, rc_conv=64,
        qb=512, kc=512,
        tm_xa=512, tm_rope=1024,
    )


def kernel(x, mem, positions, ffn1_norm, ffn1_w_gate, ffn1_w_up, ffn1_w_down, mix_norm, w_in,
           conf_dw, conf_dw_b, conf_ln_g, conf_ln_b, sc_dw, w_out, xa_norm, mem_norm,
           xa_wq, xa_wkv, xa_wo, ffn2_norm, ffn2_w_gate, ffn2_w_up, ffn2_w_down, final_norm):
    bsz, seq, d = x.shape
    depth = w_in.shape[0]
    t = bsz * seq
    tl = _tiles(seq, ffn1_w_gate.shape[2])
    bf = lambda a: a.astype(BF16)
    row = lambda a: a.reshape(1, -1)

    cos, sin = _rope_tables(positions, tl["tm_rope"])
    xf = x.reshape(t, d)
    w_in_pad = jnp.pad(w_in, ((0, 0), (0, 0), (0, N_IN_PAD - w_in.shape[2])))

    for l in range(depth):
        xf = _ffn(xf, row(ffn1_norm[l]), bf(ffn1_w_gate[l]), bf(ffn1_w_up[l]), bf(ffn1_w_down[l]),
                  row(final_norm), final=False, tm=tl["tm_ffn"], tf=tl["tf"])

        yab, qT, k, vT, qiT, kidx, kwT = _proj(
            xf, row(mix_norm[l]), bf(w_in_pad[l]), cos, sin,
            conf_dw[l], row(conf_dw_b[l]), row(conf_ln_g[l]), row(conf_ln_b[l]), sc_dw[l],
            bsz=bsz, seq=seq, tm=tl["tm_proj"], kc=tl["kc"], rc=tl["rc_conv"])
        yc = _dsa(qT, qiT, kwT, k, vT, kidx, qb=tl["qb"], kc=tl["kc"])

        n_ab = CONV_CH + SC_CH
        kT_mem, v_mem = _memkv(mem, row(mem_norm[l]), bf(xa_wkv[l]))
        xf = _xattn(xf, yab.reshape(t, n_ab), yc.reshape(t, ATT_W), bf(w_out[l, 0:n_ab]), bf(w_out[l, n_ab:]),
                    row(xa_norm[l]), bf(xa_wq[l]), kT_mem, v_mem, bf(xa_wo[l]),
                    bsz=bsz, seq=seq, tm=tl["tm_xa"])

        xf = _ffn(xf, row(ffn2_norm[l]), bf(ffn2_w_gate[l]), bf(ffn2_w_up[l]), bf(ffn2_w_down[l]),
                  row(final_norm), final=(l == depth - 1), tm=tl["tm_ffn"], tf=tl["tf"])

    return xf.reshape(bsz, seq, d)
```

```python
import functools
import math

import jax
import jax.numpy as jnp
from jax import lax
from jax.experimental import pallas as pl
from jax.experimental.pallas import tpu as pltpu

F32 = jnp.float32
BF16 = jnp.bfloat16
I32 = jnp.int32
I16 = jnp.int16

CONV_CH = 256
CONV_W = 31
SC_CH = 256
SC_W = 3
N_HEADS = 8
HEAD_DIM = 64
ATT_W = N_HEADS * HEAD_DIM
IDX_HEADS = 4
IDX_DIM = 64
TOPK_MAX = 256
ROPE_THETA = 10000.0
XA_HEADS = 4
IDX_SCALE = (IDX_HEADS ** -0.5) * (IDX_DIM ** -0.5)
ATT_SCALE = HEAD_DIM ** -0.5
QK_SCALE = ATT_SCALE * math.log2(math.e)

LANES = 128
SUBLANES = 8
VMEM_LIMIT = 58 * 1024 * 1024

CONV_COLS = 2 * CONV_CH + 3 * SC_CH
Q_OFF = CONV_COLS
K_OFF = Q_OFF + ATT_W
V_OFF = K_OFF + ATT_W
QI_OFF = V_OFF + ATT_W
KW_OFF = QI_OFF + IDX_HEADS * IDX_DIM
N_IN_PAD = KW_OFF + LANES

HALF = 2 ** 15
INT_MAX = 2 ** 31 - 1
NEG_INF_KEY = -2139095041
NEG_BIG = -1e30


def _cparams(sem, vmem=VMEM_LIMIT):
    return pltpu.CompilerParams(dimension_semantics=sem, vmem_limit_bytes=vmem)


def _rms(x, g, eps=1e-6):
    return x * lax.rsqrt(jnp.mean(x * x, axis=-1, keepdims=True) + eps) * g


def _sigmoid(x):
    return 1.0 / (1.0 + jnp.exp(-x))


def _rope_kernel(pos_ref, invf_ref, sgn_ref, cos_ref, sin_ref):
    ang = pos_ref[...].astype(F32) * invf_ref[...]
    cos_ref[...] = jnp.cos(ang)
    sin_ref[...] = jnp.sin(ang) * sgn_ref[...]


def _rope_tables(positions, tm):
    t = positions.size
    half = HEAD_DIM // 2
    inv_freq = ROPE_THETA ** (-jnp.arange(0, HEAD_DIM, 2, dtype=F32) / HEAD_DIM)
    invf = jnp.tile(inv_freq, LANES // half)[None, :]
    sgn = jnp.tile(jnp.concatenate([-jnp.ones((half,), F32), jnp.ones((half,), F32)]),
                   LANES // HEAD_DIM)[None, :]
    pos = jnp.broadcast_to(positions.reshape(t, 1), (t, LANES))
    row = pl.BlockSpec((tm, LANES), lambda i: (i, 0))
    one = pl.BlockSpec((1, LANES), lambda i: (0, 0))
    return pl.pallas_call(
        _rope_kernel,
        grid=(t // tm,),
        in_specs=[row, one, one],
        out_specs=[row, row],
        out_shape=[jax.ShapeDtypeStruct((t, LANES), F32)] * 2,
        compiler_params=_cparams(("parallel",)),
        name="rope_tables",
    )(pos, invf, sgn)


def _ffn_kernel(x_ref, g_ref, wg_ref, wu_ref, wd_ref, fg_ref, o_ref, *, final):
    x = x_ref[...]
    h = _rms(x, g_ref[...]).astype(BF16)
    a = jnp.dot(h, wg_ref[...], preferred_element_type=F32)
    b = jnp.dot(h, wu_ref[...], preferred_element_type=F32)
    t = (a * _sigmoid(a)) * b
    y = x + 0.5 * jnp.dot(t.astype(BF16), wd_ref[...], preferred_element_type=F32)
    if final:
        y = _rms(y, fg_ref[...])
    o_ref[...] = y


def _ffn(x, g, wg, wu, wd, fg, *, final, tm):
    t, d = x.shape
    f = wg.shape[1]
    resident = lambda r, c: pl.BlockSpec((r, c), lambda i: (0, 0), pipeline_mode=pl.Buffered(1))
    return pl.pallas_call(
        functools.partial(_ffn_kernel, final=final),
        grid=(t // tm,),
        in_specs=[
            pl.BlockSpec((tm, d), lambda i: (i, 0)),
            pl.BlockSpec((1, d), lambda i: (0, 0)),
            resident(d, f), resident(d, f), resident(f, d),
            pl.BlockSpec((1, d), lambda i: (0, 0)),
        ],
        out_specs=pl.BlockSpec((tm, d), lambda i: (i, 0)),
        out_shape=jax.ShapeDtypeStruct((t, d), F32),
        compiler_params=_cparams(("parallel",)),
        name="ffn_final" if final else "ffn",
    )(x, g, wg, wu, wd, fg)


CONV_HALO = 32
SC_HALO = 8


def _conv_mixers(cv, dw_ref, dwb_ref, lng_ref, lnb_ref, sw_ref, yab_ref, ha_scr, ub_scr, sh_scr, *, rc):
    ts = cv.shape[0]
    sub = SUBLANES
    first_tap = CONV_HALO - (CONV_W - 1)

    b_off = 2 * CONV_CH
    ha_scr[CONV_HALO:CONV_HALO + ts, :] = cv[:, 0:CONV_CH] * _sigmoid(cv[:, CONV_CH:2 * CONV_CH])
    ub_scr[SC_HALO:SC_HALO + ts, :] = cv[:, b_off + SC_CH:b_off + 2 * SC_CH] * cv[:, b_off + 2 * SC_CH:b_off + 3 * SC_CH]
    span = sh_scr.shape[1]
    for ph in range(1, sub):
        sh_scr[ph - 1] = ha_scr[ph:ph + span, :]

    for r in range(ts // rc):
        base = r * rc
        acc = jnp.zeros((rc, CONV_CH), F32)
        for j in range(CONV_W):
            off = base + first_tap + j
            ph = off % sub
            src = ha_scr[off:off + rc, :] if ph == 0 else sh_scr[ph - 1, off - ph:off - ph + rc, :]
            acc = acc + src * dw_ref[j:j + 1, :]
        acc = acc + dwb_ref[...]
        mu = jnp.mean(acc, axis=-1, keepdims=True)
        cen = acc - mu
        var = jnp.mean(cen * cen, axis=-1, keepdims=True)
        y = cen * lax.rsqrt(var + 1e-5) * lng_ref[...] + lnb_ref[...]
        yab_ref[base:base + rc, 0:CONV_CH] = (y * _sigmoid(y)).astype(BF16)

        accb = jnp.zeros((rc, SC_CH), F32)
        for j in range(SC_W):
            off = base + SC_HALO - (SC_W - 1) + j
            accb = accb + ub_scr[off:off + rc, :] * sw_ref[j:j + 1, :]
        yab_ref[base:base + rc, CONV_CH:CONV_CH + SC_CH] = (cv[base:base + rc, b_off:b_off + SC_CH] * accb).astype(BF16)

    ha_scr[0:CONV_HALO, :] = ha_scr[ts:ts + CONV_HALO, :]
    ub_scr[0:SC_HALO, :] = ub_scr[ts:ts + SC_HALO, :]


def _proj_kernel(x_ref, g_ref, w_ref, cos_ref, sin_ref, dw_ref, dwb_ref, lng_ref, lnb_ref, sw_ref,
                 yab_ref, qT_ref, k_ref, vT_ref, qiT_ref, kidx_ref, kwT_ref,
                 ha_scr, ub_scr, sh_scr, *, kc, rc):
    tm = x_ref.shape[0]

    @pl.when(pl.program_id(1) == 0)
    def _():
        ha_scr[0:CONV_HALO, :] = jnp.zeros((CONV_HALO, CONV_CH), F32)
        ub_scr[0:SC_HALO, :] = jnp.zeros((SC_HALO, SC_CH), F32)

    h = _rms(x_ref[...], g_ref[...]).astype(BF16)
    cos = cos_ref[...]
    sin = sin_ref[...]
    lane = lax.broadcasted_iota(I32, (tm, LANES), 1)
    first_half = (lane % HEAD_DIM) < (HEAD_DIM // 2)

    def rope(t, c, s):
        rot = jnp.where(first_half, pltpu.roll(t, LANES - HEAD_DIM // 2, 1), pltpu.roll(t, HEAD_DIM // 2, 1))
        return t * c + rot * s

    cv = jnp.dot(h, w_ref[:, 0:CONV_COLS], preferred_element_type=F32)
    _conv_mixers(cv, dw_ref, dwb_ref, lng_ref, lnb_ref, sw_ref, yab_ref, ha_scr, ub_scr, sh_scr, rc=rc)

    q = jnp.dot(h, w_ref[:, Q_OFF:Q_OFF + ATT_W], preferred_element_type=F32)
    for g in range(ATT_W // LANES):
        qr = rope(q[:, g * LANES:(g + 1) * LANES], cos, sin) * QK_SCALE
        qT_ref[g * LANES:(g + 1) * LANES, :] = qr.T.astype(BF16)

    k = jnp.dot(h, w_ref[:, K_OFF:K_OFF + ATT_W], preferred_element_type=F32)
    for g in range(ATT_W // LANES):
        k_ref[:, g * LANES:(g + 1) * LANES] = rope(k[:, g * LANES:(g + 1) * LANES], cos, sin).astype(BF16)

    v = jnp.dot(h, w_ref[:, V_OFF:V_OFF + ATT_W], preferred_element_type=F32)
    for c in range(tm // kc):
        vT_ref[c] = v[c * kc:(c + 1) * kc, :].T.astype(BF16)

    qi = jnp.dot(h, w_ref[:, QI_OFF:QI_OFF + IDX_HEADS * IDX_DIM], preferred_element_type=F32)
    for g in range(IDX_HEADS * IDX_DIM // LANES):
        qiT_ref[g * LANES:(g + 1) * LANES, :] = rope(qi[:, g * LANES:(g + 1) * LANES], cos, sin).T.astype(BF16)

    kw = jnp.dot(h, w_ref[:, KW_OFF:KW_OFF + LANES], preferred_element_type=F32)
    is_kidx = lane < IDX_DIM
    kw = rope(kw, jnp.where(is_kidx, cos, 1.0), jnp.where(is_kidx, sin, 0.0))
    kidx_ref[...] = kw.astype(BF16)
    kwT_ref[...] = kw.T


def _proj(x, g, w, cos, sin, dw, dwb, lng, lnb, sw, *, bsz, seq, tm, kc, rc):
    d = x.shape[1]
    ns = seq // tm
    tok = lambda b, s: (b * ns + s, 0)
    small = lambda r, c: pl.BlockSpec((r, c), lambda b, s: (0, 0))
    return pl.pallas_call(
        functools.partial(_proj_kernel, kc=kc, rc=rc),
        grid=(bsz, ns),
        in_specs=[
            pl.BlockSpec((tm, d), tok),
            small(1, d),
            small(d, N_IN_PAD),
            pl.BlockSpec((tm, LANES), tok),
            pl.BlockSpec((tm, LANES), tok),
            small(CONV_W, CONV_CH), small(1, CONV_CH), small(1, CONV_CH), small(1, CONV_CH),
            small(SC_W, SC_CH),
        ],
        out_specs=[
            pl.BlockSpec((None, tm, CONV_CH + SC_CH), lambda b, s: (b, s, 0)),
            pl.BlockSpec((None, ATT_W, tm), lambda b, s: (b, 0, s)),
            pl.BlockSpec((None, tm, ATT_W), lambda b, s: (b, s, 0)),
            pl.BlockSpec((None, tm // kc, ATT_W, kc), lambda b, s: (b, s, 0, 0)),
            pl.BlockSpec((None, IDX_HEADS * IDX_DIM, tm), lambda b, s: (b, 0, s)),
            pl.BlockSpec((None, tm, LANES), lambda b, s: (b, s, 0)),
            pl.BlockSpec((None, LANES, tm), lambda b, s: (b, 0, s)),
        ],
        out_shape=[
            jax.ShapeDtypeStruct((bsz, seq, CONV_CH + SC_CH), BF16),
            jax.ShapeDtypeStruct((bsz, ATT_W, seq), BF16),
            jax.ShapeDtypeStruct((bsz, seq, ATT_W), BF16),
            jax.ShapeDtypeStruct((bsz, seq // kc, ATT_W, kc), BF16),
            jax.ShapeDtypeStruct((bsz, IDX_HEADS * IDX_DIM, seq), BF16),
            jax.ShapeDtypeStruct((bsz, seq, LANES), BF16),
            jax.ShapeDtypeStruct((bsz, LANES, seq), F32),
        ],
        scratch_shapes=[pltpu.VMEM((tm + CONV_HALO, CONV_CH), F32),
                        pltpu.VMEM((tm + SC_HALO, SC_CH), F32),
                        pltpu.VMEM((SUBLANES - 1, tm + CONV_HALO - SUBLANES, CONV_CH), F32)],
        compiler_params=_cparams(("arbitrary", "arbitrary")),
        name="in_proj",
    )(x, g, w, cos, sin, dw, dwb, lng, lnb, sw)


def _sortable(x):
    b = lax.bitcast_convert_type(x, I32)
    return b ^ ((b >> 31) & INT_MAX)


def _dsa_kernel(qT_ref, qiT_ref, wT_ref, k_ref, vT_ref, kidx_ref, o_ref,
                key_scr, k16_scr, oT_scr, qpad_scr, m_scr, l_scr, *, kc, topk):
    qb = qT_ref.shape[1]
    i = pl.program_id(1)
    nk = (i * qb) // kc + 1
    q_pos = i * qb + lax.broadcasted_iota(I32, (1, qb), 1)
    row_iota = lax.broadcasted_iota(I32, (kc, qb), 0)

    w = wT_ref[...] * IDX_SCALE
    zeros_half = jnp.zeros((LANES - IDX_DIM, qb), BF16)
    qi = [jnp.concatenate([qiT_ref[h * IDX_DIM:(h + 1) * IDX_DIM, :], zeros_half], axis=0)
          for h in range(IDX_HEADS)]

    def score_chunk(c, diagonal):
        kcs = kidx_ref[pl.ds(pl.multiple_of(c * kc, kc), kc), :]
        sc = jnp.zeros((kc, qb), F32)
        for h in range(IDX_HEADS):
            lg = jnp.dot(kcs, qi[h], preferred_element_type=F32)
            sc = sc + w[h:h + 1, :] * jnp.maximum(lg, 0.0)
        if diagonal:
            sc = jnp.where((c * kc + row_iota) <= q_pos, sc, -jnp.inf)
        key = _sortable(sc)
        key_scr[c] = key
        k16_scr[c] = (key >> 16).astype(I16)

    def score_body(c, carry):
        score_chunk(c, False)
        return carry

    lax.fori_loop(0, nk - 1, score_body, 0)
    score_chunk(nk - 1, True)

    fold = 16 * 2 * SUBLANES * LANES // qb

    def count16(mid):
        midb = jnp.broadcast_to(mid, (fold, qb)).astype(I16)

        def body(c, acc):
            for r in range(kc // fold):
                acc = acc + jnp.where(k16_scr[c, r * fold:(r + 1) * fold, :] >= midb, jnp.int16(1), jnp.int16(0))
            return acc

        acc = lax.fori_loop(0, nk, body, jnp.zeros((fold, qb), I16))
        return jnp.sum(acc.astype(I32), axis=0, keepdims=True)

    def bisect16(want, cnt_all):
        def step(_, carry):
            lo, hi, cnt_lo, cnt_hi = carry
            mid = (lo + hi) >> 1
            cnt = jnp.where(mid == -HALF, cnt_all, count16(mid))
            ok = cnt >= want
            return (jnp.where(ok, mid, lo), jnp.where(ok, hi, mid),
                    jnp.where(ok, cnt, cnt_lo), jnp.where(ok, cnt_hi, cnt))
        zero = jnp.zeros((1, qb), I32)
        t, _, cnt_t, cnt_above = lax.fori_loop(0, 16, step, (zero - HALF, zero + HALF, cnt_all, zero))
        return t, cnt_t, cnt_above

    n_stored = nk * kc + jnp.zeros((1, qb), I32)
    t_hi, cge_hi, cgt_hi = bisect16(topk, n_stored)

    def low_body(c, carry):
        kk = key_scr[c]
        low = (kk & (2 * HALF - 1)) - HALF
        k16_scr[c] = jnp.where((kk >> 16) == t_hi, low, -HALF).astype(I16)
        return carry

    lax.fori_loop(0, nk, low_body, 0)
    t_lo, cge_lo, cgt_lo = bisect16(topk - cgt_hi, cge_hi - cgt_hi)

    tau = t_hi * (2 * HALF) + (t_lo + HALF)
    full = tau > NEG_INF_KEY
    tau_eff = jnp.maximum(tau, NEG_INF_KEY)
    need = jnp.where(full, topk - (cgt_hi + cgt_lo), 0).astype(F32)
    ltri = jnp.where(lax.broadcasted_iota(I32, (kc, kc), 0) >= lax.broadcasted_iota(I32, (kc, kc), 1),
                     1.0, 0.0).astype(BF16)

    zeros_head = jnp.zeros((HEAD_DIM, qb), BF16)
    for h in range(N_HEADS):
        qh = qT_ref[h * HEAD_DIM:(h + 1) * HEAD_DIM, :]
        qpad_scr[h] = jnp.concatenate([qh, zeros_head] if h % 2 == 0 else [zeros_head, qh], axis=0)
    m_scr[...] = jnp.full((N_HEADS, qb), NEG_BIG, F32)
    l_scr[...] = jnp.zeros((N_HEADS, qb), F32)
    oT_scr[...] = jnp.zeros((ATT_W, qb), F32)

    ones_rows = jnp.ones((16, kc), BF16)

    def att_body(c, run):
        kk = key_scr[c]
        eq = kk == tau_eff
        rank = jnp.dot(ltri, jnp.where(eq, 1.0, 0.0).astype(BF16), preferred_element_type=F32)
        tie = jnp.where(rank <= need - run, 0.0, -jnp.inf)
        bias = jnp.where(kk > tau_eff, 0.0, jnp.where(eq, tie, -jnp.inf))
        row0 = pl.multiple_of(c * kc, kc)
        m_old = m_scr[...]
        l_old = l_scr[...]
        s_all = [jnp.dot(k_ref[pl.ds(row0, kc), (h // 2) * LANES:(h // 2 + 1) * LANES], qpad_scr[h],
                         preferred_element_type=F32) + bias for h in range(N_HEADS)]
        m_new, alphas, p_all = [], [], []
        for h in range(N_HEADS):
            m = m_old[h:h + 1, :]
            mn = jnp.maximum(m, jnp.max(s_all[h], axis=0, keepdims=True))
            p_all.append(jnp.exp2(s_all[h] - mn).astype(BF16))
            alphas.append(jnp.exp2(m - mn))
            m_new.append(mn)
        pv = [jnp.dot(jnp.concatenate([vT_ref[c, h * HEAD_DIM:(h + 1) * HEAD_DIM, :], ones_rows], axis=0),
                      p_all[h], preferred_element_type=F32) for h in range(N_HEADS)]
        alpha_rows = [jnp.broadcast_to(a, (HEAD_DIM, qb)) for a in alphas]
        oT_scr[...] = (jnp.concatenate(alpha_rows, axis=0) * oT_scr[...]
                       + jnp.concatenate([x[0:HEAD_DIM] for x in pv], axis=0))
        m_scr[...] = jnp.concatenate(m_new, axis=0)
        l_scr[...] = (jnp.concatenate(alphas, axis=0) * l_old
                      + jnp.concatenate([x[HEAD_DIM:HEAD_DIM + 1] for x in pv], axis=0))
        return run + rank[kc - 1:kc, :]

    lax.fori_loop(0, nk, att_body, jnp.zeros((1, qb), F32))
    for h in range(N_HEADS):
        hs = slice(h * HEAD_DIM, (h + 1) * HEAD_DIM)
        oT_scr[hs, :] = oT_scr[hs, :] / l_scr[h:h + 1, :]

    o_ref[...] = oT_scr[...].T.astype(BF16)


def _dsa(qT, qiT, kwT, k, vT, kidx, *, qb, kc):
    bsz, seq, _ = k.shape
    assert kc % qb == 0 and seq % kc == 0
    topk = min(TOPK_MAX, seq // 4)
    w_row_block = IDX_DIM // 8
    return pl.pallas_call(
        functools.partial(_dsa_kernel, kc=kc, topk=topk),
        grid=(bsz, seq // qb),
        in_specs=[
            pl.BlockSpec((None, ATT_W, qb), lambda b, i: (b, 0, i)),
            pl.BlockSpec((None, IDX_HEADS * IDX_DIM, qb), lambda b, i: (b, 0, i)),
            pl.BlockSpec((None, 8, qb), lambda b, i: (b, w_row_block, i)),
            pl.BlockSpec((None, seq, ATT_W), lambda b, i: (b, 0, 0), pipeline_mode=pl.Buffered(1)),
            pl.BlockSpec((None, seq // kc, ATT_W, kc), lambda b, i: (b, 0, 0, 0), pipeline_mode=pl.Buffered(1)),
            pl.BlockSpec((None, seq, LANES), lambda b, i: (b, 0, 0), pipeline_mode=pl.Buffered(1)),
        ],
        out_specs=pl.BlockSpec((None, qb, ATT_W), lambda b, i: (b, i, 0)),
        out_shape=jax.ShapeDtypeStruct((bsz, seq, ATT_W), BF16),
        scratch_shapes=[pltpu.VMEM((seq // kc, kc, qb), I32),
                        pltpu.VMEM((seq // kc, kc, qb), I16),
                        pltpu.VMEM((ATT_W, qb), F32),
                        pltpu.VMEM((N_HEADS, LANES, qb), BF16),
                        pltpu.VMEM((N_HEADS, qb), F32),
                        pltpu.VMEM((N_HEADS, qb), F32)],
        compiler_params=_cparams(("parallel", "parallel")),
        name="dsa_attention",
    )(qT, qiT, kwT, k, vT, kidx)


def _memkv_kernel(mem_ref, g_ref, wkv_ref, kT_ref, v_ref):
    d = mem_ref.shape[1]
    mn = _rms(mem_ref[...], g_ref[...]).astype(BF16)
    kv = jnp.dot(mn, wkv_ref[...], preferred_element_type=F32)
    kT_ref[...] = kv[:, 0:d].T.astype(BF16)
    v_ref[...] = kv[:, d:2 * d].astype(BF16)


def _memkv(mem, g, wkv):
    bsz, m, d = mem.shape
    return pl.pallas_call(
        _memkv_kernel,
        grid=(bsz,),
        in_specs=[pl.BlockSpec((None, m, d), lambda b: (b, 0, 0)),
                  pl.BlockSpec((1, d), lambda b: (0, 0)),
                  pl.BlockSpec((d, 2 * d), lambda b: (0, 0))],
        out_specs=[pl.BlockSpec((None, d, m), lambda b: (b, 0, 0)),
                   pl.BlockSpec((None, m, d), lambda b: (b, 0, 0))],
        out_shape=[jax.ShapeDtypeStruct((bsz, d, m), BF16), jax.ShapeDtypeStruct((bsz, m, d), BF16)],
        compiler_params=_cparams(("parallel",)),
        name="mem_kv",
    )(mem, g, wkv)


def _xattn_kernel(x_ref, yab_ref, yc_ref, wab_ref, wc_ref, g_ref, wq_ref, kT_ref, v_ref, wo_ref, o_ref):
    d = x_ref.shape[1]
    hd = d // XA_HEADS
    x = (x_ref[...]
         + jnp.dot(yab_ref[...], wab_ref[...], preferred_element_type=F32)
         + jnp.dot(yc_ref[...], wc_ref[...], preferred_element_type=F32))
    hq = _rms(x, g_ref[...]).astype(BF16)
    q = (jnp.dot(hq, wq_ref[...], preferred_element_type=F32) * (hd ** -0.5)).astype(BF16)
    outs = []
    for h in range(XA_HEADS):
        s = jnp.dot(q[:, h * hd:(h + 1) * hd], kT_ref[h * hd:(h + 1) * hd, :], preferred_element_type=F32)
        m = jnp.max(s, axis=-1, keepdims=True)
        p = jnp.exp(s - m)
        l = jnp.sum(p, axis=-1, keepdims=True)
        o = jnp.dot(p.astype(BF16), v_ref[:, h * hd:(h + 1) * hd], preferred_element_type=F32) / l
        outs.append(o.astype(BF16))
    o_ref[...] = x + jnp.dot(jnp.concatenate(outs, axis=-1), wo_ref[...], preferred_element_type=F32)


def _xattn(x, yab, yc, wab, wc, g, wq, kT, v, wo, *, bsz, seq, tm):
    d = x.shape[1]
    m = v.shape[1]
    ns = seq // tm
    tok_c = lambda c: pl.BlockSpec((tm, c), lambda b, s: (b * ns + s, 0))
    tok = tok_c(d)
    full = lambda a: pl.BlockSpec(a.shape, lambda b, s: (0, 0))
    return pl.pallas_call(
        _xattn_kernel,
        grid=(bsz, ns),
        in_specs=[tok, tok_c(yab.shape[1]), tok_c(yc.shape[1]), full(wab), full(wc),
                  pl.BlockSpec((1, d), lambda b, s: (0, 0)),
                  pl.BlockSpec((d, d), lambda b, s: (0, 0)),
                  pl.BlockSpec((None, d, m), lambda b, s: (b, 0, 0)),
                  pl.BlockSpec((None, m, d), lambda b, s: (b, 0, 0)),
                  pl.BlockSpec((d, d), lambda b, s: (0, 0))],
        out_specs=tok,
        out_shape=jax.ShapeDtypeStruct((bsz * seq, d), F32),
        compiler_params=_cparams(("parallel", "parallel")),
        name="mem_xattn",
    )(x, yab, yc, wab, wc, g, wq, kT, v, wo)


def _tiles(seq):
    assert seq % 512 == 0, "sequence length must be a multiple of 512"
    return dict(
        tm_ffn=512,
        tm_proj=512, rc_conv=64,
        qb=512, kc=512,
        tm_xa=512, tm_rope=1024,
    )


def kernel(x, mem, positions, ffn1_norm, ffn1_w_gate, ffn1_w_up, ffn1_w_down, mix_norm, w_in,
           conf_dw, conf_dw_b, conf_ln_g, conf_ln_b, sc_dw, w_out, xa_norm, mem_norm,
           xa_wq, xa_wkv, xa_wo, ffn2_norm, ffn2_w_gate, ffn2_w_up, ffn2_w_down, final_norm):
    bsz, seq, d = x.shape
    depth = w_in.shape[0]
    t = bsz * seq
    tl = _tiles(seq)
    bf = lambda a: a.astype(BF16)
    row = lambda a: a.reshape(1, -1)

    cos, sin = _rope_tables(positions, tl["tm_rope"])
    xf = x.reshape(t, d)
    w_in_pad = jnp.pad(w_in, ((0, 0), (0, 0), (0, N_IN_PAD - w_in.shape[2])))

    for l in range(depth):
        xf = _ffn(xf, row(ffn1_norm[l]), bf(ffn1_w_gate[l]), bf(ffn1_w_up[l]), bf(ffn1_w_down[l]),
                  row(final_norm), final=False, tm=tl["tm_ffn"])

        yab, qT, k, vT, qiT, kidx, kwT = _proj(
            xf, row(mix_norm[l]), bf(w_in_pad[l]), cos, sin,
            conf_dw[l], row(conf_dw_b[l]), row(conf_ln_g[l]), row(conf_ln_b[l]), sc_dw[l],
            bsz=bsz, seq=seq, tm=tl["tm_proj"], kc=tl["kc"], rc=tl["rc_conv"])
        yc = _dsa(qT, qiT, kwT, k, vT, kidx, qb=tl["qb"], kc=tl["kc"])

        n_ab = CONV_CH + SC_CH
        kT_mem, v_mem = _memkv(mem, row(mem_norm[l]), bf(xa_wkv[l]))
        xf = _xattn(xf, yab.reshape(t, n_ab), yc.reshape(t, ATT_W), bf(w_out[l, 0:n_ab]), bf(w_out[l, n_ab:]),
                    row(xa_norm[l]), bf(xa_wq[l]), kT_mem, v_mem, bf(xa_wo[l]),
                    bsz=bsz, seq=seq, tm=tl["tm_xa"])

        xf = _ffn(xf, row(ffn2_norm[l]), bf(ffn2_w_gate[l]), bf(ffn2_w_up[l]), bf(ffn2_w_down[l]),
                  row(final_norm), final=(l == depth - 1), tm=tl["tm_ffn"])

    return xf.reshape(bsz, seq, d)
```

```python
import functools
import math

import jax
import jax.numpy as jnp
from jax import lax
from jax.experimental import pallas as pl
from jax.experimental.pallas import tpu as pltpu

F32 = jnp.float32
BF16 = jnp.bfloat16
I32 = jnp.int32
I16 = jnp.int16

CONV_CH = 256
CONV_W = 31
SC_CH = 256
SC_W = 3
N_HEADS = 8
HEAD_DIM = 64
ATT_W = N_HEADS * HEAD_DIM
IDX_HEADS = 4
IDX_DIM = 64
TOPK_MAX = 256
ROPE_THETA = 10000.0
XA_HEADS = 4
IDX_SCALE = (IDX_HEADS ** -0.5) * (IDX_DIM ** -0.5)
ATT_SCALE = HEAD_DIM ** -0.5
QK_SCALE = ATT_SCALE * math.log2(math.e)

LANES = 128
SUBLANES = 8
VMEM_LIMIT = 58 * 1024 * 1024

CONV_COLS = 2 * CONV_CH + 3 * SC_CH
Q_OFF = CONV_COLS
K_OFF = Q_OFF + ATT_W
V_OFF = K_OFF + ATT_W
QI_OFF = V_OFF + ATT_W
KW_OFF = QI_OFF + IDX_HEADS * IDX_DIM
N_IN_PAD = KW_OFF + LANES

HALF = 2 ** 15
INT_MAX = 2 ** 31 - 1
NEG_INF_KEY = -2139095041
NEG_BIG = -1e30


def _cparams(sem, vmem=VMEM_LIMIT):
    return pltpu.CompilerParams(dimension_semantics=sem, vmem_limit_bytes=vmem)


def _rms(x, g, eps=1e-6):
    return x * lax.rsqrt(jnp.mean(x * x, axis=-1, keepdims=True) + eps) * g


def _sigmoid(x):
    return 1.0 / (1.0 + jnp.exp(-x))


def _rope_kernel(pos_ref, invf_ref, sgn_ref, cos_ref, sin_ref):
    ang = pos_ref[...].astype(F32) * invf_ref[...]
    cos_ref[...] = jnp.cos(ang)
    sin_ref[...] = jnp.sin(ang) * sgn_ref[...]


def _rope_tables(positions, tm):
    t = positions.size
    half = HEAD_DIM // 2
    inv_freq = ROPE_THETA ** (-jnp.arange(0, HEAD_DIM, 2, dtype=F32) / HEAD_DIM)
    invf = jnp.tile(inv_freq, LANES // half)[None, :]
    sgn = jnp.tile(jnp.concatenate([-jnp.ones((half,), F32), jnp.ones((half,), F32)]),
                   LANES // HEAD_DIM)[None, :]
    pos = jnp.broadcast_to(positions.reshape(t, 1), (t, LANES))
    row = pl.BlockSpec((tm, LANES), lambda i: (i, 0))
    one = pl.BlockSpec((1, LANES), lambda i: (0, 0))
    return pl.pallas_call(
        _rope_kernel,
        grid=(t // tm,),
        in_specs=[row, one, one],
        out_specs=[row, row],
        out_shape=[jax.ShapeDtypeStruct((t, LANES), F32)] * 2,
        compiler_params=_cparams(("parallel",)),
        name="rope_tables",
    )(pos, invf, sgn)


def _ffn_kernel(x_ref, g_ref, wg_ref, wu_ref, wd_ref, fg_ref, o_ref, *, final):
    x = x_ref[...]
    h = _rms(x, g_ref[...]).astype(BF16)
    a = jnp.dot(h, wg_ref[...], preferred_element_type=F32)
    b = jnp.dot(h, wu_ref[...], preferred_element_type=F32)
    t = (a * _sigmoid(a)) * b
    y = x + 0.5 * jnp.dot(t.astype(BF16), wd_ref[...], preferred_element_type=F32)
    if final:
        y = _rms(y, fg_ref[...])
    o_ref[...] = y


def _ffn(x, g, wg, wu, wd, fg, *, final, tm):
    t, d = x.shape
    f = wg.shape[1]
    resident = lambda r, c: pl.BlockSpec((r, c), lambda i: (0, 0), pipeline_mode=pl.Buffered(1))
    return pl.pallas_call(
        functools.partial(_ffn_kernel, final=final),
        grid=(t // tm,),
        in_specs=[
            pl.BlockSpec((tm, d), lambda i: (i, 0)),
            pl.BlockSpec((1, d), lambda i: (0, 0)),
            resident(d, f), resident(d, f), resident(f, d),
            pl.BlockSpec((1, d), lambda i: (0, 0)),
        ],
        out_specs=pl.BlockSpec((tm, d), lambda i: (i, 0)),
        out_shape=jax.ShapeDtypeStruct((t, d), F32),
        compiler_params=_cparams(("parallel",)),
        name="ffn_final" if final else "ffn",
    )(x, g, wg, wu, wd, fg)


CONV_HALO = 32
SC_HALO = 8


def _conv_mixers(cv, dw_ref, dwb_ref, lng_ref, lnb_ref, sw_ref, yab_ref, ha_scr, ub_scr, sh_scr, *, rc):
    ts = cv.shape[0]
    sub = SUBLANES
    first_tap = CONV_HALO - (CONV_W - 1)

    b_off = 2 * CONV_CH
    ha_scr[CONV_HALO:CONV_HALO + ts, :] = cv[:, 0:CONV_CH] * _sigmoid(cv[:, CONV_CH:2 * CONV_CH])
    ub_scr[SC_HALO:SC_HALO + ts, :] = cv[:, b_off + SC_CH:b_off + 2 * SC_CH] * cv[:, b_off + 2 * SC_CH:b_off + 3 * SC_CH]
    span = sh_scr.shape[1]
    for ph in range(1, sub):
        sh_scr[ph - 1] = ha_scr[ph:ph + span, :]

    for r in range(ts // rc):
        base = r * rc
        acc = jnp.zeros((rc, CONV_CH), F32)
        for j in range(CONV_W):
            off = base + first_tap + j
            ph = off % sub
            src = ha_scr[off:off + rc, :] if ph == 0 else sh_scr[ph - 1, off - ph:off - ph + rc, :]
            acc = acc + src * dw_ref[j:j + 1, :]
        acc = acc + dwb_ref[...]
        mu = jnp.mean(acc, axis=-1, keepdims=True)
        cen = acc - mu
        var = jnp.mean(cen * cen, axis=-1, keepdims=True)
        y = cen * lax.rsqrt(var + 1e-5) * lng_ref[...] + lnb_ref[...]
        yab_ref[base:base + rc, 0:CONV_CH] = (y * _sigmoid(y)).astype(BF16)

        accb = jnp.zeros((rc, SC_CH), F32)
        for j in range(SC_W):
            off = base + SC_HALO - (SC_W - 1) + j
            accb = accb + ub_scr[off:off + rc, :] * sw_ref[j:j + 1, :]
        yab_ref[base:base + rc, CONV_CH:CONV_CH + SC_CH] = (cv[base:base + rc, b_off:b_off + SC_CH] * accb).astype(BF16)

    ha_scr[0:CONV_HALO, :] = ha_scr[ts:ts + CONV_HALO, :]
    ub_scr[0:SC_HALO, :] = ub_scr[ts:ts + SC_HALO, :]


def _proj_kernel(x_ref, g_ref, w_ref, cos_ref, sin_ref, dw_ref, dwb_ref, lng_ref, lnb_ref, sw_ref,
                 yab_ref, qT_ref, k_ref, vT_ref, qiT_ref, kidx_ref, kwT_ref,
                 ha_scr, ub_scr, sh_scr, *, kc, rc):
    tm = x_ref.shape[0]

    @pl.when(pl.program_id(1) == 0)
    def _():
        ha_scr[0:CONV_HALO, :] = jnp.zeros((CONV_HALO, CONV_CH), F32)
        ub_scr[0:SC_HALO, :] = jnp.zeros((SC_HALO, SC_CH), F32)

    h = _rms(x_ref[...], g_ref[...]).astype(BF16)
    cos = cos_ref[...]
    sin = sin_ref[...]
    lane = lax.broadcasted_iota(I32, (tm, LANES), 1)
    first_half = (lane % HEAD_DIM) < (HEAD_DIM // 2)

    def rope(t, c, s):
        rot = jnp.where(first_half, pltpu.roll(t, LANES - HEAD_DIM // 2, 1), pltpu.roll(t, HEAD_DIM // 2, 1))
        return t * c + rot * s

    cv = jnp.dot(h, w_ref[:, 0:CONV_COLS], preferred_element_type=F32)
    _conv_mixers(cv, dw_ref, dwb_ref, lng_ref, lnb_ref, sw_ref, yab_ref, ha_scr, ub_scr, sh_scr, rc=rc)

    q = jnp.dot(h, w_ref[:, Q_OFF:Q_OFF + ATT_W], preferred_element_type=F32)
    for g in range(ATT_W // LANES):
        qr = rope(q[:, g * LANES:(g + 1) * LANES], cos, sin) * QK_SCALE
        qT_ref[g * LANES:(g + 1) * LANES, :] = qr.T.astype(BF16)

    k = jnp.dot(h, w_ref[:, K_OFF:K_OFF + ATT_W], preferred_element_type=F32)
    for g in range(ATT_W // LANES):
        k_ref[:, g * LANES:(g + 1) * LANES] = rope(k[:, g * LANES:(g + 1) * LANES], cos, sin).astype(BF16)

    v = jnp.dot(h, w_ref[:, V_OFF:V_OFF + ATT_W], preferred_element_type=F32)
    for c in range(tm // kc):
        vT_ref[c] = v[c * kc:(c + 1) * kc, :].T.astype(BF16)

    qi = jnp.dot(h, w_ref[:, QI_OFF:QI_OFF + IDX_HEADS * IDX_DIM], preferred_element_type=F32)
    for g in range(IDX_HEADS * IDX_DIM // LANES):
        qiT_ref[g * LANES:(g + 1) * LANES, :] = rope(qi[:, g * LANES:(g + 1) * LANES], cos, sin).T.astype(BF16)

    kw = jnp.dot(h, w_ref[:, KW_OFF:KW_OFF + LANES], preferred_element_type=F32)
    is_kidx = lane < IDX_DIM
    kw = rope(kw, jnp.where(is_kidx, cos, 1.0), jnp.where(is_kidx, sin, 0.0))
    kidx_ref[...] = kw.astype(BF16)
    kwT_ref[...] = kw.T


def _proj(x, g, w, cos, sin, dw, dwb, lng, lnb, sw, *, bsz, seq, tm, kc, rc):
    d = x.shape[1]
    ns = seq // tm
    tok = lambda b, s: (b * ns + s, 0)
    small = lambda r, c: pl.BlockSpec((r, c), lambda b, s: (0, 0))
    return pl.pallas_call(
        functools.partial(_proj_kernel, kc=kc, rc=rc),
        grid=(bsz, ns),
        in_specs=[
            pl.BlockSpec((tm, d), tok),
            small(1, d),
            small(d, N_IN_PAD),
            pl.BlockSpec((tm, LANES), tok),
            pl.BlockSpec((tm, LANES), tok),
            small(CONV_W, CONV_CH), small(1, CONV_CH), small(1, CONV_CH), small(1, CONV_CH),
            small(SC_W, SC_CH),
        ],
        out_specs=[
            pl.BlockSpec((None, tm, CONV_CH + SC_CH), lambda b, s: (b, s, 0)),
            pl.BlockSpec((None, ATT_W, tm), lambda b, s: (b, 0, s)),
            pl.BlockSpec((None, tm, ATT_W), lambda b, s: (b, s, 0)),
            pl.BlockSpec((None, tm // kc, ATT_W, kc), lambda b, s: (b, s, 0, 0)),
            pl.BlockSpec((None, IDX_HEADS * IDX_DIM, tm), lambda b, s: (b, 0, s)),
            pl.BlockSpec((None, tm, LANES), lambda b, s: (b, s, 0)),
            pl.BlockSpec((None, LANES, tm), lambda b, s: (b, 0, s)),
        ],
        out_shape=[
            jax.ShapeDtypeStruct((bsz, seq, CONV_CH + SC_CH), BF16),
            jax.ShapeDtypeStruct((bsz, ATT_W, seq), BF16),
            jax.ShapeDtypeStruct((bsz, seq, ATT_W), BF16),
            jax.ShapeDtypeStruct((bsz, seq // kc, ATT_W, kc), BF16),
            jax.ShapeDtypeStruct((bsz, IDX_HEADS * IDX_DIM, seq), BF16),
            jax.ShapeDtypeStruct((bsz, seq, LANES), BF16),
            jax.ShapeDtypeStruct((bsz, LANES, seq), F32),
        ],
        scratch_shapes=[pltpu.VMEM((tm + CONV_HALO, CONV_CH), F32),
                        pltpu.VMEM((tm + SC_HALO, SC_CH), F32),
                        pltpu.VMEM((SUBLANES - 1, tm + CONV_HALO - SUBLANES, CONV_CH), F32)],
        compiler_params=_cparams(("arbitrary", "arbitrary")),
        name="in_proj",
    )(x, g, w, cos, sin, dw, dwb, lng, lnb, sw)


def _sortable(x):
    b = lax.bitcast_convert_type(x, I32)
    return b ^ ((b >> 31) & INT_MAX)


def _dsa_kernel(qT_ref, qiT_ref, wT_ref, k_ref, vT_ref, kidx_ref, o_ref,
                key_scr, k16_scr, oT_scr, qpad_scr, m_scr, l_scr, *, kc, topk):
    qb = qT_ref.shape[1]
    i = pl.program_id(1)
    nk = (i * qb) // kc + 1
    q_pos = i * qb + lax.broadcasted_iota(I32, (1, qb), 1)
    row_iota = lax.broadcasted_iota(I32, (kc, qb), 0)

    w = wT_ref[...] * IDX_SCALE
    zeros_half = jnp.zeros((LANES - IDX_DIM, qb), BF16)
    qi = [jnp.concatenate([qiT_ref[h * IDX_DIM:(h + 1) * IDX_DIM, :], zeros_half], axis=0)
          for h in range(IDX_HEADS)]

    def score_chunk(c, diagonal):
        kcs = kidx_ref[pl.ds(pl.multiple_of(c * kc, kc), kc), :]
        sc = jnp.zeros((kc, qb), F32)
        for h in range(IDX_HEADS):
            lg = jnp.dot(kcs, qi[h], preferred_element_type=F32)
            sc = sc + w[h:h + 1, :] * jnp.maximum(lg, 0.0)
        if diagonal:
            sc = jnp.where((c * kc + row_iota) <= q_pos, sc, -jnp.inf)
        key = _sortable(sc)
        key_scr[c] = key
        k16_scr[c] = (key >> 16).astype(I16)

    def score_body(c, carry):
        score_chunk(c, False)
        return carry

    lax.fori_loop(0, nk - 1, score_body, 0)
    score_chunk(nk - 1, True)

    fold = 16 * 2 * SUBLANES * LANES // qb

    def count16(mid):
        midb = jnp.broadcast_to(mid, (fold, qb)).astype(I16)

        def body(c, acc):
            for r in range(kc // fold):
                acc = acc + jnp.where(k16_scr[c, r * fold:(r + 1) * fold, :] >= midb, jnp.int16(1), jnp.int16(0))
            return acc

        acc = lax.fori_loop(0, nk, body, jnp.zeros((fold, qb), I16))
        return jnp.sum(acc.astype(I32), axis=0, keepdims=True)

    def bisect16(want, cnt_all):
        def step(_, carry):
            lo, hi, cnt_lo, cnt_hi = carry
            mid = (lo + hi) >> 1
            cnt = jnp.where(mid == -HALF, cnt_all, count16(mid))
            ok = cnt >= want
            return (jnp.where(ok, mid, lo), jnp.where(ok, hi, mid),
                    jnp.where(ok, cnt, cnt_lo), jnp.where(ok, cnt_hi, cnt))
        zero = jnp.zeros((1, qb), I32)
        t, _, cnt_t, cnt_above = lax.fori_loop(0, 16, step, (zero - HALF, zero + HALF, cnt_all, zero))
        return t, cnt_t, cnt_above

    n_stored = nk * kc + jnp.zeros((1, qb), I32)
    t_hi, cge_hi, cgt_hi = bisect16(topk, n_stored)

    def low_body(c, carry):
        kk = key_scr[c]
        low = (kk & (2 * HALF - 1)) - HALF
        k16_scr[c] = jnp.where((kk >> 16) == t_hi, low, -HALF).astype(I16)
        return carry

    lax.fori_loop(0, nk, low_body, 0)
    t_lo, cge_lo, cgt_lo = bisect16(topk - cgt_hi, cge_hi - cgt_hi)

    tau = t_hi * (2 * HALF) + (t_lo + HALF)
    full = tau > NEG_INF_KEY
    tau_eff = jnp.maximum(tau, NEG_INF_KEY)
    need = jnp.where(full, topk - (cgt_hi + cgt_lo), 0).astype(F32)
    ltri = jnp.where(lax.broadcasted_iota(I32, (kc, kc), 0) >= lax.broadcasted_iota(I32, (kc, kc), 1),
                     1.0, 0.0).astype(BF16)

    zeros_head = jnp.zeros((HEAD_DIM, qb), BF16)
    for h in range(N_HEADS):
        qh = qT_ref[h * HEAD_DIM:(h + 1) * HEAD_DIM, :]
        qpad_scr[h] = jnp.concatenate([qh, zeros_head] if h % 2 == 0 else [zeros_head, qh], axis=0)
    m_scr[...] = jnp.full((N_HEADS, qb), NEG_BIG, F32)
    l_scr[...] = jnp.zeros((N_HEADS, qb), F32)
    oT_scr[...] = jnp.zeros((ATT_W, qb), F32)

    ones_rows = jnp.ones((16, kc), BF16)

    def att_body(c, run):
        kk = key_scr[c]
        eq = kk == tau_eff
        rank = jnp.dot(ltri, jnp.where(eq, 1.0, 0.0).astype(BF16), preferred_element_type=F32)
        tie = jnp.where(rank <= need - run, 0.0, -jnp.inf)
        bias = jnp.where(kk > tau_eff, 0.0, jnp.where(eq, tie, -jnp.inf))
        row0 = pl.multiple_of(c * kc, kc)
        m_old = m_scr[...]
        l_old = l_scr[...]
        s_all = [jnp.dot(k_ref[pl.ds(row0, kc), (h // 2) * LANES:(h // 2 + 1) * LANES], qpad_scr[h],
                         preferred_element_type=F32) + bias for h in range(N_HEADS)]
        m_new, alphas, p_all = [], [], []
        for h in range(N_HEADS):
            m = m_old[h:h + 1, :]
            mn = jnp.maximum(m, jnp.max(s_all[h], axis=0, keepdims=True))
            p_all.append(jnp.exp2(s_all[h] - mn).astype(BF16))
            alphas.append(jnp.exp2(m - mn))
            m_new.append(mn)
        pv = [jnp.dot(jnp.concatenate([vT_ref[c, h * HEAD_DIM:(h + 1) * HEAD_DIM, :], ones_rows], axis=0),
                      p_all[h], preferred_element_type=F32) for h in range(N_HEADS)]
        alpha_rows = [jnp.broadcast_to(a, (HEAD_DIM, qb)) for a in alphas]
        oT_scr[...] = (jnp.concatenate(alpha_rows, axis=0) * oT_scr[...]
                       + jnp.concatenate([x[0:HEAD_DIM] for x in pv], axis=0))
        m_scr[...] = jnp.concatenate(m_new, axis=0)
        l_scr[...] = (jnp.concatenate(alphas, axis=0) * l_old
                      + jnp.concatenate([x[HEAD_DIM:HEAD_DIM + 1] for x in pv], axis=0))
        return run + rank[kc - 1:kc, :]

    lax.fori_loop(0, nk, att_body, jnp.zeros((1, qb), F32))
    for h in range(N_HEADS):
        hs = slice(h * HEAD_DIM, (h + 1) * HEAD_DIM)
        oT_scr[hs, :] = oT_scr[hs, :] / l_scr[h:h + 1, :]

    o_ref[...] = oT_scr[...].T.astype(BF16)


def _dsa(qT, qiT, kwT, k, vT, kidx, *, qb, kc):
    bsz, seq, _ = k.shape
    assert kc % qb == 0 and seq % kc == 0
    topk = min(TOPK_MAX, seq // 4)
    w_row_block = IDX_DIM // 8
    return pl.pallas_call(
        functools.partial(_dsa_kernel, kc=kc, topk=topk),
        grid=(bsz, seq // qb),
        in_specs=[
            pl.BlockSpec((None, ATT_W, qb), lambda b, i: (b, 0, i)),
            pl.BlockSpec((None, IDX_HEADS * IDX_DIM, qb), lambda b, i: (b, 0, i)),
            pl.BlockSpec((None, 8, qb), lambda b, i: (b, w_row_block, i)),
            pl.BlockSpec((None, seq, ATT_W), lambda b, i: (b, 0, 0), pipeline_mode=pl.Buffered(1)),
            pl.BlockSpec((None, seq // kc, ATT_W, kc), lambda b, i: (b, 0, 0, 0), pipeline_mode=pl.Buffered(1)),
            pl.BlockSpec((None, seq, LANES), lambda b, i: (b, 0, 0), pipeline_mode=pl.Buffered(1)),
        ],
        out_specs=pl.BlockSpec((None, qb, ATT_W), lambda b, i: (b, i, 0)),
        out_shape=jax.ShapeDtypeStruct((bsz, seq, ATT_W), BF16),
        scratch_shapes=[pltpu.VMEM((seq // kc, kc, qb), I32),
                        pltpu.VMEM((seq // kc, kc, qb), I16),
                        pltpu.VMEM((ATT_W, qb), F32),
                        pltpu.VMEM((N_HEADS, LANES, qb), BF16),
                        pltpu.VMEM((N_HEADS, qb), F32),
                        pltpu.VMEM((N_HEADS, qb), F32)],
        compiler_params=_cparams(("parallel", "parallel")),
        name="dsa_attention",
    )(qT, qiT, kwT, k, vT, kidx)


def _memkv_kernel(mem_ref, g_ref, wkv_ref, kT_ref, v_ref):
    d = mem_ref.shape[1]
    mn = _rms(mem_ref[...], g_ref[...]).astype(BF16)
    kv = jnp.dot(mn, wkv_ref[...], preferred_element_type=F32)
    kT_ref[...] = kv[:, 0:d].T.astype(BF16)
    v_ref[...] = kv[:, d:2 * d].astype(BF16)


def _memkv(mem, g, wkv):
    bsz, m, d = mem.shape
    return pl.pallas_call(
        _memkv_kernel,
        grid=(bsz,),
        in_specs=[pl.BlockSpec((None, m, d), lambda b: (b, 0, 0)),
                  pl.BlockSpec((1, d), lambda b: (0, 0)),
                  pl.BlockSpec((d, 2 * d), lambda b: (0, 0))],
        out_specs=[pl.BlockSpec((None, d, m), lambda b: (b, 0, 0)),
                   pl.BlockSpec((None, m, d), lambda b: (b, 0, 0))],
        out_shape=[jax.ShapeDtypeStruct((bsz, d, m), BF16), jax.ShapeDtypeStruct((bsz, m, d), BF16)],
        compiler_params=_cparams(("parallel",)),
        name="mem_kv",
    )(mem, g, wkv)


def _xattn_kernel(x_ref, yab_ref, yc_ref, wmix_ref, g_ref, wq_ref, kT_ref, v_ref, wo_ref, o_ref):
    d = x_ref.shape[1]
    hd = d // XA_HEADS
    n_ab = yab_ref.shape[1]
    x = (x_ref[...]
         + jnp.dot(yab_ref[...], wmix_ref[0:n_ab, :], preferred_element_type=F32)
         + jnp.dot(yc_ref[...], wmix_ref[n_ab:, :], preferred_element_type=F32))
    hq = _rms(x, g_ref[...]).astype(BF16)
    q = (jnp.dot(hq, wq_ref[...], preferred_element_type=F32) * (hd ** -0.5)).astype(BF16)
    outs = []
    for h in range(XA_HEADS):
        s = jnp.dot(q[:, h * hd:(h + 1) * hd], kT_ref[h * hd:(h + 1) * hd, :], preferred_element_type=F32)
        m = jnp.max(s, axis=-1, keepdims=True)
        p = jnp.exp(s - m)
        l = jnp.sum(p, axis=-1, keepdims=True)
        o = jnp.dot(p.astype(BF16), v_ref[:, h * hd:(h + 1) * hd], preferred_element_type=F32) / l
        outs.append(o.astype(BF16))
    o_ref[...] = x + jnp.dot(jnp.concatenate(outs, axis=-1), wo_ref[...], preferred_element_type=F32)


def _xattn(x, yab, yc, wmix, g, wq, kT, v, wo, *, bsz, seq, tm):
    d = x.shape[1]
    m = v.shape[1]
    ns = seq // tm
    tok_c = lambda c: pl.BlockSpec((tm, c), lambda b, s: (b * ns + s, 0))
    tok = tok_c(d)
    full = lambda a: pl.BlockSpec(a.shape, lambda b, s: (0, 0))
    return pl.pallas_call(
        _xattn_kernel,
        grid=(bsz, ns),
        in_specs=[tok, tok_c(yab.shape[1]), tok_c(yc.shape[1]), full(wmix),
                  pl.BlockSpec((1, d), lambda b, s: (0, 0)),
                  pl.BlockSpec((d, d), lambda b, s: (0, 0)),
                  pl.BlockSpec((None, d, m), lambda b, s: (b, 0, 0)),
                  pl.BlockSpec((None, m, d), lambda b, s: (b, 0, 0)),
                  pl.BlockSpec((d, d), lambda b, s: (0, 0))],
        out_specs=tok,
        out_shape=jax.ShapeDtypeStruct((bsz * seq, d), F32),
        compiler_params=_cparams(("parallel", "parallel")),
        name="mem_xattn",
    )(x, yab, yc, wmix, g, wq, kT, v, wo)


def _tiles(seq):
    assert seq % 512 == 0, "sequence length must be a multiple of 512"
    return dict(
        tm_ffn=512,
        tm_proj=512, rc_conv=64,
        qb=512, kc=512,
        tm_xa=1024, tm_rope=1024,
    )


def kernel(x, mem, positions, ffn1_norm, ffn1_w_gate, ffn1_w_up, ffn1_w_down, mix_norm, w_in,
           conf_dw, conf_dw_b, conf_ln_g, conf_ln_b, sc_dw, w_out, xa_norm, mem_norm,
           xa_wq, xa_wkv, xa_wo, ffn2_norm, ffn2_w_gate, ffn2_w_up, ffn2_w_down, final_norm):
    bsz, seq, d = x.shape
    depth = w_in.shape[0]
    t = bsz * seq
    tl = _tiles(seq)
    bf = lambda a: a.astype(BF16)
    row = lambda a: a.reshape(1, -1)

    cos, sin = _rope_tables(positions, tl["tm_rope"])
    xf = x.reshape(t, d)
    pad_cols = ((0, 0), (0, N_IN_PAD - w_in.shape[2]))

    for l in range(depth):
        xf = _ffn(xf, row(ffn1_norm[l]), bf(ffn1_w_gate[l]), bf(ffn1_w_up[l]), bf(ffn1_w_down[l]),
                  row(final_norm), final=False, tm=tl["tm_ffn"])

        yab, qT, k, vT, qiT, kidx, kwT = _proj(
            xf, row(mix_norm[l]), jnp.pad(bf(w_in[l]), pad_cols), cos, sin,
            conf_dw[l], row(conf_dw_b[l]), row(conf_ln_g[l]), row(conf_ln_b[l]), sc_dw[l],
            bsz=bsz, seq=seq, tm=tl["tm_proj"], kc=tl["kc"], rc=tl["rc_conv"])
        yc = _dsa(qT, qiT, kwT, k, vT, kidx, qb=tl["qb"], kc=tl["kc"])

        kT_mem, v_mem = _memkv(mem, row(mem_norm[l]), bf(xa_wkv[l]))
        xf = _xattn(xf, yab.reshape(t, CONV_CH + SC_CH), yc.reshape(t, ATT_W), bf(w_out[l]),
                    row(xa_norm[l]), bf(xa_wq[l]), kT_mem, v_mem, bf(xa_wo[l]),
                    bsz=bsz, seq=seq, tm=tl["tm_xa"])

        xf = _ffn(xf, row(ffn2_norm[l]), bf(ffn2_w_gate[l]), bf(ffn2_w_up[l]), bf(ffn2_w_down[l]),
                  row(final_norm), final=(l == depth - 1), tm=tl["tm_ffn"])

    return xf.reshape(bsz, seq, d)
```

```python
import functools
import math

import jax
import jax.numpy as jnp
from jax import lax
from jax.experimental import pallas as pl
from jax.experimental.pallas import tpu as pltpu

F32 = jnp.float32
BF16 = jnp.bfloat16
I32 = jnp.int32
I16 = jnp.int16

CONV_CH = 256
CONV_W = 31
SC_CH = 256
SC_W = 3
N_HEADS = 8
HEAD_DIM = 64
ATT_W = N_HEADS * HEAD_DIM
IDX_HEADS = 4
IDX_DIM = 64
TOPK_MAX = 256
ROPE_THETA = 10000.0
XA_HEADS = 4
IDX_SCALE = (IDX_HEADS ** -0.5) * (IDX_DIM ** -0.5)
ATT_SCALE = HEAD_DIM ** -0.5
QK_SCALE = ATT_SCALE * math.log2(math.e)

LANES = 128
SUBLANES = 8
VMEM_LIMIT = 58 * 1024 * 1024

CONV_COLS = 2 * CONV_CH + 3 * SC_CH
Q_OFF = CONV_COLS
K_OFF = Q_OFF + ATT_W
V_OFF = K_OFF + ATT_W
QI_OFF = V_OFF + ATT_W
KW_OFF = QI_OFF + IDX_HEADS * IDX_DIM
N_IN_PAD = KW_OFF + LANES

HALF = 2 ** 15
INT_MAX = 2 ** 31 - 1
NEG_INF_KEY = -2139095041
NEG_BIG = -1e30


def _cparams(sem, vmem=VMEM_LIMIT):
    return pltpu.CompilerParams(dimension_semantics=sem, vmem_limit_bytes=vmem)


def _rms(x, g, eps=1e-6):
    return x * lax.rsqrt(jnp.mean(x * x, axis=-1, keepdims=True) + eps) * g


def _sigmoid(x):
    return 1.0 / (1.0 + jnp.exp(-x))


def _rope_kernel(pos_ref, invf_ref, sgn_ref, cos_ref, sin_ref):
    ang = pos_ref[...].astype(F32) * invf_ref[...]
    cos_ref[...] = jnp.cos(ang)
    sin_ref[...] = jnp.sin(ang) * sgn_ref[...]


def _rope_tables(positions, tm):
    t = positions.size
    half = HEAD_DIM // 2
    inv_freq = ROPE_THETA ** (-jnp.arange(0, HEAD_DIM, 2, dtype=F32) / HEAD_DIM)
    invf = jnp.tile(inv_freq, LANES // half)[None, :]
    sgn = jnp.tile(jnp.concatenate([-jnp.ones((half,), F32), jnp.ones((half,), F32)]),
                   LANES // HEAD_DIM)[None, :]
    pos = jnp.broadcast_to(positions.reshape(t, 1), (t, LANES))
    row = pl.BlockSpec((tm, LANES), lambda i: (i, 0))
    one = pl.BlockSpec((1, LANES), lambda i: (0, 0))
    return pl.pallas_call(
        _rope_kernel,
        grid=(t // tm,),
        in_specs=[row, one, one],
        out_specs=[row, row],
        out_shape=[jax.ShapeDtypeStruct((t, LANES), F32)] * 2,
        compiler_params=_cparams(("parallel",)),
        name="rope_tables",
    )(pos, invf, sgn)


def _ffn_kernel(x_ref, g_ref, wg_ref, wu_ref, wd_ref, fg_ref, o_ref, *, final):
    x = x_ref[...]
    h = _rms(x, g_ref[...]).astype(BF16)
    a = jnp.dot(h, wg_ref[...], preferred_element_type=F32)
    b = jnp.dot(h, wu_ref[...], preferred_element_type=F32)
    t = (a * _sigmoid(a)) * b
    y = x + 0.5 * jnp.dot(t.astype(BF16), wd_ref[...], preferred_element_type=F32)
    if final:
        y = _rms(y, fg_ref[...])
    o_ref[...] = y


def _ffn(x, g, wg, wu, wd, fg, *, final, tm):
    t, d = x.shape
    f = wg.shape[1]
    resident = lambda r, c: pl.BlockSpec((r, c), lambda i: (0, 0), pipeline_mode=pl.Buffered(1))
    return pl.pallas_call(
        functools.partial(_ffn_kernel, final=final),
        grid=(t // tm,),
        in_specs=[
            pl.BlockSpec((tm, d), lambda i: (i, 0)),
            pl.BlockSpec((1, d), lambda i: (0, 0)),
            resident(d, f), resident(d, f), resident(f, d),
            pl.BlockSpec((1, d), lambda i: (0, 0)),
        ],
        out_specs=pl.BlockSpec((tm, d), lambda i: (i, 0)),
        out_shape=jax.ShapeDtypeStruct((t, d), F32),
        compiler_params=_cparams(("parallel",)),
        name="ffn_final" if final else "ffn",
    )(x, g, wg, wu, wd, fg)


CONV_HALO = 32
SC_HALO = 8


def _conv_mixers(cv, dw_ref, dwb_ref, lng_ref, lnb_ref, sw_ref, yab_ref, ha_scr, ub_scr, sh_scr, *, rc):
    ts = cv.shape[0]
    sub = SUBLANES
    first_tap = CONV_HALO - (CONV_W - 1)

    b_off = 2 * CONV_CH
    ha_scr[CONV_HALO:CONV_HALO + ts, :] = cv[:, 0:CONV_CH] * _sigmoid(cv[:, CONV_CH:2 * CONV_CH])
    ub_scr[SC_HALO:SC_HALO + ts, :] = cv[:, b_off + SC_CH:b_off + 2 * SC_CH] * cv[:, b_off + 2 * SC_CH:b_off + 3 * SC_CH]
    span = sh_scr.shape[1]
    for ph in range(1, sub):
        sh_scr[ph - 1] = ha_scr[ph:ph + span, :]

    for r in range(ts // rc):
        base = r * rc
        acc = jnp.zeros((rc, CONV_CH), F32)
        for j in range(CONV_W):
            off = base + first_tap + j
            ph = off % sub
            src = ha_scr[off:off + rc, :] if ph == 0 else sh_scr[ph - 1, off - ph:off - ph + rc, :]
            acc = acc + src * dw_ref[j:j + 1, :]
        acc = acc + dwb_ref[...]
        mu = jnp.mean(acc, axis=-1, keepdims=True)
        cen = acc - mu
        var = jnp.mean(cen * cen, axis=-1, keepdims=True)
        y = cen * lax.rsqrt(var + 1e-5) * lng_ref[...] + lnb_ref[...]
        yab_ref[base:base + rc, 0:CONV_CH] = (y * _sigmoid(y)).astype(BF16)

        accb = jnp.zeros((rc, SC_CH), F32)
        for j in range(SC_W):
            off = base + SC_HALO - (SC_W - 1) + j
            accb = accb + ub_scr[off:off + rc, :] * sw_ref[j:j + 1, :]
        yab_ref[base:base + rc, CONV_CH:CONV_CH + SC_CH] = (cv[base:base + rc, b_off:b_off + SC_CH] * accb).astype(BF16)

    ha_scr[0:CONV_HALO, :] = ha_scr[ts:ts + CONV_HALO, :]
    ub_scr[0:SC_HALO, :] = ub_scr[ts:ts + SC_HALO, :]


def _proj_kernel(x_ref, g_ref, w_ref, cos_ref, sin_ref, dw_ref, dwb_ref, lng_ref, lnb_ref, sw_ref,
                 yab_ref, qT_ref, k_ref, vT_ref, qiT_ref, kidx_ref, kwT_ref,
                 ha_scr, ub_scr, sh_scr, *, kc, rc):
    tm = x_ref.shape[0]

    @pl.when(pl.program_id(1) == 0)
    def _():
        ha_scr[0:CONV_HALO, :] = jnp.zeros((CONV_HALO, CONV_CH), F32)
        ub_scr[0:SC_HALO, :] = jnp.zeros((SC_HALO, SC_CH), F32)

    h = _rms(x_ref[...], g_ref[...]).astype(BF16)
    cos = cos_ref[...]
    sin = sin_ref[...]
    lane = lax.broadcasted_iota(I32, (tm, LANES), 1)
    first_half = (lane % HEAD_DIM) < (HEAD_DIM // 2)

    def rope(t, c, s):
        rot = jnp.where(first_half, pltpu.roll(t, LANES - HEAD_DIM // 2, 1), pltpu.roll(t, HEAD_DIM // 2, 1))
        return t * c + rot * s

    cv = jnp.dot(h, w_ref[:, 0:CONV_COLS], preferred_element_type=F32)
    _conv_mixers(cv, dw_ref, dwb_ref, lng_ref, lnb_ref, sw_ref, yab_ref, ha_scr, ub_scr, sh_scr, rc=rc)

    q = jnp.dot(h, w_ref[:, Q_OFF:Q_OFF + ATT_W], preferred_element_type=F32)
    for g in range(ATT_W // LANES):
        qr = rope(q[:, g * LANES:(g + 1) * LANES], cos, sin) * QK_SCALE
        qT_ref[g * LANES:(g + 1) * LANES, :] = qr.T.astype(BF16)

    k = jnp.dot(h, w_ref[:, K_OFF:K_OFF + ATT_W], preferred_element_type=F32)
    for g in range(ATT_W // LANES):
        k_ref[:, g * LANES:(g + 1) * LANES] = rope(k[:, g * LANES:(g + 1) * LANES], cos, sin).astype(BF16)

    v = jnp.dot(h, w_ref[:, V_OFF:V_OFF + ATT_W], preferred_element_type=F32)
    for c in range(tm // kc):
        vT_ref[c] = v[c * kc:(c + 1) * kc, :].T.astype(BF16)

    qi = jnp.dot(h, w_ref[:, QI_OFF:QI_OFF + IDX_HEADS * IDX_DIM], preferred_element_type=F32)
    for g in range(IDX_HEADS * IDX_DIM // LANES):
        qiT_ref[g * LANES:(g + 1) * LANES, :] = rope(qi[:, g * LANES:(g + 1) * LANES], cos, sin).T.astype(BF16)

    kw = jnp.dot(h, w_ref[:, KW_OFF:KW_OFF + LANES], preferred_element_type=F32)
    is_kidx = lane < IDX_DIM
    kw = rope(kw, jnp.where(is_kidx, cos, 1.0), jnp.where(is_kidx, sin, 0.0))
    kidx_ref[...] = kw.astype(BF16)
    kwT_ref[...] = kw.T


def _proj(x, g, w, cos, sin, dw, dwb, lng, lnb, sw, *, bsz, seq, tm, kc, rc):
    d = x.shape[1]
    ns = seq // tm
    tok = lambda b, s: (b * ns + s, 0)
    small = lambda r, c: pl.BlockSpec((r, c), lambda b, s: (0, 0))
    return pl.pallas_call(
        functools.partial(_proj_kernel, kc=kc, rc=rc),
        grid=(bsz, ns),
        in_specs=[
            pl.BlockSpec((tm, d), tok),
            small(1, d),
            small(d, N_IN_PAD),
            pl.BlockSpec((tm, LANES), tok),
            pl.BlockSpec((tm, LANES), tok),
            small(CONV_W, CONV_CH), small(1, CONV_CH), small(1, CONV_CH), small(1, CONV_CH),
            small(SC_W, SC_CH),
        ],
        out_specs=[
            pl.BlockSpec((None, tm, CONV_CH + SC_CH), lambda b, s: (b, s, 0)),
            pl.BlockSpec((None, ATT_W, tm), lambda b, s: (b, 0, s)),
            pl.BlockSpec((None, tm, ATT_W), lambda b, s: (b, s, 0)),
            pl.BlockSpec((None, tm // kc, ATT_W, kc), lambda b, s: (b, s, 0, 0)),
            pl.BlockSpec((None, IDX_HEADS * IDX_DIM, tm), lambda b, s: (b, 0, s)),
            pl.BlockSpec((None, tm, LANES), lambda b, s: (b, s, 0)),
            pl.BlockSpec((None, LANES, tm), lambda b, s: (b, 0, s)),
        ],
        out_shape=[
            jax.ShapeDtypeStruct((bsz, seq, CONV_CH + SC_CH), BF16),
            jax.ShapeDtypeStruct((bsz, ATT_W, seq), BF16),
            jax.ShapeDtypeStruct((bsz, seq, ATT_W), BF16),
            jax.ShapeDtypeStruct((bsz, seq // kc, ATT_W, kc), BF16),
            jax.ShapeDtypeStruct((bsz, IDX_HEADS * IDX_DIM, seq), BF16),
            jax.ShapeDtypeStruct((bsz, seq, LANES), BF16),
            jax.ShapeDtypeStruct((bsz, LANES, seq), F32),
        ],
        scratch_shapes=[pltpu.VMEM((tm + CONV_HALO, CONV_CH), F32),
                        pltpu.VMEM((tm + SC_HALO, SC_CH), F32),
                        pltpu.VMEM((SUBLANES - 1, tm + CONV_HALO - SUBLANES, CONV_CH), F32)],
        compiler_params=_cparams(("arbitrary", "arbitrary")),
        name="in_proj",
    )(x, g, w, cos, sin, dw, dwb, lng, lnb, sw)


def _sortable(x):
    b = lax.bitcast_convert_type(x, I32)
    return b ^ ((b >> 31) & INT_MAX)


def _dsa_kernel(qT_ref, qiT_ref, wT_ref, k_ref, vT_ref, kidx_ref, o_ref,
                key_scr, k16_scr, oT_scr, qpad_scr, m_scr, l_scr, *, kc, topk):
    qb = qT_ref.shape[1]
    i = pl.program_id(1)
    nk = (i * qb) // kc + 1
    q_pos = i * qb + lax.broadcasted_iota(I32, (1, qb), 1)
    row_iota = lax.broadcasted_iota(I32, (kc, qb), 0)

    w = wT_ref[...] * IDX_SCALE
    zeros_half = jnp.zeros((LANES - IDX_DIM, qb), BF16)
    qi = [jnp.concatenate([qiT_ref[h * IDX_DIM:(h + 1) * IDX_DIM, :], zeros_half], axis=0)
          for h in range(IDX_HEADS)]

    def score_chunk(c, diagonal):
        kcs = kidx_ref[pl.ds(pl.multiple_of(c * kc, kc), kc), :]
        sc = jnp.zeros((kc, qb), F32)
        for h in range(IDX_HEADS):
            lg = jnp.dot(kcs, qi[h], preferred_element_type=F32)
            sc = sc + w[h:h + 1, :] * jnp.maximum(lg, 0.0)
        if diagonal:
            sc = jnp.where((c * kc + row_iota) <= q_pos, sc, -jnp.inf)
        key = _sortable(sc)
        key_scr[c] = key
        k16_scr[c] = (key >> 16).astype(I16)

    def score_body(c, carry):
        score_chunk(c, False)
        return carry

    lax.fori_loop(0, nk - 1, score_body, 0)
    score_chunk(nk - 1, True)

    fold = 16 * 2 * SUBLANES * LANES // qb

    def count16(mid):
        midb = jnp.broadcast_to(mid, (fold, qb)).astype(I16)

        def body(c, acc):
            for r in range(kc // fold):
                acc = acc + jnp.where(k16_scr[c, r * fold:(r + 1) * fold, :] >= midb, jnp.int16(1), jnp.int16(0))
            return acc

        acc = lax.fori_loop(0, nk, body, jnp.zeros((fold, qb), I16))
        return jnp.sum(acc.astype(I32), axis=0, keepdims=True)

    def bisect16(want, cnt_all):
        def step(_, carry):
            lo, hi, cnt_lo, cnt_hi = carry
            mid = (lo + hi) >> 1
            cnt = jnp.where(mid == -HALF, cnt_all, count16(mid))
            ok = cnt >= want
            return (jnp.where(ok, mid, lo), jnp.where(ok, hi, mid),
                    jnp.where(ok, cnt, cnt_lo), jnp.where(ok, cnt_hi, cnt))
        zero = jnp.zeros((1, qb), I32)
        t, _, cnt_t, cnt_above = lax.fori_loop(0, 16, step, (zero - HALF, zero + HALF, cnt_all, zero))
        return t, cnt_t, cnt_above

    n_stored = nk * kc + jnp.zeros((1, qb), I32)
    t_hi, cge_hi, cgt_hi = bisect16(topk, n_stored)

    t_hi16 = jnp.broadcast_to(t_hi, (fold, qb)).astype(I16)

    def low_body(c, carry):
        for r in range(kc // fold):
            rows = slice(r * fold, (r + 1) * fold)
            low = (key_scr[c, rows, :] ^ HALF).astype(I16)
            k16_scr[c, rows, :] = jnp.where(k16_scr[c, rows, :] == t_hi16, low, jnp.int16(-HALF))
        return carry

    lax.fori_loop(0, nk, low_body, 0)
    t_lo, cge_lo, cgt_lo = bisect16(topk - cgt_hi, cge_hi - cgt_hi)

    tau = t_hi * (2 * HALF) + (t_lo + HALF)
    full = tau > NEG_INF_KEY
    tau_eff = jnp.maximum(tau, NEG_INF_KEY)
    need = jnp.where(full, topk - (cgt_hi + cgt_lo), 0).astype(F32)
    ltri = jnp.where(lax.broadcasted_iota(I32, (kc, kc), 0) >= lax.broadcasted_iota(I32, (kc, kc), 1),
                     1.0, 0.0).astype(BF16)

    zeros_head = jnp.zeros((HEAD_DIM, qb), BF16)
    for h in range(N_HEADS):
        qh = qT_ref[h * HEAD_DIM:(h + 1) * HEAD_DIM, :]
        qpad_scr[h] = jnp.concatenate([qh, zeros_head] if h % 2 == 0 else [zeros_head, qh], axis=0)
    m_scr[...] = jnp.full((N_HEADS, qb), NEG_BIG, F32)
    l_scr[...] = jnp.zeros((N_HEADS, qb), F32)
    oT_scr[...] = jnp.zeros((ATT_W, qb), F32)

    ones_rows = jnp.ones((16, kc), BF16)

    def att_body(c, run):
        kk = key_scr[c]
        eq = kk == tau_eff
        rank = jnp.dot(ltri, jnp.where(eq, 1.0, 0.0).astype(BF16), preferred_element_type=F32)
        tie = jnp.where(rank <= need - run, 0.0, -jnp.inf)
        bias = jnp.where(kk > tau_eff, 0.0, jnp.where(eq, tie, -jnp.inf))
        row0 = pl.multiple_of(c * kc, kc)
        m_old = m_scr[...]
        l_old = l_scr[...]
        s_all = [jnp.dot(k_ref[pl.ds(row0, kc), (h // 2) * LANES:(h // 2 + 1) * LANES], qpad_scr[h],
                         preferred_element_type=F32) + bias for h in range(N_HEADS)]
        m_new, alphas, p_all = [], [], []
        for h in range(N_HEADS):
            m = m_old[h:h + 1, :]
            mn = jnp.maximum(m, jnp.max(s_all[h], axis=0, keepdims=True))
            p_all.append(jnp.exp2(s_all[h] - mn).astype(BF16))
            alphas.append(jnp.exp2(m - mn))
            m_new.append(mn)
        pv = [jnp.dot(jnp.concatenate([vT_ref[c, h * HEAD_DIM:(h + 1) * HEAD_DIM, :], ones_rows], axis=0),
                      p_all[h], preferred_element_type=F32) for h in range(N_HEADS)]
        alpha_rows = [jnp.broadcast_to(a, (HEAD_DIM, qb)) for a in alphas]
        oT_scr[...] = (jnp.concatenate(alpha_rows, axis=0) * oT_scr[...]
                       + jnp.concatenate([x[0:HEAD_DIM] for x in pv], axis=0))
        m_scr[...] = jnp.concatenate(m_new, axis=0)
        l_scr[...] = (jnp.concatenate(alphas, axis=0) * l_old
                      + jnp.concatenate([x[HEAD_DIM:HEAD_DIM + 1] for x in pv], axis=0))
        return run + rank[kc - 1:kc, :]

    lax.fori_loop(0, nk, att_body, jnp.zeros((1, qb), F32))
    for h in range(N_HEADS):
        hs = slice(h * HEAD_DIM, (h + 1) * HEAD_DIM)
        oT_scr[hs, :] = oT_scr[hs, :] / l_scr[h:h + 1, :]

    o_ref[...] = oT_scr[...].T.astype(BF16)


def _dsa(qT, qiT, kwT, k, vT, kidx, *, qb, kc):
    bsz, seq, _ = k.shape
    assert kc % qb == 0 and seq % kc == 0
    topk = min(TOPK_MAX, seq // 4)
    w_row_block = IDX_DIM // 8
    return pl.pallas_call(
        functools.partial(_dsa_kernel, kc=kc, topk=topk),
        grid=(bsz, seq // qb),
        in_specs=[
            pl.BlockSpec((None, ATT_W, qb), lambda b, i: (b, 0, i)),
            pl.BlockSpec((None, IDX_HEADS * IDX_DIM, qb), lambda b, i: (b, 0, i)),
            pl.BlockSpec((None, 8, qb), lambda b, i: (b, w_row_block, i)),
            pl.BlockSpec((None, seq, ATT_W), lambda b, i: (b, 0, 0), pipeline_mode=pl.Buffered(1)),
            pl.BlockSpec((None, seq // kc, ATT_W, kc), lambda b, i: (b, 0, 0, 0), pipeline_mode=pl.Buffered(1)),
            pl.BlockSpec((None, seq, LANES), lambda b, i: (b, 0, 0), pipeline_mode=pl.Buffered(1)),
        ],
        out_specs=pl.BlockSpec((None, qb, ATT_W), lambda b, i: (b, i, 0)),
        out_shape=jax.ShapeDtypeStruct((bsz, seq, ATT_W), BF16),
        scratch_shapes=[pltpu.VMEM((seq // kc, kc, qb), I32),
                        pltpu.VMEM((seq // kc, kc, qb), I16),
                        pltpu.VMEM((ATT_W, qb), F32),
                        pltpu.VMEM((N_HEADS, LANES, qb), BF16),
                        pltpu.VMEM((N_HEADS, qb), F32),
                        pltpu.VMEM((N_HEADS, qb), F32)],
        compiler_params=_cparams(("parallel", "parallel")),
        name="dsa_attention",
    )(qT, qiT, kwT, k, vT, kidx)


def _memkv_kernel(mem_ref, g_ref, wkv_ref, kT_ref, v_ref):
    d = mem_ref.shape[1]
    mn = _rms(mem_ref[...], g_ref[...]).astype(BF16)
    kv = jnp.dot(mn, wkv_ref[...], preferred_element_type=F32)
    kT_ref[...] = kv[:, 0:d].T.astype(BF16)
    v_ref[...] = kv[:, d:2 * d].astype(BF16)


def _memkv(mem, g, wkv):
    bsz, m, d = mem.shape
    return pl.pallas_call(
        _memkv_kernel,
        grid=(bsz,),
        in_specs=[pl.BlockSpec((None, m, d), lambda b: (b, 0, 0)),
                  pl.BlockSpec((1, d), lambda b: (0, 0)),
                  pl.BlockSpec((d, 2 * d), lambda b: (0, 0))],
        out_specs=[pl.BlockSpec((None, d, m), lambda b: (b, 0, 0)),
                   pl.BlockSpec((None, m, d), lambda b: (b, 0, 0))],
        out_shape=[jax.ShapeDtypeStruct((bsz, d, m), BF16), jax.ShapeDtypeStruct((bsz, m, d), BF16)],
        compiler_params=_cparams(("parallel",)),
        name="mem_kv",
    )(mem, g, wkv)


def _xattn_kernel(x_ref, yab_ref, yc_ref, wmix_ref, g_ref, wq_ref, kT_ref, v_ref, wo_ref, o_ref):
    d = x_ref.shape[1]
    hd = d // XA_HEADS
    n_ab = yab_ref.shape[1]
    x = (x_ref[...]
         + jnp.dot(yab_ref[...], wmix_ref[0:n_ab, :], preferred_element_type=F32)
         + jnp.dot(yc_ref[...], wmix_ref[n_ab:, :], preferred_element_type=F32))
    hq = _rms(x, g_ref[...]).astype(BF16)
    q = (jnp.dot(hq, wq_ref[...], preferred_element_type=F32) * (hd ** -0.5)).astype(BF16)
    outs = []
    for h in range(XA_HEADS):
        s = jnp.dot(q[:, h * hd:(h + 1) * hd], kT_ref[h * hd:(h + 1) * hd, :], preferred_element_type=F32)
        m = jnp.max(s, axis=-1, keepdims=True)
        p = jnp.exp(s - m)
        l = jnp.sum(p, axis=-1, keepdims=True)
        o = jnp.dot(p.astype(BF16), v_ref[:, h * hd:(h + 1) * hd], preferred_element_type=F32) / l
        outs.append(o.astype(BF16))
    o_ref[...] = x + jnp.dot(jnp.concatenate(outs, axis=-1), wo_ref[...], preferred_element_type=F32)


def _xattn(x, yab, yc, wmix, g, wq, kT, v, wo, *, bsz, seq, tm):
    d = x.shape[1]
    m = v.shape[1]
    ns = seq // tm
    tok_c = lambda c: pl.BlockSpec((tm, c), lambda b, s: (b * ns + s, 0))
    tok = tok_c(d)
    full = lambda a: pl.BlockSpec(a.shape, lambda b, s: (0, 0))
    return pl.pallas_call(
        _xattn_kernel,
        grid=(bsz, ns),
        in_specs=[tok, tok_c(yab.shape[1]), tok_c(yc.shape[1]), full(wmix),
                  pl.BlockSpec((1, d), lambda b, s: (0, 0)),
                  pl.BlockSpec((d, d), lambda b, s: (0, 0)),
                  pl.BlockSpec((None, d, m), lambda b, s: (b, 0, 0)),
                  pl.BlockSpec((None, m, d), lambda b, s: (b, 0, 0)),
                  pl.BlockSpec((d, d), lambda b, s: (0, 0))],
        out_specs=tok,
        out_shape=jax.ShapeDtypeStruct((bsz * seq, d), F32),
        compiler_params=_cparams(("parallel", "parallel")),
        name="mem_xattn",
    )(x, yab, yc, wmix, g, wq, kT, v, wo)


def _tiles(seq):
    tiles = dict(
        tm_ffn=512,
        tm_proj=512, rc_conv=64,
        qb=512, kc=512,
        tm_xa=1024, tm_rope=1024,
    )
    assert all(seq % tiles[name] == 0 for name in ("tm_ffn", "tm_proj", "qb", "kc", "tm_xa", "tm_rope")), seq
    return tiles


def kernel(x, mem, positions, ffn1_norm, ffn1_w_gate, ffn1_w_up, ffn1_w_down, mix_norm, w_in,
           conf_dw, conf_dw_b, conf_ln_g, conf_ln_b, sc_dw, w_out, xa_norm, mem_norm,
           xa_wq, xa_wkv, xa_wo, ffn2_norm, ffn2_w_gate, ffn2_w_up, ffn2_w_down, final_norm):
    bsz, seq, d = x.shape
    depth = w_in.shape[0]
    t = bsz * seq
    tl = _tiles(seq)
    bf = lambda a: a.astype(BF16)
    row = lambda a: a.reshape(1, -1)

    cos, sin = _rope_tables(positions, tl["tm_rope"])
    xf = x.reshape(t, d)
    pad_cols = ((0, 0), (0, N_IN_PAD - w_in.shape[2]))

    for l in range(depth):
        xf = _ffn(xf, row(ffn1_norm[l]), bf(ffn1_w_gate[l]), bf(ffn1_w_up[l]), bf(ffn1_w_down[l]),
                  row(final_norm), final=False, tm=tl["tm_ffn"])

        yab, qT, k, vT, qiT, kidx, kwT = _proj(
            xf, row(mix_norm[l]), jnp.pad(bf(w_in[l]), pad_cols), cos, sin,
            conf_dw[l], row(conf_dw_b[l]), row(conf_ln_g[l]), row(conf_ln_b[l]), sc_dw[l],
            bsz=bsz, seq=seq, tm=tl["tm_proj"], kc=tl["kc"], rc=tl["rc_conv"])
        yc = _dsa(qT, qiT, kwT, k, vT, kidx, qb=tl["qb"], kc=tl["kc"])

        kT_mem, v_mem = _memkv(mem, row(mem_norm[l]), bf(xa_wkv[l]))
        xf = _xattn(xf, yab.reshape(t, CONV_CH + SC_CH), yc.reshape(t, ATT_W), bf(w_out[l]),
                    row(xa_norm[l]), bf(xa_wq[l]), kT_mem, v_mem, bf(xa_wo[l]),
                    bsz=bsz, seq=seq, tm=tl["tm_xa"])

        xf = _ffn(xf, row(ffn2_norm[l]), bf(ffn2_w_gate[l]), bf(ffn2_w_up[l]), bf(ffn2_w_down[l]),
                  row(final_norm), final=(l == depth - 1), tm=tl["tm_ffn"])

    return xf.reshape(bsz, seq, d)
```

```python
import functools
import math

import jax
import jax.numpy as jnp
from jax import lax
from jax.experimental import pallas as pl
from jax.experimental.pallas import tpu as pltpu

F32 = jnp.float32
BF16 = jnp.bfloat16
I32 = jnp.int32

CONV_CH = 256
CONV_W = 31
SC_CH = 256
SC_W = 3
N_HEADS = 8
HEAD_DIM = 64
ATT_W = N_HEADS * HEAD_DIM
IDX_HEADS = 4
IDX_DIM = 64
TOPK_MAX = 256
ROPE_THETA = 10000.0
XA_HEADS = 4
IDX_SCALE = (IDX_HEADS ** -0.5) * (IDX_DIM ** -0.5)
ATT_SCALE = HEAD_DIM ** -0.5
QK_SCALE = ATT_SCALE * math.log2(math.e)

LANES = 128
SUBLANES = 8
VMEM_LIMIT = 58 * 1024 * 1024

CONV_COLS = 2 * CONV_CH + 3 * SC_CH
Q_OFF = CONV_COLS
K_OFF = Q_OFF + ATT_W
V_OFF = K_OFF + ATT_W
QI_OFF = V_OFF + ATT_W
KW_OFF = QI_OFF + IDX_HEADS * IDX_DIM
N_IN_PAD = KW_OFF + LANES

KEY_BITS = 32
INT_MIN = -2 ** 31
INT_MAX = 2 ** 31 - 1
NEG_INF_KEY = -2139095041
NEG_BIG = -1e30


def _cparams(sem, vmem=VMEM_LIMIT):
    return pltpu.CompilerParams(dimension_semantics=sem, vmem_limit_bytes=vmem)


def _rms(x, g, eps=1e-6):
    return x * lax.rsqrt(jnp.mean(x * x, axis=-1, keepdims=True) + eps) * g


def _sigmoid(x):
    return 1.0 / (1.0 + jnp.exp(-x))


def _rope_kernel(pos_ref, invf_ref, sgn_ref, cos_ref, sin_ref):
    ang = pos_ref[...].astype(F32) * invf_ref[...]
    cos_ref[...] = jnp.cos(ang)
    sin_ref[...] = jnp.sin(ang) * sgn_ref[...]


def _rope_tables(positions, tm):
    t = positions.size
    half = HEAD_DIM // 2
    inv_freq = ROPE_THETA ** (-jnp.arange(0, HEAD_DIM, 2, dtype=F32) / HEAD_DIM)
    invf = jnp.tile(inv_freq, LANES // half)[None, :]
    sgn = jnp.tile(jnp.concatenate([-jnp.ones((half,), F32), jnp.ones((half,), F32)]),
                   LANES // HEAD_DIM)[None, :]
    pos = jnp.broadcast_to(positions.reshape(t, 1), (t, LANES))
    row = pl.BlockSpec((tm, LANES), lambda i: (i, 0))
    one = pl.BlockSpec((1, LANES), lambda i: (0, 0))
    return pl.pallas_call(
        _rope_kernel,
        grid=(t // tm,),
        in_specs=[row, one, one],
        out_specs=[row, row],
        out_shape=[jax.ShapeDtypeStruct((t, LANES), F32)] * 2,
        compiler_params=_cparams(("parallel",)),
        name="rope_tables",
    )(pos, invf, sgn)


def _ffn_kernel(x_ref, g_ref, wg_ref, wu_ref, wd_ref, fg_ref, o_ref, *, final):
    x = x_ref[...]
    h = _rms(x, g_ref[...]).astype(BF16)
    a = jnp.dot(h, wg_ref[...], preferred_element_type=F32)
    b = jnp.dot(h, wu_ref[...], preferred_element_type=F32)
    t = (a * _sigmoid(a)) * b
    y = x + 0.5 * jnp.dot(t.astype(BF16), wd_ref[...], preferred_element_type=F32)
    if final:
        y = _rms(y, fg_ref[...])
    o_ref[...] = y


def _ffn(x, g, wg, wu, wd, fg, *, final, tm):
    t, d = x.shape
    f = wg.shape[1]
    resident = lambda r, c: pl.BlockSpec((r, c), lambda i: (0, 0), pipeline_mode=pl.Buffered(1))
    return pl.pallas_call(
        functools.partial(_ffn_kernel, final=final),
        grid=(t // tm,),
        in_specs=[
            pl.BlockSpec((tm, d), lambda i: (i, 0)),
            pl.BlockSpec((1, d), lambda i: (0, 0)),
            resident(d, f), resident(d, f), resident(f, d),
            pl.BlockSpec((1, d), lambda i: (0, 0)),
        ],
        out_specs=pl.BlockSpec((tm, d), lambda i: (i, 0)),
        out_shape=jax.ShapeDtypeStruct((t, d), F32),
        compiler_params=_cparams(("parallel",)),
        name="ffn_final" if final else "ffn",
    )(x, g, wg, wu, wd, fg)


CONV_HALO = 32
SC_HALO = 8


def _conv_mixers(cv, dw_ref, dwb_ref, lng_ref, lnb_ref, sw_ref, yab_ref, ha_scr, ub_scr, sh_scr, *, rc):
    ts = cv.shape[0]
    sub = SUBLANES
    first_tap = CONV_HALO - (CONV_W - 1)

    b_off = 2 * CONV_CH
    ha_scr[CONV_HALO:CONV_HALO + ts, :] = cv[:, 0:CONV_CH] * _sigmoid(cv[:, CONV_CH:2 * CONV_CH])
    ub_scr[SC_HALO:SC_HALO + ts, :] = cv[:, b_off + SC_CH:b_off + 2 * SC_CH] * cv[:, b_off + 2 * SC_CH:b_off + 3 * SC_CH]
    span = sh_scr.shape[1]
    for ph in range(1, sub):
        sh_scr[ph - 1] = ha_scr[ph:ph + span, :]

    for r in range(ts // rc):
        base = r * rc
        acc = jnp.zeros((rc, CONV_CH), F32)
        for j in range(CONV_W):
            off = base + first_tap + j
            ph = off % sub
            src = ha_scr[off:off + rc, :] if ph == 0 else sh_scr[ph - 1, off - ph:off - ph + rc, :]
            acc = acc + src * dw_ref[j:j + 1, :]
        acc = acc + dwb_ref[...]
        mu = jnp.mean(acc, axis=-1, keepdims=True)
        cen = acc - mu
        var = jnp.mean(cen * cen, axis=-1, keepdims=True)
        y = cen * lax.rsqrt(var + 1e-5) * lng_ref[...] + lnb_ref[...]
        yab_ref[base:base + rc, 0:CONV_CH] = (y * _sigmoid(y)).astype(BF16)

        accb = jnp.zeros((rc, SC_CH), F32)
        for j in range(SC_W):
            off = base + SC_HALO - (SC_W - 1) + j
            accb = accb + ub_scr[off:off + rc, :] * sw_ref[j:j + 1, :]
        yab_ref[base:base + rc, CONV_CH:CONV_CH + SC_CH] = (cv[base:base + rc, b_off:b_off + SC_CH] * accb).astype(BF16)

    ha_scr[0:CONV_HALO, :] = ha_scr[ts:ts + CONV_HALO, :]
    ub_scr[0:SC_HALO, :] = ub_scr[ts:ts + SC_HALO, :]


def _proj_kernel(x_ref, g_ref, w_ref, cos_ref, sin_ref, dw_ref, dwb_ref, lng_ref, lnb_ref, sw_ref,
                 yab_ref, qT_ref, k_ref, vT_ref, qiT_ref, kidx_ref, kwT_ref,
                 ha_scr, ub_scr, sh_scr, *, kc, rc):
    tm = x_ref.shape[0]

    @pl.when(pl.program_id(1) == 0)
    def _():
        ha_scr[0:CONV_HALO, :] = jnp.zeros((CONV_HALO, CONV_CH), F32)
        ub_scr[0:SC_HALO, :] = jnp.zeros((SC_HALO, SC_CH), F32)

    h = _rms(x_ref[...], g_ref[...]).astype(BF16)
    cos = cos_ref[...]
    sin = sin_ref[...]
    lane = lax.broadcasted_iota(I32, (tm, LANES), 1)
    first_half = (lane % HEAD_DIM) < (HEAD_DIM // 2)

    def rope(t, c, s):
        rot = jnp.where(first_half, pltpu.roll(t, LANES - HEAD_DIM // 2, 1), pltpu.roll(t, HEAD_DIM // 2, 1))
        return t * c + rot * s

    cv = jnp.dot(h, w_ref[:, 0:CONV_COLS], preferred_element_type=F32)
    _conv_mixers(cv, dw_ref, dwb_ref, lng_ref, lnb_ref, sw_ref, yab_ref, ha_scr, ub_scr, sh_scr, rc=rc)

    q = jnp.dot(h, w_ref[:, Q_OFF:Q_OFF + ATT_W], preferred_element_type=F32)
    for g in range(ATT_W // LANES):
        qr = rope(q[:, g * LANES:(g + 1) * LANES], cos, sin) * QK_SCALE
        qT_ref[g * LANES:(g + 1) * LANES, :] = qr.T.astype(BF16)

    k = jnp.dot(h, w_ref[:, K_OFF:K_OFF + ATT_W], preferred_element_type=F32)
    for g in range(ATT_W // LANES):
        k_ref[:, g * LANES:(g + 1) * LANES] = rope(k[:, g * LANES:(g + 1) * LANES], cos, sin).astype(BF16)

    v = jnp.dot(h, w_ref[:, V_OFF:V_OFF + ATT_W], preferred_element_type=F32)
    for c in range(tm // kc):
        vT_ref[c] = v[c * kc:(c + 1) * kc, :].T.astype(BF16)

    qi = jnp.dot(h, w_ref[:, QI_OFF:QI_OFF + IDX_HEADS * IDX_DIM], preferred_element_type=F32)
    for g in range(IDX_HEADS * IDX_DIM // LANES):
        qiT_ref[g * LANES:(g + 1) * LANES, :] = rope(qi[:, g * LANES:(g + 1) * LANES], cos, sin).T.astype(BF16)

    kw = jnp.dot(h, w_ref[:, KW_OFF:KW_OFF + LANES], preferred_element_type=F32)
    is_kidx = lane < IDX_DIM
    kw = rope(kw, jnp.where(is_kidx, cos, 1.0), jnp.where(is_kidx, sin, 0.0))
    kidx_ref[...] = kw.astype(BF16)
    kwT_ref[...] = kw.T


def _proj(x, g, w, cos, sin, dw, dwb, lng, lnb, sw, *, bsz, seq, tm, kc, rc):
    d = x.shape[1]
    ns = seq // tm
    tok = lambda b, s: (b * ns + s, 0)
    small = lambda r, c: pl.BlockSpec((r, c), lambda b, s: (0, 0))
    return pl.pallas_call(
        functools.partial(_proj_kernel, kc=kc, rc=rc),
        grid=(bsz, ns),
        in_specs=[
            pl.BlockSpec((tm, d), tok),
            small(1, d),
            small(d, N_IN_PAD),
            pl.BlockSpec((tm, LANES), tok),
            pl.BlockSpec((tm, LANES), tok),
            small(CONV_W, CONV_CH), small(1, CONV_CH), small(1, CONV_CH), small(1, CONV_CH),
            small(SC_W, SC_CH),
        ],
        out_specs=[
            pl.BlockSpec((None, tm, CONV_CH + SC_CH), lambda b, s: (b, s, 0)),
            pl.BlockSpec((None, ATT_W, tm), lambda b, s: (b, 0, s)),
            pl.BlockSpec((None, tm, ATT_W), lambda b, s: (b, s, 0)),
            pl.BlockSpec((None, tm // kc, ATT_W, kc), lambda b, s: (b, s, 0, 0)),
            pl.BlockSpec((None, IDX_HEADS * IDX_DIM, tm), lambda b, s: (b, 0, s)),
            pl.BlockSpec((None, tm, LANES), lambda b, s: (b, s, 0)),
            pl.BlockSpec((None, LANES, tm), lambda b, s: (b, 0, s)),
        ],
        out_shape=[
            jax.ShapeDtypeStruct((bsz, seq, CONV_CH + SC_CH), BF16),
            jax.ShapeDtypeStruct((bsz, ATT_W, seq), BF16),
            jax.ShapeDtypeStruct((bsz, seq, ATT_W), BF16),
            jax.ShapeDtypeStruct((bsz, seq // kc, ATT_W, kc), BF16),
            jax.ShapeDtypeStruct((bsz, IDX_HEADS * IDX_DIM, seq), BF16),
            jax.ShapeDtypeStruct((bsz, seq, LANES), BF16),
            jax.ShapeDtypeStruct((bsz, LANES, seq), F32),
        ],
        scratch_shapes=[pltpu.VMEM((tm + CONV_HALO, CONV_CH), F32),
                        pltpu.VMEM((tm + SC_HALO, SC_CH), F32),
                        pltpu.VMEM((SUBLANES - 1, tm + CONV_HALO - SUBLANES, CONV_CH), F32)],
        compiler_params=_cparams(("arbitrary", "arbitrary")),
        name="in_proj",
    )(x, g, w, cos, sin, dw, dwb, lng, lnb, sw)


def _sortable(x):
    b = lax.bitcast_convert_type(x, I32)
    return b ^ ((b >> 31) & INT_MAX)


def _transpose_bits(rows):
    rows = list(rows)
    j, m = KEY_BITS // 2, 0x0000FFFF
    while j:
        k = 0
        while k < KEY_BITS:
            t = (rows[k] ^ lax.shift_right_logical(rows[k + j], jnp.int32(j))) & m
            rows[k] = rows[k] ^ t
            rows[k + j] = rows[k + j] ^ (t << j)
            k = (k + j + 1) & ~j
        j >>= 1
        m = (m ^ (m << j)) & 0xFFFFFFFF
    return rows


def _dsa_kernel(qT_ref, qiT_ref, wT_ref, k_ref, vT_ref, kidx_ref, o_ref,
                key_scr, planes_scr, alive_scr, oT_scr, qpad_scr, m_scr, l_scr, *, kc, topk):
    qb = qT_ref.shape[1]
    i = pl.program_id(1)
    nk = (i * qb) // kc + 1
    q_pos = i * qb + lax.broadcasted_iota(I32, (1, qb), 1)
    row_iota = lax.broadcasted_iota(I32, (kc, qb), 0)

    w = wT_ref[...] * IDX_SCALE
    zeros_half = jnp.zeros((LANES - IDX_DIM, qb), BF16)
    qi = [jnp.concatenate([qiT_ref[h * IDX_DIM:(h + 1) * IDX_DIM, :], zeros_half], axis=0)
          for h in range(IDX_HEADS)]

    def score_chunk(c, diagonal):
        kcs = kidx_ref[pl.ds(pl.multiple_of(c * kc, kc), kc), :]
        sc = jnp.zeros((kc, qb), F32)
        for h in range(IDX_HEADS):
            lg = jnp.dot(kcs, qi[h], preferred_element_type=F32)
            sc = sc + w[h:h + 1, :] * jnp.maximum(lg, 0.0)
        if diagonal:
            sc = jnp.where((c * kc + row_iota) <= q_pos, sc, -jnp.inf)
        key = _sortable(sc)
        key_scr[c] = key

    def score_body(c, carry):
        score_chunk(c, False)
        return carry

    lax.fori_loop(0, nk - 1, score_body, 0)
    score_chunk(nk - 1, True)

    grp = kc // KEY_BITS

    def plane_body(c, carry):
        u = key_scr[c] ^ INT_MIN
        planes = _transpose_bits([u[a * grp:(a + 1) * grp, :] for a in range(KEY_BITS)])
        for b in range(KEY_BITS):
            planes_scr[c, b] = planes[b]
        alive_scr[c] = jnp.full((grp, qb), -1, I32)
        return carry

    lax.fori_loop(0, nk, plane_body, 0)

    def decide(b, acc, cnt_gt, prefix):
        cnt_b = jnp.sum(acc, axis=0, keepdims=True)
        take = (cnt_gt + cnt_b) >= topk
        bit = jnp.left_shift(jnp.int32(1), KEY_BITS - 1 - b)
        return (jnp.where(take, cnt_gt, cnt_gt + cnt_b), prefix | jnp.where(take, bit, 0),
                jnp.where(take, 0, -1) + jnp.zeros((grp, qb), I32))

    def first_count(c, acc):
        return acc + lax.population_count(planes_scr[c, 0])

    zero_row = jnp.zeros((1, qb), I32)
    acc0 = lax.fori_loop(0, nk, first_count, jnp.zeros((grp, qb), I32))
    state = decide(0, acc0, zero_row, zero_row)

    def bit_step(b, state):
        cnt_gt, prefix, drop = state

        def body(c, acc):
            alive = alive_scr[c] & (planes_scr[c, b - 1] ^ drop)
            alive_scr[c] = alive
            return acc + lax.population_count(alive & planes_scr[c, b])

        acc = lax.fori_loop(0, nk, body, jnp.zeros((grp, qb), I32))
        return decide(b, acc, cnt_gt, prefix)

    cnt_gt, prefix, _ = lax.fori_loop(1, KEY_BITS, bit_step, state)

    tau = prefix ^ INT_MIN
    full = tau > NEG_INF_KEY
    tau_eff = jnp.maximum(tau, NEG_INF_KEY)
    need = jnp.where(full, topk - cnt_gt, 0).astype(F32)
    ltri = jnp.where(lax.broadcasted_iota(I32, (kc, kc), 0) >= lax.broadcasted_iota(I32, (kc, kc), 1),
                     1.0, 0.0).astype(BF16)

    zeros_head = jnp.zeros((HEAD_DIM, qb), BF16)
    for h in range(N_HEADS):
        qh = qT_ref[h * HEAD_DIM:(h + 1) * HEAD_DIM, :]
        qpad_scr[h] = jnp.concatenate([qh, zeros_head] if h % 2 == 0 else [zeros_head, qh], axis=0)
    m_scr[...] = jnp.full((N_HEADS, qb), NEG_BIG, F32)
    l_scr[...] = jnp.zeros((N_HEADS, qb), F32)
    oT_scr[...] = jnp.zeros((ATT_W, qb), F32)

    ones_rows = jnp.ones((16, kc), BF16)

    def att_body(c, run):
        kk = key_scr[c]
        eq = kk == tau_eff
        rank = jnp.dot(ltri, jnp.where(eq, 1.0, 0.0).astype(BF16), preferred_element_type=F32)
        tie = jnp.where(rank <= need - run, 0.0, -jnp.inf)
        bias = jnp.where(kk > tau_eff, 0.0, jnp.where(eq, tie, -jnp.inf))
        row0 = pl.multiple_of(c * kc, kc)
        m_old = m_scr[...]
        l_old = l_scr[...]
        s_all = [jnp.dot(k_ref[pl.ds(row0, kc), (h // 2) * LANES:(h // 2 + 1) * LANES], qpad_scr[h],
                         preferred_element_type=F32) + bias for h in range(N_HEADS)]
        m_new, alphas, p_all = [], [], []
        for h in range(N_HEADS):
            m = m_old[h:h + 1, :]
            mn = jnp.maximum(m, jnp.max(s_all[h], axis=0, keepdims=True))
            p_all.append(jnp.exp2(s_all[h] - mn).astype(BF16))
            alphas.append(jnp.exp2(m - mn))
            m_new.append(mn)
        pv = [jnp.dot(jnp.concatenate([vT_ref[c, h * HEAD_DIM:(h + 1) * HEAD_DIM, :], ones_rows], axis=0),
                      p_all[h], preferred_element_type=F32) for h in range(N_HEADS)]
        alpha_rows = [jnp.broadcast_to(a, (HEAD_DIM, qb)) for a in alphas]
        oT_scr[...] = (jnp.concatenate(alpha_rows, axis=0) * oT_scr[...]
                       + jnp.concatenate([x[0:HEAD_DIM] for x in pv], axis=0))
        m_scr[...] = jnp.concatenate(m_new, axis=0)
        l_scr[...] = (jnp.concatenate(alphas, axis=0) * l_old
                      + jnp.concatenate([x[HEAD_DIM:HEAD_DIM + 1] for x in pv], axis=0))
        return run + rank[kc - 1:kc, :]

    lax.fori_loop(0, nk, att_body, jnp.zeros((1, qb), F32))
    for h in range(N_HEADS):
        hs = slice(h * HEAD_DIM, (h + 1) * HEAD_DIM)
        oT_scr[hs, :] = oT_scr[hs, :] / l_scr[h:h + 1, :]

    o_ref[...] = oT_scr[...].T.astype(BF16)


def _dsa(qT, qiT, kwT, k, vT, kidx, *, qb, kc):
    bsz, seq, _ = k.shape
    assert kc % qb == 0 and seq % kc == 0
    topk = min(TOPK_MAX, seq // 4)
    w_row_block = IDX_DIM // 8
    return pl.pallas_call(
        functools.partial(_dsa_kernel, kc=kc, topk=topk),
        grid=(bsz, seq // qb),
        in_specs=[
            pl.BlockSpec((None, ATT_W, qb), lambda b, i: (b, 0, i)),
            pl.BlockSpec((None, IDX_HEADS * IDX_DIM, qb), lambda b, i: (b, 0, i)),
            pl.BlockSpec((None, 8, qb), lambda b, i: (b, w_row_block, i)),
            pl.BlockSpec((None, seq, ATT_W), lambda b, i: (b, 0, 0), pipeline_mode=pl.Buffered(1)),
            pl.BlockSpec((None, seq // kc, ATT_W, kc), lambda b, i: (b, 0, 0, 0), pipeline_mode=pl.Buffered(1)),
            pl.BlockSpec((None, seq, LANES), lambda b, i: (b, 0, 0), pipeline_mode=pl.Buffered(1)),
        ],
        out_specs=pl.BlockSpec((None, qb, ATT_W), lambda b, i: (b, i, 0)),
        out_shape=jax.ShapeDtypeStruct((bsz, seq, ATT_W), BF16),
        scratch_shapes=[pltpu.VMEM((seq // kc, kc, qb), I32),
                        pltpu.VMEM((seq // kc, KEY_BITS, kc // KEY_BITS, qb), I32),
                        pltpu.VMEM((seq // kc, kc // KEY_BITS, qb), I32),
                        pltpu.VMEM((ATT_W, qb), F32),
                        pltpu.VMEM((N_HEADS, LANES, qb), BF16),
                        pltpu.VMEM((N_HEADS, qb), F32),
                        pltpu.VMEM((N_HEADS, qb), F32)],
        compiler_params=_cparams(("parallel", "parallel")),
        name="dsa_attention",
    )(qT, qiT, kwT, k, vT, kidx)


def _memkv_kernel(mem_ref, g_ref, wkv_ref, kT_ref, v_ref):
    d = mem_ref.shape[1]
    mn = _rms(mem_ref[...], g_ref[...]).astype(BF16)
    kv = jnp.dot(mn, wkv_ref[...], preferred_element_type=F32)
    kT_ref[...] = kv[:, 0:d].T.astype(BF16)
    v_ref[...] = kv[:, d:2 * d].astype(BF16)


def _memkv(mem, g, wkv):
    bsz, m, d = mem.shape
    return pl.pallas_call(
        _memkv_kernel,
        grid=(bsz,),
        in_specs=[pl.BlockSpec((None, m, d), lambda b: (b, 0, 0)),
                  pl.BlockSpec((1, d), lambda b: (0, 0)),
                  pl.BlockSpec((d, 2 * d), lambda b: (0, 0))],
        out_specs=[pl.BlockSpec((None, d, m), lambda b: (b, 0, 0)),
                   pl.BlockSpec((None, m, d), lambda b: (b, 0, 0))],
        out_shape=[jax.ShapeDtypeStruct((bsz, d, m), BF16), jax.ShapeDtypeStruct((bsz, m, d), BF16)],
        compiler_params=_cparams(("parallel",)),
        name="mem_kv",
    )(mem, g, wkv)


def _xattn_kernel(x_ref, yab_ref, yc_ref, wmix_ref, g_ref, wq_ref, kT_ref, v_ref, wo_ref, o_ref):
    d = x_ref.shape[1]
    hd = d // XA_HEADS
    n_ab = yab_ref.shape[1]
    x = (x_ref[...]
         + jnp.dot(yab_ref[...], wmix_ref[0:n_ab, :], preferred_element_type=F32)
         + jnp.dot(yc_ref[...], wmix_ref[n_ab:, :], preferred_element_type=F32))
    hq = _rms(x, g_ref[...]).astype(BF16)
    q = (jnp.dot(hq, wq_ref[...], preferred_element_type=F32) * (hd ** -0.5)).astype(BF16)
    outs = []
    for h in range(XA_HEADS):
        s = jnp.dot(q[:, h * hd:(h + 1) * hd], kT_ref[h * hd:(h + 1) * hd, :], preferred_element_type=F32)
        m = jnp.max(s, axis=-1, keepdims=True)
        p = jnp.exp(s - m)
        l = jnp.sum(p, axis=-1, keepdims=True)
        o = jnp.dot(p.astype(BF16), v_ref[:, h * hd:(h + 1) * hd], preferred_element_type=F32) / l
        outs.append(o.astype(BF16))
    o_ref[...] = x + jnp.dot(jnp.concatenate(outs, axis=-1), wo_ref[...], preferred_element_type=F32)


def _xattn(x, yab, yc, wmix, g, wq, kT, v, wo, *, bsz, seq, tm):
    d = x.shape[1]
    m = v.shape[1]
    ns = seq // tm
    tok_c = lambda c: pl.BlockSpec((tm, c), lambda b, s: (b * ns + s, 0))
    tok = tok_c(d)
    full = lambda a: pl.BlockSpec(a.shape, lambda b, s: (0, 0))
    return pl.pallas_call(
        _xattn_kernel,
        grid=(bsz, ns),
        in_specs=[tok, tok_c(yab.shape[1]), tok_c(yc.shape[1]), full(wmix),
                  pl.BlockSpec((1, d), lambda b, s: (0, 0)),
                  pl.BlockSpec((d, d), lambda b, s: (0, 0)),
                  pl.BlockSpec((None, d, m), lambda b, s: (b, 0, 0)),
                  pl.BlockSpec((None, m, d), lambda b, s: (b, 0, 0)),
                  pl.BlockSpec((d, d), lambda b, s: (0, 0))],
        out_specs=tok,
        out_shape=jax.ShapeDtypeStruct((bsz * seq, d), F32),
        compiler_params=_cparams(("parallel", "parallel")),
        name="mem_xattn",
    )(x, yab, yc, wmix, g, wq, kT, v, wo)


def _tiles(seq):
    tiles = dict(
        tm_ffn=512,
        tm_proj=512, rc_conv=64,
        qb=256, kc=512,
        tm_xa=1024, tm_rope=1024,
    )
    assert all(seq % tiles[name] == 0 for name in ("tm_ffn", "tm_proj", "qb", "kc", "tm_xa", "tm_rope")), seq
    return tiles


def kernel(x, mem, positions, ffn1_norm, ffn1_w_gate, ffn1_w_up, ffn1_w_down, mix_norm, w_in,
           conf_dw, conf_dw_b, conf_ln_g, conf_ln_b, sc_dw, w_out, xa_norm, mem_norm,
           xa_wq, xa_wkv, xa_wo, ffn2_norm, ffn2_w_gate, ffn2_w_up, ffn2_w_down, final_norm):
    bsz, seq, d = x.shape
    depth = w_in.shape[0]
    t = bsz * seq
    tl = _tiles(seq)
    bf = lambda a: a.astype(BF16)
    row = lambda a: a.reshape(1, -1)

    cos, sin = _rope_tables(positions, tl["tm_rope"])
    xf = x.reshape(t, d)
    pad_cols = ((0, 0), (0, N_IN_PAD - w_in.shape[2]))

    for l in range(depth):
        xf = _ffn(xf, row(ffn1_norm[l]), bf(ffn1_w_gate[l]), bf(ffn1_w_up[l]), bf(ffn1_w_down[l]),
                  row(final_norm), final=False, tm=tl["tm_ffn"])

        yab, qT, k, vT, qiT, kidx, kwT = _proj(
            xf, row(mix_norm[l]), jnp.pad(bf(w_in[l]), pad_cols), cos, sin,
            conf_dw[l], row(conf_dw_b[l]), row(conf_ln_g[l]), row(conf_ln_b[l]), sc_dw[l],
            bsz=bsz, seq=seq, tm=tl["tm_proj"], kc=tl["kc"], rc=tl["rc_conv"])
        yc = _dsa(qT, qiT, kwT, k, vT, kidx, qb=tl["qb"], kc=tl["kc"])

        kT_mem, v_mem = _memkv(mem, row(mem_norm[l]), bf(xa_wkv[l]))
        xf = _xattn(xf, yab.reshape(t, CONV_CH + SC_CH), yc.reshape(t, ATT_W), bf(w_out[l]),
                    row(xa_norm[l]), bf(xa_wq[l]), kT_mem, v_mem, bf(xa_wo[l]),
                    bsz=bsz, seq=seq, tm=tl["tm_xa"])

        xf = _ffn(xf, row(ffn2_norm[l]), bf(ffn2_w_gate[l]), bf(ffn2_w_up[l]), bf(ffn2_w_down[l]),
                  row(final_norm), final=(l == depth - 1), tm=tl["tm_ffn"])

    return xf.reshape(bsz, seq, d)
```

```python
import functools
import math

import jax
import jax.numpy as jnp
from jax import lax
from jax.experimental import pallas as pl
from jax.experimental.pallas import tpu as pltpu

F32 = jnp.float32
BF16 = jnp.bfloat16
I32 = jnp.int32

CONV_CH = 256
CONV_W = 31
SC_CH = 256
SC_W = 3
N_HEADS = 8
HEAD_DIM = 64
ATT_W = N_HEADS * HEAD_DIM
IDX_HEADS = 4
IDX_DIM = 64
TOPK_MAX = 256
ROPE_THETA = 10000.0
XA_HEADS = 4
IDX_SCALE = (IDX_HEADS ** -0.5) * (IDX_DIM ** -0.5)
ATT_SCALE = HEAD_DIM ** -0.5
QK_SCALE = ATT_SCALE * math.log2(math.e)

LANES = 128
SUBLANES = 8
VMEM_LIMIT = 58 * 1024 * 1024

CONV_COLS = 2 * CONV_CH + 3 * SC_CH
Q_OFF = CONV_COLS
K_OFF = Q_OFF + ATT_W
V_OFF = K_OFF + ATT_W
QI_OFF = V_OFF + ATT_W
KW_OFF = QI_OFF + IDX_HEADS * IDX_DIM
N_IN_PAD = KW_OFF + LANES

KEY_BITS = 32
INT_MIN = -2 ** 31
INT_MAX = 2 ** 31 - 1
NEG_INF_KEY = -2139095041
NEG_BIG = -1e30


def _cparams(sem, vmem=VMEM_LIMIT):
    return pltpu.CompilerParams(dimension_semantics=sem, vmem_limit_bytes=vmem)


def _rms(x, g, eps=1e-6):
    return x * lax.rsqrt(jnp.mean(x * x, axis=-1, keepdims=True) + eps) * g


def _sigmoid(x):
    return 1.0 / (1.0 + jnp.exp(-x))


def _rope_kernel(pos_ref, invf_ref, sgn_ref, cos_ref, sin_ref):
    ang = pos_ref[...].astype(F32) * invf_ref[...]
    cos_ref[...] = jnp.cos(ang)
    sin_ref[...] = jnp.sin(ang) * sgn_ref[...]


def _rope_tables(positions, tm):
    t = positions.size
    half = HEAD_DIM // 2
    inv_freq = ROPE_THETA ** (-jnp.arange(0, HEAD_DIM, 2, dtype=F32) / HEAD_DIM)
    invf = jnp.tile(inv_freq, LANES // half)[None, :]
    sgn = jnp.tile(jnp.concatenate([-jnp.ones((half,), F32), jnp.ones((half,), F32)]),
                   LANES // HEAD_DIM)[None, :]
    pos = jnp.broadcast_to(positions.reshape(t, 1), (t, LANES))
    row = pl.BlockSpec((tm, LANES), lambda i: (i, 0))
    one = pl.BlockSpec((1, LANES), lambda i: (0, 0))
    return pl.pallas_call(
        _rope_kernel,
        grid=(t // tm,),
        in_specs=[row, one, one],
        out_specs=[row, row],
        out_shape=[jax.ShapeDtypeStruct((t, LANES), F32)] * 2,
        compiler_params=_cparams(("parallel",)),
        name="rope_tables",
    )(pos, invf, sgn)


def _ffn_kernel(x_ref, g_ref, wg_ref, wu_ref, wd_ref, fg_ref, o_ref, *, final):
    x = x_ref[...]
    h = _rms(x, g_ref[...]).astype(BF16)
    a = jnp.dot(h, wg_ref[...], preferred_element_type=F32)
    b = jnp.dot(h, wu_ref[...], preferred_element_type=F32)
    t = (a * _sigmoid(a)) * b
    y = x + 0.5 * jnp.dot(t.astype(BF16), wd_ref[...], preferred_element_type=F32)
    if final:
        y = _rms(y, fg_ref[...])
    o_ref[...] = y


def _ffn(x, g, wg, wu, wd, fg, *, final, tm):
    t, d = x.shape
    f = wg.shape[1]
    resident = lambda r, c: pl.BlockSpec((r, c), lambda i: (0, 0), pipeline_mode=pl.Buffered(1))
    return pl.pallas_call(
        functools.partial(_ffn_kernel, final=final),
        grid=(t // tm,),
        in_specs=[
            pl.BlockSpec((tm, d), lambda i: (i, 0)),
            pl.BlockSpec((1, d), lambda i: (0, 0)),
            resident(d, f), resident(d, f), resident(f, d),
            pl.BlockSpec((1, d), lambda i: (0, 0)),
        ],
        out_specs=pl.BlockSpec((tm, d), lambda i: (i, 0)),
        out_shape=jax.ShapeDtypeStruct((t, d), F32),
        compiler_params=_cparams(("parallel",)),
        name="ffn_final" if final else "ffn",
    )(x, g, wg, wu, wd, fg)


CONV_HALO = 32
SC_HALO = 8


def _conv_mixers(cv, dw_ref, dwb_ref, lng_ref, lnb_ref, sw_ref, yab_ref, ha_scr, ub_scr, sh_scr, *, rc):
    ts = cv.shape[0]
    sub = SUBLANES
    first_tap = CONV_HALO - (CONV_W - 1)

    b_off = 2 * CONV_CH
    ha_scr[CONV_HALO:CONV_HALO + ts, :] = cv[:, 0:CONV_CH] * _sigmoid(cv[:, CONV_CH:2 * CONV_CH])
    ub_scr[SC_HALO:SC_HALO + ts, :] = cv[:, b_off + SC_CH:b_off + 2 * SC_CH] * cv[:, b_off + 2 * SC_CH:b_off + 3 * SC_CH]
    span = sh_scr.shape[1]
    for ph in range(1, sub):
        sh_scr[ph - 1] = ha_scr[ph:ph + span, :]

    for r in range(ts // rc):
        base = r * rc
        acc = jnp.zeros((rc, CONV_CH), F32)
        for j in range(CONV_W):
            off = base + first_tap + j
            ph = off % sub
            src = ha_scr[off:off + rc, :] if ph == 0 else sh_scr[ph - 1, off - ph:off - ph + rc, :]
            acc = acc + src * dw_ref[j:j + 1, :]
        acc = acc + dwb_ref[...]
        mu = jnp.mean(acc, axis=-1, keepdims=True)
        cen = acc - mu
        var = jnp.mean(cen * cen, axis=-1, keepdims=True)
        y = cen * lax.rsqrt(var + 1e-5) * lng_ref[...] + lnb_ref[...]
        yab_ref[base:base + rc, 0:CONV_CH] = (y * _sigmoid(y)).astype(BF16)

        accb = jnp.zeros((rc, SC_CH), F32)
        for j in range(SC_W):
            off = base + SC_HALO - (SC_W - 1) + j
            accb = accb + ub_scr[off:off + rc, :] * sw_ref[j:j + 1, :]
        yab_ref[base:base + rc, CONV_CH:CONV_CH + SC_CH] = (cv[base:base + rc, b_off:b_off + SC_CH] * accb).astype(BF16)

    ha_scr[0:CONV_HALO, :] = ha_scr[ts:ts + CONV_HALO, :]
    ub_scr[0:SC_HALO, :] = ub_scr[ts:ts + SC_HALO, :]


def _proj_kernel(x_ref, g_ref, w_ref, cos_ref, sin_ref, dw_ref, dwb_ref, lng_ref, lnb_ref, sw_ref,
                 yab_ref, qT_ref, k_ref, vT_ref, qiT_ref, kidx_ref, kwT_ref,
                 ha_scr, ub_scr, sh_scr, *, kc, rc):
    tm = x_ref.shape[0]

    @pl.when(pl.program_id(1) == 0)
    def _():
        ha_scr[0:CONV_HALO, :] = jnp.zeros((CONV_HALO, CONV_CH), F32)
        ub_scr[0:SC_HALO, :] = jnp.zeros((SC_HALO, SC_CH), F32)

    h = _rms(x_ref[...], g_ref[...]).astype(BF16)
    cos = cos_ref[...]
    sin = sin_ref[...]
    lane = lax.broadcasted_iota(I32, (tm, LANES), 1)
    first_half = (lane % HEAD_DIM) < (HEAD_DIM // 2)

    def rope(t, c, s):
        rot = jnp.where(first_half, pltpu.roll(t, LANES - HEAD_DIM // 2, 1), pltpu.roll(t, HEAD_DIM // 2, 1))
        return t * c + rot * s

    cv = jnp.dot(h, w_ref[:, 0:CONV_COLS], preferred_element_type=F32)
    _conv_mixers(cv, dw_ref, dwb_ref, lng_ref, lnb_ref, sw_ref, yab_ref, ha_scr, ub_scr, sh_scr, rc=rc)

    q = jnp.dot(h, w_ref[:, Q_OFF:Q_OFF + ATT_W], preferred_element_type=F32)
    for g in range(ATT_W // LANES):
        qr = rope(q[:, g * LANES:(g + 1) * LANES], cos, sin) * QK_SCALE
        qT_ref[g * LANES:(g + 1) * LANES, :] = qr.T.astype(BF16)

    k = jnp.dot(h, w_ref[:, K_OFF:K_OFF + ATT_W], preferred_element_type=F32)
    for g in range(ATT_W // LANES):
        k_ref[:, g * LANES:(g + 1) * LANES] = rope(k[:, g * LANES:(g + 1) * LANES], cos, sin).astype(BF16)

    v = jnp.dot(h, w_ref[:, V_OFF:V_OFF + ATT_W], preferred_element_type=F32)
    for c in range(tm // kc):
        vT_ref[c] = v[c * kc:(c + 1) * kc, :].T.astype(BF16)

    qi = jnp.dot(h, w_ref[:, QI_OFF:QI_OFF + IDX_HEADS * IDX_DIM], preferred_element_type=F32)
    for g in range(IDX_HEADS * IDX_DIM // LANES):
        qiT_ref[g * LANES:(g + 1) * LANES, :] = rope(qi[:, g * LANES:(g + 1) * LANES], cos, sin).T.astype(BF16)

    kw = jnp.dot(h, w_ref[:, KW_OFF:KW_OFF + LANES], preferred_element_type=F32)
    is_kidx = lane < IDX_DIM
    kw = rope(kw, jnp.where(is_kidx, cos, 1.0), jnp.where(is_kidx, sin, 0.0))
    kidx_ref[...] = kw.astype(BF16)
    kwT_ref[...] = kw.T


def _proj(x, g, w, cos, sin, dw, dwb, lng, lnb, sw, *, bsz, seq, tm, kc, rc):
    d = x.shape[1]
    ns = seq // tm
    tok = lambda b, s: (b * ns + s, 0)
    small = lambda r, c: pl.BlockSpec((r, c), lambda b, s: (0, 0))
    return pl.pallas_call(
        functools.partial(_proj_kernel, kc=kc, rc=rc),
        grid=(bsz, ns),
        in_specs=[
            pl.BlockSpec((tm, d), tok),
            small(1, d),
            small(d, N_IN_PAD),
            pl.BlockSpec((tm, LANES), tok),
            pl.BlockSpec((tm, LANES), tok),
            small(CONV_W, CONV_CH), small(1, CONV_CH), small(1, CONV_CH), small(1, CONV_CH),
            small(SC_W, SC_CH),
        ],
        out_specs=[
            pl.BlockSpec((None, tm, CONV_CH + SC_CH), lambda b, s: (b, s, 0)),
            pl.BlockSpec((None, ATT_W, tm), lambda b, s: (b, 0, s)),
            pl.BlockSpec((None, tm, ATT_W), lambda b, s: (b, s, 0)),
            pl.BlockSpec((None, tm // kc, ATT_W, kc), lambda b, s: (b, s, 0, 0)),
            pl.BlockSpec((None, IDX_HEADS * IDX_DIM, tm), lambda b, s: (b, 0, s)),
            pl.BlockSpec((None, tm, LANES), lambda b, s: (b, s, 0)),
            pl.BlockSpec((None, LANES, tm), lambda b, s: (b, 0, s)),
        ],
        out_shape=[
            jax.ShapeDtypeStruct((bsz, seq, CONV_CH + SC_CH), BF16),
            jax.ShapeDtypeStruct((bsz, ATT_W, seq), BF16),
            jax.ShapeDtypeStruct((bsz, seq, ATT_W), BF16),
            jax.ShapeDtypeStruct((bsz, seq // kc, ATT_W, kc), BF16),
            jax.ShapeDtypeStruct((bsz, IDX_HEADS * IDX_DIM, seq), BF16),
            jax.ShapeDtypeStruct((bsz, seq, LANES), BF16),
            jax.ShapeDtypeStruct((bsz, LANES, seq), F32),
        ],
        scratch_shapes=[pltpu.VMEM((tm + CONV_HALO, CONV_CH), F32),
                        pltpu.VMEM((tm + SC_HALO, SC_CH), F32),
                        pltpu.VMEM((SUBLANES - 1, tm + CONV_HALO - SUBLANES, CONV_CH), F32)],
        compiler_params=_cparams(("arbitrary", "arbitrary")),
        name="in_proj",
    )(x, g, w, cos, sin, dw, dwb, lng, lnb, sw)


def _sortable(x):
    b = lax.bitcast_convert_type(x, I32)
    return b ^ ((b >> 31) & INT_MAX)


def _transpose_bits(rows):
    rows = list(rows)
    j, m = KEY_BITS // 2, 0x0000FFFF
    while j:
        k = 0
        while k < KEY_BITS:
            t = (rows[k] ^ lax.shift_right_logical(rows[k + j], jnp.int32(j))) & m
            rows[k] = rows[k] ^ t
            rows[k + j] = rows[k + j] ^ (t << j)
            k = (k + j + 1) & ~j
        j >>= 1
        m = (m ^ (m << j)) & 0xFFFFFFFF
    return rows


def _dsa_kernel(qT_ref, qiT_ref, wT_ref, k_ref, vT_ref, kidx_ref, o_ref,
                planes_scr, alive_scr, gt_scr, oT_scr, qpad_scr, m_scr, l_scr, *, kc, topk):
    qb = qT_ref.shape[1]
    i = pl.program_id(1)
    nk = (i * qb) // kc + 1
    q_pos = i * qb + lax.broadcasted_iota(I32, (1, qb), 1)
    row_iota = lax.broadcasted_iota(I32, (kc, qb), 0)

    grp = kc // KEY_BITS
    w = wT_ref[...] * IDX_SCALE
    zeros_half = jnp.zeros((LANES - IDX_DIM, qb), BF16)
    qi = [jnp.concatenate([qiT_ref[h * IDX_DIM:(h + 1) * IDX_DIM, :], zeros_half], axis=0)
          for h in range(IDX_HEADS)]

    def score_chunk(c, diagonal):
        kcs = kidx_ref[pl.ds(pl.multiple_of(c * kc, kc), kc), :]
        sc = jnp.zeros((kc, qb), F32)
        for h in range(IDX_HEADS):
            lg = jnp.dot(kcs, qi[h], preferred_element_type=F32)
            sc = sc + w[h:h + 1, :] * jnp.maximum(lg, 0.0)
        if diagonal:
            sc = jnp.where((c * kc + row_iota) <= q_pos, sc, -jnp.inf)
        u = _sortable(sc) ^ INT_MIN
        planes = _transpose_bits([u[a * grp:(a + 1) * grp, :] for a in range(KEY_BITS)])
        for b in range(KEY_BITS):
            planes_scr[c, b] = planes[b]
        alive_scr[c] = jnp.full((grp, qb), -1, I32)
        gt_scr[c] = jnp.zeros((grp, qb), I32)

    def score_body(c, carry):
        score_chunk(c, False)
        return carry

    lax.fori_loop(0, nk - 1, score_body, 0)
    score_chunk(nk - 1, True)

    def decide(b, acc, cnt_gt, prefix):
        cnt_b = jnp.sum(acc, axis=0, keepdims=True)
        take = (cnt_gt + cnt_b) >= topk
        bit = jnp.left_shift(jnp.int32(1), KEY_BITS - 1 - b)
        return (jnp.where(take, cnt_gt, cnt_gt + cnt_b), prefix | jnp.where(take, bit, 0),
                jnp.where(take, 0, -1) + jnp.zeros((grp, qb), I32))

    def settle(c, b, drop):
        alive = alive_scr[c]
        plane = planes_scr[c, b]
        gt_scr[c] = gt_scr[c] | (alive & plane & drop)
        alive = alive & (plane ^ drop)
        alive_scr[c] = alive
        return alive

    def first_count(c, acc):
        return acc + lax.population_count(planes_scr[c, 0])

    zero_row = jnp.zeros((1, qb), I32)
    acc0 = lax.fori_loop(0, nk, first_count, jnp.zeros((grp, qb), I32))
    state = decide(0, acc0, zero_row, zero_row)

    def bit_step(b, state):
        cnt_gt, prefix, drop = state

        def body(c, acc):
            return acc + lax.population_count(settle(c, b - 1, drop) & planes_scr[c, b])

        acc = lax.fori_loop(0, nk, body, jnp.zeros((grp, qb), I32))
        return decide(b, acc, cnt_gt, prefix)

    cnt_gt, prefix, drop = lax.fori_loop(1, KEY_BITS, bit_step, state)

    def last_settle(c, carry):
        settle(c, KEY_BITS - 1, drop)
        return carry

    lax.fori_loop(0, nk, last_settle, 0)

    tau = prefix ^ INT_MIN
    need = jnp.where(tau > NEG_INF_KEY, topk - cnt_gt, 0).astype(F32)
    ltri = jnp.where(lax.broadcasted_iota(I32, (kc, kc), 0) >= lax.broadcasted_iota(I32, (kc, kc), 1),
                     1.0, 0.0).astype(BF16)

    zeros_head = jnp.zeros((HEAD_DIM, qb), BF16)
    for h in range(N_HEADS):
        qh = qT_ref[h * HEAD_DIM:(h + 1) * HEAD_DIM, :]
        qpad_scr[h] = jnp.concatenate([qh, zeros_head] if h % 2 == 0 else [zeros_head, qh], axis=0)
    m_scr[...] = jnp.full((N_HEADS, qb), NEG_BIG, F32)
    l_scr[...] = jnp.zeros((N_HEADS, qb), F32)
    oT_scr[...] = jnp.zeros((ATT_W, qb), F32)

    ones_rows = jnp.ones((16, kc), BF16)

    def att_body(c, run):
        gt_w = gt_scr[c]
        eq_w = alive_scr[c]
        above, equal = [], []
        for a in range(KEY_BITS):
            bit = INT_MIN if a == 0 else 1 << (KEY_BITS - 1 - a)
            above.append(jnp.where((gt_w & bit) != 0, 0.0, -jnp.inf))
            equal.append(jnp.where((eq_w & bit) != 0, 1.0, 0.0))
        equal = jnp.concatenate(equal, axis=0)
        rank = jnp.dot(ltri, equal.astype(BF16), preferred_element_type=F32)
        tie = jnp.where(rank <= need - run, 0.0, -jnp.inf)
        bias = jnp.where(equal > 0.0, tie, jnp.concatenate(above, axis=0))
        row0 = pl.multiple_of(c * kc, kc)
        m_old = m_scr[...]
        l_old = l_scr[...]
        s_all = [jnp.dot(k_ref[pl.ds(row0, kc), (h // 2) * LANES:(h // 2 + 1) * LANES], qpad_scr[h],
                         preferred_element_type=F32) + bias for h in range(N_HEADS)]
        m_new, alphas, p_all = [], [], []
        for h in range(N_HEADS):
            m = m_old[h:h + 1, :]
            mn = jnp.maximum(m, jnp.max(s_all[h], axis=0, keepdims=True))
            p_all.append(jnp.exp2(s_all[h] - mn).astype(BF16))
            alphas.append(jnp.exp2(m - mn))
            m_new.append(mn)
        pv = [jnp.dot(jnp.concatenate([vT_ref[c, h * HEAD_DIM:(h + 1) * HEAD_DIM, :], ones_rows], axis=0),
                      p_all[h], preferred_element_type=F32) for h in range(N_HEADS)]
        alpha_rows = [jnp.broadcast_to(a, (HEAD_DIM, qb)) for a in alphas]
        oT_scr[...] = (jnp.concatenate(alpha_rows, axis=0) * oT_scr[...]
                       + jnp.concatenate([x[0:HEAD_DIM] for x in pv], axis=0))
        m_scr[...] = jnp.concatenate(m_new, axis=0)
        l_scr[...] = (jnp.concatenate(alphas, axis=0) * l_old
                      + jnp.concatenate([x[HEAD_DIM:HEAD_DIM + 1] for x in pv], axis=0))
        return run + rank[kc - 1:kc, :]

    lax.fori_loop(0, nk, att_body, jnp.zeros((1, qb), F32))
    for h in range(N_HEADS):
        hs = slice(h * HEAD_DIM, (h + 1) * HEAD_DIM)
        oT_scr[hs, :] = oT_scr[hs, :] / l_scr[h:h + 1, :]

    o_ref[...] = oT_scr[...].T.astype(BF16)


def _dsa(qT, qiT, kwT, k, vT, kidx, *, qb, kc):
    bsz, seq, _ = k.shape
    assert kc % qb == 0 and seq % kc == 0
    topk = min(TOPK_MAX, seq // 4)
    w_row_block = IDX_DIM // 8
    return pl.pallas_call(
        functools.partial(_dsa_kernel, kc=kc, topk=topk),
        grid=(bsz, seq // qb),
        in_specs=[
            pl.BlockSpec((None, ATT_W, qb), lambda b, i: (b, 0, i)),
            pl.BlockSpec((None, IDX_HEADS * IDX_DIM, qb), lambda b, i: (b, 0, i)),
            pl.BlockSpec((None, 8, qb), lambda b, i: (b, w_row_block, i)),
            pl.BlockSpec((None, seq, ATT_W), lambda b, i: (b, 0, 0), pipeline_mode=pl.Buffered(1)),
            pl.BlockSpec((None, seq // kc, ATT_W, kc), lambda b, i: (b, 0, 0, 0), pipeline_mode=pl.Buffered(1)),
            pl.BlockSpec((None, seq, LANES), lambda b, i: (b, 0, 0), pipeline_mode=pl.Buffered(1)),
        ],
        out_specs=pl.BlockSpec((None, qb, ATT_W), lambda b, i: (b, i, 0)),
        out_shape=jax.ShapeDtypeStruct((bsz, seq, ATT_W), BF16),
        scratch_shapes=[pltpu.VMEM((seq // kc, KEY_BITS, kc // KEY_BITS, qb), I32),
                        pltpu.VMEM((seq // kc, kc // KEY_BITS, qb), I32),
                        pltpu.VMEM((seq // kc, kc // KEY_BITS, qb), I32),
                        pltpu.VMEM((ATT_W, qb), F32),
                        pltpu.VMEM((N_HEADS, LANES, qb), BF16),
                        pltpu.VMEM((N_HEADS, qb), F32),
                        pltpu.VMEM((N_HEADS, qb), F32)],
        compiler_params=_cparams(("parallel", "parallel")),
        name="dsa_attention",
    )(qT, qiT, kwT, k, vT, kidx)


def _memkv_kernel(mem_ref, g_ref, wkv_ref, kT_ref, v_ref):
    d = mem_ref.shape[1]
    mn = _rms(mem_ref[...], g_ref[...]).astype(BF16)
    kv = jnp.dot(mn, wkv_ref[...], preferred_element_type=F32)
    kT_ref[...] = kv[:, 0:d].T.astype(BF16)
    v_ref[...] = kv[:, d:2 * d].astype(BF16)


def _memkv(mem, g, wkv):
    bsz, m, d = mem.shape
    return pl.pallas_call(
        _memkv_kernel,
        grid=(bsz,),
        in_specs=[pl.BlockSpec((None, m, d), lambda b: (b, 0, 0)),
                  pl.BlockSpec((1, d), lambda b: (0, 0)),
                  pl.BlockSpec((d, 2 * d), lambda b: (0, 0))],
        out_specs=[pl.BlockSpec((None, d, m), lambda b: (b, 0, 0)),
                   pl.BlockSpec((None, m, d), lambda b: (b, 0, 0))],
        out_shape=[jax.ShapeDtypeStruct((bsz, d, m), BF16), jax.ShapeDtypeStruct((bsz, m, d), BF16)],
        compiler_params=_cparams(("parallel",)),
        name="mem_kv",
    )(mem, g, wkv)


def _xattn_kernel(x_ref, yab_ref, yc_ref, wmix_ref, g_ref, wq_ref, kT_ref, v_ref, wo_ref, o_ref):
    d = x_ref.shape[1]
    hd = d // XA_HEADS
    n_ab = yab_ref.shape[1]
    x = (x_ref[...]
         + jnp.dot(yab_ref[...], wmix_ref[0:n_ab, :], preferred_element_type=F32)
         + jnp.dot(yc_ref[...], wmix_ref[n_ab:, :], preferred_element_type=F32))
    hq = _rms(x, g_ref[...]).astype(BF16)
    q = (jnp.dot(hq, wq_ref[...], preferred_element_type=F32) * (hd ** -0.5)).astype(BF16)
    outs = []
    for h in range(XA_HEADS):
        s = jnp.dot(q[:, h * hd:(h + 1) * hd], kT_ref[h * hd:(h + 1) * hd, :], preferred_element_type=F32)
        m = jnp.max(s, axis=-1, keepdims=True)
        p = jnp.exp(s - m)
        l = jnp.sum(p, axis=-1, keepdims=True)
        o = jnp.dot(p.astype(BF16), v_ref[:, h * hd:(h + 1) * hd], preferred_element_type=F32) / l
        outs.append(o.astype(BF16))
    o_ref[...] = x + jnp.dot(jnp.concatenate(outs, axis=-1), wo_ref[...], preferred_element_type=F32)


def _xattn(x, yab, yc, wmix, g, wq, kT, v, wo, *, bsz, seq, tm):
    d = x.shape[1]
    m = v.shape[1]
    ns = seq // tm
    tok_c = lambda c: pl.BlockSpec((tm, c), lambda b, s: (b * ns + s, 0))
    tok = tok_c(d)
    full = lambda a: pl.BlockSpec(a.shape, lambda b, s: (0, 0))
    return pl.pallas_call(
        _xattn_kernel,
        grid=(bsz, ns),
        in_specs=[tok, tok_c(yab.shape[1]), tok_c(yc.shape[1]), full(wmix),
                  pl.BlockSpec((1, d), lambda b, s: (0, 0)),
                  pl.BlockSpec((d, d), lambda b, s: (0, 0)),
                  pl.BlockSpec((None, d, m), lambda b, s: (b, 0, 0)),
                  pl.BlockSpec((None, m, d), lambda b, s: (b, 0, 0)),
                  pl.BlockSpec((d, d), lambda b, s: (0, 0))],
        out_specs=tok,
        out_shape=jax.ShapeDtypeStruct((bsz * seq, d), F32),
        compiler_params=_cparams(("parallel", "parallel")),
        name="mem_xattn",
    )(x, yab, yc, wmix, g, wq, kT, v, wo)


def _tiles(seq):
    tiles = dict(
        tm_ffn=512,
        tm_proj=512, rc_conv=64,
        qb=512, kc=512,
        tm_xa=1024, tm_rope=1024,
    )
    assert all(seq % tiles[name] == 0 for name in ("tm_ffn", "tm_proj", "qb", "kc", "tm_xa", "tm_rope")), seq
    return tiles


def kernel(x, mem, positions, ffn1_norm, ffn1_w_gate, ffn1_w_up, ffn1_w_down, mix_norm, w_in,
           conf_dw, conf_dw_b, conf_ln_g, conf_ln_b, sc_dw, w_out, xa_norm, mem_norm,
           xa_wq, xa_wkv, xa_wo, ffn2_norm, ffn2_w_gate, ffn2_w_up, ffn2_w_down, final_norm):
    bsz, seq, d = x.shape
    depth = w_in.shape[0]
    t = bsz * seq
    tl = _tiles(seq)
    bf = lambda a: a.astype(BF16)
    row = lambda a: a.reshape(1, -1)

    cos, sin = _rope_tables(positions, tl["tm_rope"])
    xf = x.reshape(t, d)
    pad_cols = ((0, 0), (0, N_IN_PAD - w_in.shape[2]))

    for l in range(depth):
        xf = _ffn(xf, row(ffn1_norm[l]), bf(ffn1_w_gate[l]), bf(ffn1_w_up[l]), bf(ffn1_w_down[l]),
                  row(final_norm), final=False, tm=tl["tm_ffn"])

        yab, qT, k, vT, qiT, kidx, kwT = _proj(
            xf, row(mix_norm[l]), jnp.pad(bf(w_in[l]), pad_cols), cos, sin,
            conf_dw[l], row(conf_dw_b[l]), row(conf_ln_g[l]), row(conf_ln_b[l]), sc_dw[l],
            bsz=bsz, seq=seq, tm=tl["tm_proj"], kc=tl["kc"], rc=tl["rc_conv"])
        yc = _dsa(qT, qiT, kwT, k, vT, kidx, qb=tl["qb"], kc=tl["kc"])

        kT_mem, v_mem = _memkv(mem, row(mem_norm[l]), bf(xa_wkv[l]))
        xf = _xattn(xf, yab.reshape(t, CONV_CH + SC_CH), yc.reshape(t, ATT_W), bf(w_out[l]),
                    row(xa_norm[l]), bf(xa_wq[l]), kT_mem, v_mem, bf(xa_wo[l]),
                    bsz=bsz, seq=seq, tm=tl["tm_xa"])

        xf = _ffn(xf, row(ffn2_norm[l]), bf(ffn2_w_gate[l]), bf(ffn2_w_up[l]), bf(ffn2_w_down[l]),
                  row(final_norm), final=(l == depth - 1), tm=tl["tm_ffn"])

    return xf.reshape(bsz, seq, d)
```

```python
import functools
import math

import jax
import jax.numpy as jnp
from jax import lax
from jax.experimental import pallas as pl
from jax.experimental.pallas import tpu as pltpu

F32 = jnp.float32
BF16 = jnp.bfloat16
I32 = jnp.int32

CONV_CH = 256
CONV_W = 31
SC_CH = 256
SC_W = 3
N_HEADS = 8
HEAD_DIM = 64
ATT_W = N_HEADS * HEAD_DIM
IDX_HEADS = 4
IDX_DIM = 64
TOPK_MAX = 256
ROPE_THETA = 10000.0
XA_HEADS = 4
IDX_SCALE = (IDX_HEADS ** -0.5) * (IDX_DIM ** -0.5)
ATT_SCALE = HEAD_DIM ** -0.5
QK_SCALE = ATT_SCALE * math.log2(math.e)

LANES = 128
SUBLANES = 8
VMEM_LIMIT = 58 * 1024 * 1024

CONV_COLS = 2 * CONV_CH + 3 * SC_CH
Q_OFF = CONV_COLS
K_OFF = Q_OFF + ATT_W
V_OFF = K_OFF + ATT_W
QI_OFF = V_OFF + ATT_W
KW_OFF = QI_OFF + IDX_HEADS * IDX_DIM
N_IN_PAD = KW_OFF + LANES

KEY_BITS = 32
INT_MIN = -2 ** 31
NEG_INF_KEY = -2139095041
NEG_BIG = -1e30


def _cparams(sem, vmem=VMEM_LIMIT, fuse=None):
    return pltpu.CompilerParams(dimension_semantics=sem, vmem_limit_bytes=vmem, allow_input_fusion=fuse)


def _rms(x, g, eps=1e-6):
    return x * lax.rsqrt(jnp.mean(x * x, axis=-1, keepdims=True) + eps) * g


def _sigmoid(x):
    return 1.0 / (1.0 + jnp.exp(-x))


def _rope_kernel(pos_ref, invf_ref, sgn_ref, cos_ref, sin_ref):
    ang = pos_ref[...].astype(F32) * invf_ref[...]
    cos_ref[...] = jnp.cos(ang)
    sin_ref[...] = jnp.sin(ang) * sgn_ref[...]


def _rope_tables(positions, tm):
    t = positions.size
    half = HEAD_DIM // 2
    inv_freq = ROPE_THETA ** (-jnp.arange(0, HEAD_DIM, 2, dtype=F32) / HEAD_DIM)
    invf = jnp.tile(inv_freq, LANES // half)[None, :]
    sgn = jnp.tile(jnp.concatenate([-jnp.ones((half,), F32), jnp.ones((half,), F32)]),
                   LANES // HEAD_DIM)[None, :]
    pos = jnp.broadcast_to(positions.reshape(t, 1), (t, LANES))
    row = pl.BlockSpec((tm, LANES), lambda i: (i, 0))
    one = pl.BlockSpec((1, LANES), lambda i: (0, 0))
    return pl.pallas_call(
        _rope_kernel,
        grid=(t // tm,),
        in_specs=[row, one, one],
        out_specs=[row, row],
        out_shape=[jax.ShapeDtypeStruct((t, LANES), F32)] * 2,
        compiler_params=_cparams(("parallel",), fuse=[True, False, False]),
        name="rope_tables",
    )(pos, invf, sgn)


def _ffn_kernel(x_ref, g_ref, wg_ref, wu_ref, wd_ref, fg_ref, o_ref, *, final):
    x = x_ref[...]
    h = _rms(x, g_ref[...]).astype(BF16)
    a = jnp.dot(h, wg_ref[...], preferred_element_type=F32)
    b = jnp.dot(h, wu_ref[...], preferred_element_type=F32)
    t = (a * _sigmoid(a)) * b
    y = x + 0.5 * jnp.dot(t.astype(BF16), wd_ref[...], preferred_element_type=F32)
    if final:
        y = _rms(y, fg_ref[...])
    o_ref[...] = y


def _ffn(x, g, wg, wu, wd, fg, *, final, tm):
    t, d = x.shape
    f = wg.shape[1]
    resident = lambda r, c: pl.BlockSpec((r, c), lambda i: (0, 0), pipeline_mode=pl.Buffered(1))
    return pl.pallas_call(
        functools.partial(_ffn_kernel, final=final),
        grid=(t // tm,),
        in_specs=[
            pl.BlockSpec((tm, d), lambda i: (i, 0)),
            pl.BlockSpec((1, d), lambda i: (0, 0)),
            resident(d, f), resident(d, f), resident(f, d),
            pl.BlockSpec((1, d), lambda i: (0, 0)),
        ],
        out_specs=pl.BlockSpec((tm, d), lambda i: (i, 0)),
        out_shape=jax.ShapeDtypeStruct((t, d), F32),
        compiler_params=_cparams(("parallel",), fuse=[False, False, True, True, True, False]),
        name="ffn_final" if final else "ffn",
    )(x, g, wg, wu, wd, fg)


CONV_HALO = 32
SC_HALO = 8


def _conv_mixers(cv, dw_ref, dwb_ref, lng_ref, lnb_ref, sw_ref, yab_ref, ha_scr, ub_scr, sh_scr, *, rc):
    ts = cv.shape[0]
    sub = SUBLANES
    first_tap = CONV_HALO - (CONV_W - 1)

    b_off = 2 * CONV_CH
    ha_scr[CONV_HALO:CONV_HALO + ts, :] = cv[:, 0:CONV_CH] * _sigmoid(cv[:, CONV_CH:2 * CONV_CH])
    ub_scr[SC_HALO:SC_HALO + ts, :] = cv[:, b_off + SC_CH:b_off + 2 * SC_CH] * cv[:, b_off + 2 * SC_CH:b_off + 3 * SC_CH]
    span = sh_scr.shape[1]
    for ph in range(1, sub):
        sh_scr[ph - 1] = ha_scr[ph:ph + span, :]

    for r in range(ts // rc):
        base = r * rc
        acc = jnp.zeros((rc, CONV_CH), F32)
        for j in range(CONV_W):
            off = base + first_tap + j
            ph = off % sub
            src = ha_scr[off:off + rc, :] if ph == 0 else sh_scr[ph - 1, off - ph:off - ph + rc, :]
            acc = acc + src * dw_ref[j:j + 1, :]
        acc = acc + dwb_ref[...]
        mu = jnp.mean(acc, axis=-1, keepdims=True)
        cen = acc - mu
        var = jnp.mean(cen * cen, axis=-1, keepdims=True)
        y = cen * lax.rsqrt(var + 1e-5) * lng_ref[...] + lnb_ref[...]
        yab_ref[base:base + rc, 0:CONV_CH] = (y * _sigmoid(y)).astype(BF16)

        accb = jnp.zeros((rc, SC_CH), F32)
        for j in range(SC_W):
            off = base + SC_HALO - (SC_W - 1) + j
            accb = accb + ub_scr[off:off + rc, :] * sw_ref[j:j + 1, :]
        yab_ref[base:base + rc, CONV_CH:CONV_CH + SC_CH] = (cv[base:base + rc, b_off:b_off + SC_CH] * accb).astype(BF16)

    ha_scr[0:CONV_HALO, :] = ha_scr[ts:ts + CONV_HALO, :]
    ub_scr[0:SC_HALO, :] = ub_scr[ts:ts + SC_HALO, :]


def _proj_kernel(x_ref, g_ref, w_ref, cos_ref, sin_ref, dw_ref, dwb_ref, lng_ref, lnb_ref, sw_ref,
                 yab_ref, qT_ref, k_ref, vT_ref, qiT_ref, kidx_ref, kwT_ref,
                 ha_scr, ub_scr, sh_scr, *, kc, rc):
    tm = x_ref.shape[0]

    @pl.when(pl.program_id(1) == 0)
    def _():
        ha_scr[0:CONV_HALO, :] = jnp.zeros((CONV_HALO, CONV_CH), F32)
        ub_scr[0:SC_HALO, :] = jnp.zeros((SC_HALO, SC_CH), F32)

    h = _rms(x_ref[...], g_ref[...]).astype(BF16)
    cos = cos_ref[...]
    sin = sin_ref[...]
    lane = lax.broadcasted_iota(I32, (tm, LANES), 1)
    first_half = (lane % HEAD_DIM) < (HEAD_DIM // 2)

    def rope(t, c, s):
        rot = jnp.where(first_half, pltpu.roll(t, LANES - HEAD_DIM // 2, 1), pltpu.roll(t, HEAD_DIM // 2, 1))
        return t * c + rot * s

    cv = jnp.dot(h, w_ref[:, 0:CONV_COLS], preferred_element_type=F32)
    _conv_mixers(cv, dw_ref, dwb_ref, lng_ref, lnb_ref, sw_ref, yab_ref, ha_scr, ub_scr, sh_scr, rc=rc)

    q = jnp.dot(h, w_ref[:, Q_OFF:Q_OFF + ATT_W], preferred_element_type=F32)
    for g in range(ATT_W // LANES):
        qr = rope(q[:, g * LANES:(g + 1) * LANES], cos, sin) * QK_SCALE
        qT_ref[g * LANES:(g + 1) * LANES, :] = qr.T.astype(BF16)

    k = jnp.dot(h, w_ref[:, K_OFF:K_OFF + ATT_W], preferred_element_type=F32)
    for g in range(ATT_W // LANES):
        k_ref[:, g * LANES:(g + 1) * LANES] = rope(k[:, g * LANES:(g + 1) * LANES], cos, sin).astype(BF16)

    v = jnp.dot(h, w_ref[:, V_OFF:V_OFF + ATT_W], preferred_element_type=F32)
    for c in range(tm // kc):
        vT_ref[c] = v[c * kc:(c + 1) * kc, :].T.astype(BF16)

    qi = jnp.dot(h, w_ref[:, QI_OFF:QI_OFF + IDX_HEADS * IDX_DIM], preferred_element_type=F32)
    for g in range(IDX_HEADS * IDX_DIM // LANES):
        qiT_ref[g * LANES:(g + 1) * LANES, :] = rope(qi[:, g * LANES:(g + 1) * LANES], cos, sin).T.astype(BF16)

    kw = jnp.dot(h, w_ref[:, KW_OFF:KW_OFF + LANES], preferred_element_type=F32)
    is_kidx = lane < IDX_DIM
    kw = rope(kw, jnp.where(is_kidx, cos, 1.0), jnp.where(is_kidx, sin, 0.0))
    kidx_ref[...] = kw.astype(BF16)
    kwT_ref[...] = kw.T


def _proj(x, g, w, cos, sin, dw, dwb, lng, lnb, sw, *, bsz, seq, tm, kc, rc):
    d = x.shape[1]
    ns = seq // tm
    tok = lambda b, s: (b * ns + s, 0)
    small = lambda r, c: pl.BlockSpec((r, c), lambda b, s: (0, 0))
    return pl.pallas_call(
        functools.partial(_proj_kernel, kc=kc, rc=rc),
        grid=(bsz, ns),
        in_specs=[
            pl.BlockSpec((tm, d), tok),
            small(1, d),
            small(d, N_IN_PAD),
            pl.BlockSpec((tm, LANES), tok),
            pl.BlockSpec((tm, LANES), tok),
            small(CONV_W, CONV_CH), small(1, CONV_CH), small(1, CONV_CH), small(1, CONV_CH),
            small(SC_W, SC_CH),
        ],
        out_specs=[
            pl.BlockSpec((None, tm, CONV_CH + SC_CH), lambda b, s: (b, s, 0)),
            pl.BlockSpec((None, ATT_W, tm), lambda b, s: (b, 0, s)),
            pl.BlockSpec((None, tm, ATT_W), lambda b, s: (b, s, 0)),
            pl.BlockSpec((None, tm // kc, ATT_W, kc), lambda b, s: (b, s, 0, 0)),
            pl.BlockSpec((None, IDX_HEADS * IDX_DIM, tm), lambda b, s: (b, 0, s)),
            pl.BlockSpec((None, tm, LANES), lambda b, s: (b, s, 0)),
            pl.BlockSpec((None, LANES, tm), lambda b, s: (b, 0, s)),
        ],
        out_shape=[
            jax.ShapeDtypeStruct((bsz, seq, CONV_CH + SC_CH), BF16),
            jax.ShapeDtypeStruct((bsz, ATT_W, seq), BF16),
            jax.ShapeDtypeStruct((bsz, seq, ATT_W), BF16),
            jax.ShapeDtypeStruct((bsz, seq // kc, ATT_W, kc), BF16),
            jax.ShapeDtypeStruct((bsz, IDX_HEADS * IDX_DIM, seq), BF16),
            jax.ShapeDtypeStruct((bsz, seq, LANES), BF16),
            jax.ShapeDtypeStruct((bsz, LANES, seq), F32),
        ],
        scratch_shapes=[pltpu.VMEM((tm + CONV_HALO, CONV_CH), F32),
                        pltpu.VMEM((tm + SC_HALO, SC_CH), F32),
                        pltpu.VMEM((SUBLANES - 1, tm + CONV_HALO - SUBLANES, CONV_CH), F32)],
        compiler_params=_cparams(("arbitrary", "arbitrary"), fuse=[False, False, True] + [False] * 7),
        name="in_proj",
    )(x, g, w, cos, sin, dw, dwb, lng, lnb, sw)


def _sortable_bits(x):
    b = lax.bitcast_convert_type(x, I32)
    return b ^ ((b >> 31) | INT_MIN)


def _transpose_bits(rows):
    rows = list(rows)
    j, m = KEY_BITS // 2, 0x0000FFFF
    while j:
        k = 0
        while k < KEY_BITS:
            t = (rows[k] ^ lax.shift_right_logical(rows[k + j], jnp.int32(j))) & m
            rows[k] = rows[k] ^ t
            rows[k + j] = rows[k + j] ^ (t << j)
            k = (k + j + 1) & ~j
        j >>= 1
        m = (m ^ (m << j)) & 0xFFFFFFFF
    return rows


def _dsa_kernel(qT_ref, qiT_ref, wT_ref, k_ref, vT_ref, kidx_ref, o_ref,
                planes_scr, alive_scr, gt_scr, oT_scr, qpad_scr, m_scr, l_scr, ltri_scr, *, kc, topk):
    qb = qT_ref.shape[1]
    i = pl.program_id(1)
    nk = (i * qb) // kc + 1
    q_pos = i * qb + lax.broadcasted_iota(I32, (1, qb), 1)
    row_iota = lax.broadcasted_iota(I32, (kc, qb), 0)

    grp = kc // KEY_BITS
    w = wT_ref[...] * IDX_SCALE
    zeros_half = jnp.zeros((LANES - IDX_DIM, qb), BF16)
    qi = [jnp.concatenate([qiT_ref[h * IDX_DIM:(h + 1) * IDX_DIM, :], zeros_half], axis=0)
          for h in range(IDX_HEADS)]

    def score_chunk(c, diagonal):
        kcs = kidx_ref[pl.ds(pl.multiple_of(c * kc, kc), kc), :]
        sc = jnp.zeros((kc, qb), F32)
        for h in range(IDX_HEADS):
            lg = jnp.dot(kcs, qi[h], preferred_element_type=F32)
            sc = sc + w[h:h + 1, :] * jnp.maximum(lg, 0.0)
        if diagonal:
            sc = jnp.where((c * kc + row_iota) <= q_pos, sc, -jnp.inf)
        u = _sortable_bits(sc)
        planes = _transpose_bits([u[a * grp:(a + 1) * grp, :] for a in range(KEY_BITS)])
        for b in range(KEY_BITS):
            planes_scr[c, b] = planes[b]
        alive_scr[c] = jnp.full((grp, qb), -1, I32)
        gt_scr[c] = jnp.zeros((grp, qb), I32)

    def score_body(c, carry):
        score_chunk(c, False)
        return carry

    lax.fori_loop(0, nk - 1, score_body, 0)
    score_chunk(nk - 1, True)

    def decide(b, acc, cnt_gt, prefix):
        cnt_b = jnp.sum(acc, axis=0, keepdims=True)
        take = (cnt_gt + cnt_b) >= topk
        bit = jnp.left_shift(jnp.int32(1), KEY_BITS - 1 - b)
        return (jnp.where(take, cnt_gt, cnt_gt + cnt_b), prefix | jnp.where(take, bit, 0),
                jnp.where(take, 0, -1) + jnp.zeros((grp, qb), I32))

    def settle(c, b, drop):
        alive = alive_scr[c]
        plane = planes_scr[c, b]
        gt_scr[c] = gt_scr[c] | (alive & plane & drop)
        alive = alive & (plane ^ drop)
        alive_scr[c] = alive
        return alive

    def first_count(c, acc):
        return acc + lax.population_count(planes_scr[c, 0])

    zero_row = jnp.zeros((1, qb), I32)
    acc0 = lax.fori_loop(0, nk, first_count, jnp.zeros((grp, qb), I32))
    state = decide(0, acc0, zero_row, zero_row)

    def bit_step(b, state):
        cnt_gt, prefix, drop = state

        def body(c, acc):
            return acc + lax.population_count(settle(c, b - 1, drop) & planes_scr[c, b])

        acc = lax.fori_loop(0, nk, body, jnp.zeros((grp, qb), I32))
        return decide(b, acc, cnt_gt, prefix)

    cnt_gt, prefix, drop = lax.fori_loop(1, KEY_BITS, bit_step, state)

    def last_settle(c, carry):
        settle(c, KEY_BITS - 1, drop)
        return carry

    lax.fori_loop(0, nk, last_settle, 0)

    tau = prefix ^ INT_MIN
    need = jnp.where(tau > NEG_INF_KEY, topk - cnt_gt, 0).astype(F32)
    half = kc // 2
    ltri_scr[...] = jnp.where(lax.broadcasted_iota(I32, (half, half), 0) >= lax.broadcasted_iota(I32, (half, half), 1),
                              1.0, 0.0).astype(BF16)

    zeros_head = jnp.zeros((HEAD_DIM, qb), BF16)
    for h in range(N_HEADS):
        qh = qT_ref[h * HEAD_DIM:(h + 1) * HEAD_DIM, :]
        qpad_scr[h] = jnp.concatenate([qh, zeros_head] if h % 2 == 0 else [zeros_head, qh], axis=0)
    m_scr[...] = jnp.full((N_HEADS, qb), NEG_BIG, F32)
    l_scr[...] = jnp.zeros((N_HEADS, qb), F32)
    oT_scr[...] = jnp.zeros((ATT_W, qb), F32)

    ones_rows = jnp.ones((16, kc), BF16)

    def att_body(c, run):
        gt_w = gt_scr[c]
        eq_w = alive_scr[c]
        above, equal = [], []
        for a in range(KEY_BITS):
            bit = INT_MIN if a == 0 else 1 << (KEY_BITS - 1 - a)
            above.append(jnp.where((gt_w & bit) != 0, 0.0, -jnp.inf))
            equal.append(jnp.where((eq_w & bit) != 0, 1.0, 0.0))
        equal = jnp.concatenate(equal, axis=0)
        eq16 = equal.astype(BF16)
        rank_lo = jnp.dot(ltri_scr[...], eq16[0:half], preferred_element_type=F32)
        rank_hi = jnp.dot(ltri_scr[...], eq16[half:kc], preferred_element_type=F32) + rank_lo[half - 1:half, :]
        rank = jnp.concatenate([rank_lo, rank_hi], axis=0)
        tie = jnp.where(rank <= need - run, 0.0, -jnp.inf)
        bias = jnp.where(equal > 0.0, tie, jnp.concatenate(above, axis=0))
        row0 = pl.multiple_of(c * kc, kc)
        m_old = m_scr[...]
        l_old = l_scr[...]
        s_all = [jnp.dot(k_ref[pl.ds(row0, kc), (h // 2) * LANES:(h // 2 + 1) * LANES], qpad_scr[h],
                         preferred_element_type=F32) + bias for h in range(N_HEADS)]
        m_new, alphas, p_all = [], [], []
        for h in range(N_HEADS):
            m = m_old[h:h + 1, :]
            mn = jnp.maximum(m, jnp.max(s_all[h], axis=0, keepdims=True))
            p_all.append(jnp.exp2(s_all[h] - mn).astype(BF16))
            alphas.append(jnp.exp2(m - mn))
            m_new.append(mn)
        pv = [jnp.dot(jnp.concatenate([vT_ref[c, h * HEAD_DIM:(h + 1) * HEAD_DIM, :], ones_rows], axis=0),
                      p_all[h], preferred_element_type=F32) for h in range(N_HEADS)]
        alpha_rows = [jnp.broadcast_to(a, (HEAD_DIM, qb)) for a in alphas]
        oT_scr[...] = (jnp.concatenate(alpha_rows, axis=0) * oT_scr[...]
                       + jnp.concatenate([x[0:HEAD_DIM] for x in pv], axis=0))
        m_scr[...] = jnp.concatenate(m_new, axis=0)
        l_scr[...] = (jnp.concatenate(alphas, axis=0) * l_old
                      + jnp.concatenate([x[HEAD_DIM:HEAD_DIM + 1] for x in pv], axis=0))
        return run + rank[kc - 1:kc, :]

    lax.fori_loop(0, nk, att_body, jnp.zeros((1, qb), F32))
    for h in range(N_HEADS):
        hs = slice(h * HEAD_DIM, (h + 1) * HEAD_DIM)
        oT_scr[hs, :] = oT_scr[hs, :] / l_scr[h:h + 1, :]

    o_ref[...] = oT_scr[...].T.astype(BF16)


def _dsa(qT, qiT, kwT, k, vT, kidx, *, qb, kc):
    bsz, seq, _ = k.shape
    assert kc % qb == 0 and seq % kc == 0
    topk = min(TOPK_MAX, seq // 4)
    w_row_block = IDX_DIM // 8
    return pl.pallas_call(
        functools.partial(_dsa_kernel, kc=kc, topk=topk),
        grid=(bsz, seq // qb),
        in_specs=[
            pl.BlockSpec((None, ATT_W, qb), lambda b, i: (b, 0, i)),
            pl.BlockSpec((None, IDX_HEADS * IDX_DIM, qb), lambda b, i: (b, 0, i)),
            pl.BlockSpec((None, 8, qb), lambda b, i: (b, w_row_block, i)),
            pl.BlockSpec((None, seq, ATT_W), lambda b, i: (b, 0, 0), pipeline_mode=pl.Buffered(1)),
            pl.BlockSpec((None, seq // kc, ATT_W, kc), lambda b, i: (b, 0, 0, 0), pipeline_mode=pl.Buffered(1)),
            pl.BlockSpec((None, seq, LANES), lambda b, i: (b, 0, 0), pipeline_mode=pl.Buffered(1)),
        ],
        out_specs=pl.BlockSpec((None, qb, ATT_W), lambda b, i: (b, i, 0)),
        out_shape=jax.ShapeDtypeStruct((bsz, seq, ATT_W), BF16),
        scratch_shapes=[pltpu.VMEM((seq // kc, KEY_BITS, kc // KEY_BITS, qb), I32),
                        pltpu.VMEM((seq // kc, kc // KEY_BITS, qb), I32),
                        pltpu.VMEM((seq // kc, kc // KEY_BITS, qb), I32),
                        pltpu.VMEM((ATT_W, qb), F32),
                        pltpu.VMEM((N_HEADS, LANES, qb), BF16),
                        pltpu.VMEM((N_HEADS, qb), F32),
                        pltpu.VMEM((N_HEADS, qb), F32),
                        pltpu.VMEM((kc // 2, kc // 2), BF16)],
        compiler_params=_cparams(("parallel", "parallel")),
        name="dsa_attention",
    )(qT, qiT, kwT, k, vT, kidx)


def _memkv_kernel(mem_ref, g_ref, wkv_ref, kT_ref, v_ref):
    d = mem_ref.shape[1]
    mn = _rms(mem_ref[...], g_ref[...]).astype(BF16)
    kv = jnp.dot(mn, wkv_ref[...], preferred_element_type=F32)
    kT_ref[...] = kv[:, 0:d].T.astype(BF16)
    v_ref[...] = kv[:, d:2 * d].astype(BF16)


def _memkv(mem, g, wkv):
    bsz, m, d = mem.shape
    return pl.pallas_call(
        _memkv_kernel,
        grid=(bsz,),
        in_specs=[pl.BlockSpec((None, m, d), lambda b: (b, 0, 0)),
                  pl.BlockSpec((1, d), lambda b: (0, 0)),
                  pl.BlockSpec((d, 2 * d), lambda b: (0, 0))],
        out_specs=[pl.BlockSpec((None, d, m), lambda b: (b, 0, 0)),
                   pl.BlockSpec((None, m, d), lambda b: (b, 0, 0))],
        out_shape=[jax.ShapeDtypeStruct((bsz, d, m), BF16), jax.ShapeDtypeStruct((bsz, m, d), BF16)],
        compiler_params=_cparams(("parallel",), fuse=[False, False, True]),
        name="mem_kv",
    )(mem, g, wkv)


def _xattn_kernel(x_ref, yab_ref, yc_ref, wmix_ref, g_ref, wq_ref, kT_ref, v_ref, wo_ref, o_ref):
    d = x_ref.shape[1]
    hd = d // XA_HEADS
    n_ab = yab_ref.shape[1]
    x = (x_ref[...]
         + jnp.dot(yab_ref[...], wmix_ref[0:n_ab, :], preferred_element_type=F32)
         + jnp.dot(yc_ref[...], wmix_ref[n_ab:, :], preferred_element_type=F32))
    hq = _rms(x, g_ref[...]).astype(BF16)
    q = (jnp.dot(hq, wq_ref[...], preferred_element_type=F32) * (hd ** -0.5)).astype(BF16)
    outs = []
    for h in range(XA_HEADS):
        s = jnp.dot(q[:, h * hd:(h + 1) * hd], kT_ref[h * hd:(h + 1) * hd, :], preferred_element_type=F32)
        m = jnp.max(s, axis=-1, keepdims=True)
        p = jnp.exp(s - m)
        l = jnp.sum(p, axis=-1, keepdims=True)
        o = jnp.dot(p.astype(BF16), v_ref[:, h * hd:(h + 1) * hd], preferred_element_type=F32) / l
        outs.append(o.astype(BF16))
    o_ref[...] = x + jnp.dot(jnp.concatenate(outs, axis=-1), wo_ref[...], preferred_element_type=F32)


def _xattn(x, yab, yc, wmix, g, wq, kT, v, wo, *, bsz, seq, tm):
    d = x.shape[1]
    m = v.shape[1]
    ns = seq // tm
    tok_c = lambda c: pl.BlockSpec((tm, c), lambda b, s: (b * ns + s, 0))
    tok = tok_c(d)
    full = lambda a: pl.BlockSpec(a.shape, lambda b, s: (0, 0))
    return pl.pallas_call(
        _xattn_kernel,
        grid=(bsz, ns),
        in_specs=[tok, tok_c(yab.shape[1]), tok_c(yc.shape[1]), full(wmix),
                  pl.BlockSpec((1, d), lambda b, s: (0, 0)),
                  pl.BlockSpec((d, d), lambda b, s: (0, 0)),
                  pl.BlockSpec((None, d, m), lambda b, s: (b, 0, 0)),
                  pl.BlockSpec((None, m, d), lambda b, s: (b, 0, 0)),
                  pl.BlockSpec((d, d), lambda b, s: (0, 0))],
        out_specs=tok,
        out_shape=jax.ShapeDtypeStruct((bsz * seq, d), F32),
        compiler_params=_cparams(("parallel", "parallel"),
                                 fuse=[False, False, False, True, False, True, False, False, True]),
        name="mem_xattn",
    )(x, yab, yc, wmix, g, wq, kT, v, wo)


def _tiles(seq):
    tiles = dict(
        tm_ffn=512,
        tm_proj=512, rc_conv=64,
        qb=512, kc=512,
        tm_xa=1024, tm_rope=1024,
    )
    assert all(seq % tiles[name] == 0 for name in ("tm_ffn", "tm_proj", "qb", "kc", "tm_xa", "tm_rope")), seq
    return tiles


def kernel(x, mem, positions, ffn1_norm, ffn1_w_gate, ffn1_w_up, ffn1_w_down, mix_norm, w_in,
           conf_dw, conf_dw_b, conf_ln_g, conf_ln_b, sc_dw, w_out, xa_norm, mem_norm,
           xa_wq, xa_wkv, xa_wo, ffn2_norm, ffn2_w_gate, ffn2_w_up, ffn2_w_down, final_norm):
    bsz, seq, d = x.shape
    depth = w_in.shape[0]
    t = bsz * seq
    tl = _tiles(seq)
    bf = lambda a: a.astype(BF16)
    row = lambda a: a.reshape(1, -1)

    cos, sin = _rope_tables(positions, tl["tm_rope"])
    xf = x.reshape(t, d)
    pad_cols = ((0, 0), (0, N_IN_PAD - w_in.shape[2]))

    for l in range(depth):
        xf = _ffn(xf, row(ffn1_norm[l]), bf(ffn1_w_gate[l]), bf(ffn1_w_up[l]), bf(ffn1_w_down[l]),
                  row(final_norm), final=False, tm=tl["tm_ffn"])

        yab, qT, k, vT, qiT, kidx, kwT = _proj(
            xf, row(mix_norm[l]), jnp.pad(bf(w_in[l]), pad_cols), cos, sin,
            conf_dw[l], row(conf_dw_b[l]), row(conf_ln_g[l]), row(conf_ln_b[l]), sc_dw[l],
            bsz=bsz, seq=seq, tm=tl["tm_proj"], kc=tl["kc"], rc=tl["rc_conv"])
        yc = _dsa(qT, qiT, kwT, k, vT, kidx, qb=tl["qb"], kc=tl["kc"])

        kT_mem, v_mem = _memkv(mem, row(mem_norm[l]), bf(xa_wkv[l]))
        xf = _xattn(xf, yab.reshape(t, CONV_CH + SC_CH), yc.reshape(t, ATT_W), bf(w_out[l]),
                    row(xa_norm[l]), bf(xa_wq[l]), kT_mem, v_mem, bf(xa_wo[l]),
                    bsz=bsz, seq=seq, tm=tl["tm_xa"])

        xf = _ffn(xf, row(ffn2_norm[l]), bf(ffn2_w_gate[l]), bf(ffn2_w_up[l]), bf(ffn2_w_down[l]),
                  row(final_norm), final=(l == depth - 1), tm=tl["tm_ffn"])

    return xf.reshape(bsz, seq, d)
```
